```python
import numpy as np
import jax, jax.numpy as jnp
from jax import lax

D_MODEL = 1024
BATCH = 8
SEQ = 2048
DEPTH = 1
DEC_BATCH = 128
DEC_SEQ = 4
PAST_LEN = 16384
PAGE_SIZE = 128

ML_HEADS = 4
ML_DQK = 128
ML_DV = 256
ML_CHUNK = 128
GM_GROUPS = 4
GM_DG = 256
GM_CHUNK = 128
MOE_GROUPS = 4
MOE_PER_GROUP = 8
MOE_EXPERTS = MOE_GROUPS * MOE_PER_GROUP
MOE_TOPK = 2
MOE_HIDDEN = 256
EPS = 1e-6
SPLITS = (ML_HEADS * ML_DQK, ML_HEADS * ML_DQK, ML_HEADS * ML_DV, ML_HEADS * ML_DV, ML_HEADS, ML_HEADS, GM_GROUPS * GM_DG, GM_GROUPS * GM_DG, D_MODEL, D_MODEL)
IN_WIDTH = 2 * ML_HEADS * ML_DQK + 2 * ML_HEADS * ML_DV + 2 * ML_HEADS + 2 * GM_GROUPS * GM_DG + 2 * D_MODEL

kernel_name = 'hybrid_mlstm_gmlp_hmoe_step'


def rmsnorm(x, g):
    xf = x.astype(jnp.float32)
    y = xf * lax.rsqrt(jnp.mean(xf * xf, axis=-1, keepdims=True) + EPS)
    return (y * g.astype(jnp.float32)).astype(x.dtype)


def mlstm_chunk(carry, inp):
    c0, n0, m0 = carry
    q, k, v, logi, logf = inp
    L = q.shape[2]
    b = jnp.cumsum(logf, axis=-1)
    causal = jnp.tril(jnp.ones((L, L), dtype=bool))
    d = jnp.where(causal, b[..., :, None] - b[..., None, :] + logi[..., None, :], -jnp.inf)
    inter = b + m0[..., None]
    m = jnp.maximum(inter, jnp.max(d, axis=-1))
    w_intra = jnp.exp(d - m[..., None])
    w_inter = jnp.exp(inter - m)
    s = jnp.einsum('bhtd,bhsd->bhts', q, k) * w_intra
    num = jnp.einsum('bhts,bhsv->bhtv', s, v) + w_inter[..., None] * jnp.einsum('bhvd,bhtd->bhtv', c0, q)
    den = jnp.sum(s, axis=-1) + w_inter * jnp.einsum('bhd,bhtd->bht', n0, q)
    h = num / jnp.maximum(jnp.abs(den), jnp.exp(-m))[..., None]
    b_last = b[..., -1]
    w_end = b_last[..., None] - b + logi
    m_new = jnp.maximum(b_last + m0, jnp.max(w_end, axis=-1))
    decay = jnp.exp(b_last + m0 - m_new)
    w_end = jnp.exp(w_end - m_new[..., None])
    c_new = decay[..., None, None] * c0 + jnp.einsum('bhs,bhsv,bhsd->bhvd', w_end, v, k)
    n_new = decay[..., None] * n0 + jnp.einsum('bhs,bhsd->bhd', w_end, k)
    return (c_new, n_new, m_new), h


def mixer(h, w_in, b_i, b_f, ml_g, gm_g, gm_ws, gm_bs, p_a, p_b, w_out, ml_state, prompt):
    f32 = jnp.float32
    nb, t, _ = h.shape
    z = h @ w_in
    q, k, v, og, ig, fg, u, gv, ga, gb = jnp.split(z, np.cumsum(SPLITS)[:-1].tolist(), axis=-1)

    def heads(a, dh):
        return a.reshape(nb, t, ML_HEADS, dh).transpose(0, 2, 1, 3).astype(f32)
    q = heads(q, ML_DQK) * (ML_DQK ** -0.5)
    k = heads(k, ML_DQK)
    v = heads(v, ML_DV)
    logi = (ig + b_i).astype(f32).transpose(0, 2, 1)
    logf = jax.nn.log_sigmoid((fg + b_f).astype(f32)).transpose(0, 2, 1)
    state0 = (ml_state[0].astype(f32), ml_state[1].astype(f32), ml_state[2].astype(f32))
    if prompt:
        nc = t // ML_CHUNK
        def chunks(a):
            a = a.reshape(a.shape[:2] + (nc, ML_CHUNK) + a.shape[3:])
            return jnp.moveaxis(a, 2, 0)
        state, hs = lax.scan(mlstm_chunk, state0, (chunks(q), chunks(k), chunks(v), chunks(logi), chunks(logf)))
        h_ml = jnp.moveaxis(hs, 0, 2).reshape(nb, ML_HEADS, t, ML_DV)
    else:
        state, h_ml = mlstm_chunk(state0, (q, k, v, logi, logf))
    h_ml = rmsnorm(h_ml, ml_g[:, None, :]).transpose(0, 2, 1, 3).reshape(nb, t, ML_HEADS * ML_DV)
    h_ml = (jax.nn.sigmoid(og.astype(f32)) * h_ml).astype(h.dtype)

    u = jax.nn.gelu(u).reshape(nb, t, GM_GROUPS, GM_DG)
    gv = rmsnorm(jax.nn.gelu(gv).reshape(nb, t, GM_GROUPS, GM_DG), gm_g)
    ws = jnp.where(jnp.tril(jnp.ones((GM_CHUNK, GM_CHUNK), dtype=bool)), gm_ws, 0.0).astype(gv.dtype)
    if prompt:
        nc = t // GM_CHUNK
        mixed = jnp.einsum('gts,bcsgd->bctgd', ws, gv.reshape(nb, nc, GM_CHUNK, GM_GROUPS, GM_DG))
        mixed = (mixed + gm_bs.T[:, :, None]).reshape(nb, t, GM_GROUPS, GM_DG)
        v_rows = None
    else:
        mixed = jnp.einsum('gts,bsgd->btgd', ws[:, :t, :t], gv) + gm_bs[:, :t].T[:, :, None]
        v_rows = gv
    h_gm = (u * mixed).reshape(nb, t, GM_GROUPS * GM_DG)

    merged = jax.nn.sigmoid(ga) * (h_ml @ p_a) + jax.nn.sigmoid(gb) * (h_gm @ p_b)
    return merged @ w_out, state, v_rows


def hier_moe(x, rc_w, rc_b, rf_w, rf_b, e_wg, e_wu, e_wd):
    f32 = jnp.float32
    shape = x.shape
    xt = x.reshape(-1, shape[-1])
    coarse = jax.nn.softmax((xt @ rc_w).astype(f32) + rc_b.astype(f32), axis=-1)
    grp = jnp.argmax(coarse, axis=-1)
    p_grp = jnp.max(coarse, axis=-1)
    fine_all = jnp.einsum('nd,gde->nge', xt, rf_w).astype(f32) + rf_b.astype(f32)
    fine = fine_all[jnp.arange(xt.shape[0]), grp]
    top_v, top_i = lax.top_k(fine, MOE_TOPK)
    gate = jax.nn.softmax(top_v, axis=-1) * p_grp[:, None]
    eid = grp[:, None] * MOE_PER_GROUP + top_i
    combine = jnp.sum(jax.nn.one_hot(eid, MOE_EXPERTS, dtype=f32) * gate[..., None], axis=1)

    def expert(acc, ew):
        wg, wu, wd, c = ew
        hid = jax.nn.silu(xt @ wg) * (xt @ wu)
        return acc + c[:, None] * (hid @ wd).astype(f32), None

    out, _ = lax.scan(expert, jnp.zeros(xt.shape, f32), (e_wg, e_wu, e_wd, combine.T))
    return out.astype(x.dtype).reshape(shape)


def trunk(x, c, n, m, prompt, weights):
    (norm_mix_g, w_in, ml_b_i, ml_b_f, ml_norm_g, gm_norm_g, gm_ws, gm_bs, p_a, p_b, w_out,
     norm_ffn_g, rc_w, rc_b, rf_w, rf_b, e_wg, e_wu, e_wd, final_norm_g) = weights
    cs, ns, ms, vs = [], [], [], []
    for l in range(DEPTH):
        y, st, v_rows = mixer(rmsnorm(x, norm_mix_g[l]), w_in[l], ml_b_i[l], ml_b_f[l], ml_norm_g[l],
                              gm_norm_g[l], gm_ws[l], gm_bs[l], p_a[l], p_b[l], w_out[l],
                              (c[l], n[l], m[l]), prompt)
        x = x + y
        x = x + hier_moe(rmsnorm(x, norm_ffn_g[l]), rc_w[l], rc_b[l], rf_w[l], rf_b[l], e_wg[l], e_wu[l], e_wd[l])
        cs.append(st[0])
        ns.append(st[1])
        ms.append(st[2])
        vs.append(v_rows)
    v_out = None if prompt else jnp.stack(vs)
    return rmsnorm(x, final_norm_g), jnp.stack(cs), jnp.stack(ns), jnp.stack(ms), v_out


def setup_inputs(seed: int = 0) -> dict:
    key = jax.random.key(seed)
    ks = jax.random.split(key, 28)
    f32 = jnp.float32

    def nrm(k, shape, scale):
        return scale * jax.random.normal(k, shape, f32)

    mw = ML_HEADS * ML_DV
    gw = GM_GROUPS * GM_DG
    return {
        'x_prompt': nrm(ks[0], (BATCH, SEQ, D_MODEL), 1.0),
        'x_sample': nrm(ks[1], (DEC_BATCH, DEC_SEQ, D_MODEL), 1.0),
        'state_mlstm_C': nrm(ks[2], (DEPTH, DEC_BATCH, ML_HEADS, ML_DV, ML_DQK), 0.05),
        'state_mlstm_n': nrm(ks[3], (DEPTH, DEC_BATCH, ML_HEADS, ML_DQK), 0.5),
        'state_mlstm_m': nrm(ks[4], (DEPTH, DEC_BATCH, ML_HEADS), 1.0),
        'norm_mix_g': 1.0 + nrm(ks[5], (DEPTH, D_MODEL), 0.02),
        'w_in': nrm(ks[6], (DEPTH, D_MODEL, IN_WIDTH), D_MODEL ** -0.5),
        'ml_b_i': nrm(ks[7], (DEPTH, ML_HEADS), 0.5),
        'ml_b_f': 3.0 + nrm(ks[8], (DEPTH, ML_HEADS), 0.5),
        'ml_norm_g': 1.0 + nrm(ks[9], (DEPTH, ML_HEADS, ML_DV), 0.02),
        'gm_norm_g': 1.0 + nrm(ks[10], (DEPTH, GM_GROUPS, GM_DG), 0.02),
        'gm_ws': nrm(ks[11], (DEPTH, GM_GROUPS, GM_CHUNK, GM_CHUNK), 0.5 * GM_CHUNK ** -0.5),
        'gm_bs': 1.0 + nrm(ks[12], (DEPTH, GM_GROUPS, GM_CHUNK), 0.1),
        'p_a': nrm(ks[13], (DEPTH, mw, D_MODEL), mw ** -0.5),
        'p_b': nrm(ks[14], (DEPTH, gw, D_MODEL), gw ** -0.5),
        'w_out': nrm(ks[15], (DEPTH, D_MODEL, D_MODEL), D_MODEL ** -0.5),
        'norm_ffn_g': 1.0 + nrm(ks[16], (DEPTH, D_MODEL), 0.02),
        'rc_w': nrm(ks[17], (DEPTH, D_MODEL, MOE_GROUPS), D_MODEL ** -0.5),
        'rc_b': nrm(ks[18], (DEPTH, MOE_GROUPS), 0.01),
        'rf_w': nrm(ks[19], (DEPTH, MOE_GROUPS, D_MODEL, MOE_PER_GROUP), D_MODEL ** -0.5),
        'rf_b': nrm(ks[20], (DEPTH, MOE_GROUPS, MOE_PER_GROUP), 0.01),
        'e_wg': nrm(ks[21], (DEPTH, MOE_EXPERTS, D_MODEL, MOE_HIDDEN), D_MODEL ** -0.5),
        'e_wu': nrm(ks[22], (DEPTH, MOE_EXPERTS, D_MODEL, MOE_HIDDEN), D_MODEL ** -0.5),
        'e_wd': nrm(ks[23], (DEPTH, MOE_EXPERTS, MOE_HIDDEN, D_MODEL), MOE_HIDDEN ** -0.5),
        'final_norm_g': 1.0 + nrm(ks[24], (D_MODEL,), 0.02),
    }


def reference(x_prompt, x_sample, state_mlstm_C, state_mlstm_n, state_mlstm_m, norm_mix_g, w_in, ml_b_i,
              ml_b_f, ml_norm_g, gm_norm_g, gm_ws, gm_bs, p_a, p_b, w_out, norm_ffn_g, rc_w, rc_b, rf_w, rf_b,
              e_wg, e_wu, e_wd, final_norm_g):
    weights = (norm_mix_g, w_in, ml_b_i, ml_b_f, ml_norm_g, gm_norm_g, gm_ws, gm_bs, p_a, p_b, w_out,
               norm_ffn_g, rc_w, rc_b, rf_w, rf_b, e_wg, e_wu, e_wd, final_norm_g)
    f32 = jnp.float32
    nbp = x_prompt.shape[0]
    zc = jnp.zeros((DEPTH, nbp, ML_HEADS, ML_DV, ML_DQK), f32)
    zn = jnp.zeros((DEPTH, nbp, ML_HEADS, ML_DQK), f32)
    zm = jnp.zeros((DEPTH, nbp, ML_HEADS), f32)
    y_prompt, c_p, n_p, m_p, _ = trunk(x_prompt, zc, zn, zm, True, weights)
    y_sample, c_s, n_s, m_s, v_s = trunk(x_sample, state_mlstm_C, state_mlstm_n, state_mlstm_m, False, weights)
    return (y_prompt, y_sample, c_p, n_p, m_p, c_s, n_s, m_s, v_s)
```

```python
import functools

import numpy as np
import jax
import jax.numpy as jnp
from jax import lax
from jax.experimental import pallas as pl
from jax.experimental.pallas import tpu as pltpu

D_MODEL = 1024
ML_HEADS = 4
ML_DQK = 128
ML_DV = 256
ML_CHUNK = 128
GM_GROUPS = 4
GM_DG = 256
MOE_GROUPS = 4
MOE_PER_GROUP = 8
MOE_EXPERTS = MOE_GROUPS * MOE_PER_GROUP
MOE_HIDDEN = 256
EPS = 1e-6

LANES = 128
SAMPLE_PAD_T = 8
NEG = -1e30
VMEM_LIMIT = 56 * 1024 * 1024

ZB_QK, ZB_V, ZB_OG, ZB_U, ZB_GV, ZB_GA, ZB_GB = range(7)
N_ZB = 7
GATE_OFF = 2 * ML_HEADS * ML_DQK + 2 * ML_HEADS * ML_DV

f32 = jnp.float32
bf16 = jnp.bfloat16


def _sigmoid(x):
    return 1.0 / (1.0 + jnp.exp(-x))


def _log_sigmoid(x):
    return jnp.minimum(x, 0.0) - jnp.log1p(jnp.exp(-jnp.abs(x)))


def _gelu_tanh(x):
    return 0.5 * x * (1.0 + jnp.tanh(np.sqrt(2.0 / np.pi) * (x + 0.044715 * (x * x * x))))


def _rms(x, g):
    return x * lax.rsqrt(jnp.mean(x * x, axis=-1, keepdims=True) + EPS) * g


def _dot(a, b):
    return jnp.dot(a, b, preferred_element_type=f32)


def _dot_nt(a, b):
    return lax.dot_general(a, b, (((1,), (1,)), ((), ())), preferred_element_type=f32)


def _dot_tn(a, b):
    return lax.dot_general(a, b, (((0,), (0,)), ((), ())), preferred_element_type=f32)


def _in_proj_kernel(x_ref, g_ref, w_ref, wg_ref, z_ref, zg_ref, zgt_ref, xn_scr):
    j = pl.program_id(1)

    @pl.when(j == 0)
    def _():
        xn = _rms(x_ref[...], g_ref[...]).astype(bf16)
        xn_scr[...] = xn
        zg = _dot(xn, wg_ref[...])
        zg_ref[...] = zg
        zgt_ref[...] = zg.T[:2 * ML_HEADS, :]

    z_ref[...] = _dot(xn_scr[...], w_ref[...]).astype(z_ref.dtype)


def _in_proj(x, g, w_main, w_gate, tm, z_dtype):
    n = x.shape[0]
    return pl.pallas_call(
        _in_proj_kernel,
        grid=(n // tm, N_ZB),
        in_specs=[
            pl.BlockSpec((tm, D_MODEL), lambda i, j: (i, 0)),
            pl.BlockSpec((1, D_MODEL), lambda i, j: (0, 0)),
            pl.BlockSpec((D_MODEL, D_MODEL), lambda i, j: (0, j)),
            pl.BlockSpec((D_MODEL, LANES), lambda i, j: (0, 0)),
        ],
        out_specs=[
            pl.BlockSpec((tm, D_MODEL), lambda i, j: (i, j)),
            pl.BlockSpec((tm, LANES), lambda i, j: (i, 0)),
            pl.BlockSpec((2 * ML_HEADS, tm), lambda i, j: (0, i)),
        ],
        out_shape=[
            jax.ShapeDtypeStruct((n, N_ZB * D_MODEL), z_dtype),
            jax.ShapeDtypeStruct((n, LANES), f32),
            jax.ShapeDtypeStruct((2 * ML_HEADS, n), f32),
        ],
        scratch_shapes=[pltpu.VMEM((tm, D_MODEL), bf16)],
        compiler_params=pltpu.CompilerParams(
            dimension_semantics=("parallel", "arbitrary"), vmem_limit_bytes=VMEM_LIMIT),
        name="in_proj",
    )(x, g, w_main, w_gate)


def _mixer_core_kernel(bi_ref, bf_ref, zqk_ref, zv_ref, zog_ref, zu_ref, zgv_ref, zg_ref, zgt_ref,
                       c0_ref, n0_ref, m0_ref, mlg_ref, gmg_ref, ws_ref, bst_ref,
                       hml_ref, hgm_ref, c_ref, n_ref, m_ref, *maybe_v_ref, nb, L, valid, cd):
    c = pl.program_id(1)

    @pl.when(c == 0)
    def _():
        c_ref[...] = c0_ref[...]
        n_ref[...] = n0_ref[...]
        m_ref[...] = m0_ref[...]

    tt = lax.broadcasted_iota(jnp.int32, (L, L), 0)
    ss = lax.broadcasted_iota(jnp.int32, (L, L), 1)
    s_ok = ss < valid
    causal = (ss <= tt) & s_ok
    upper = tt <= ss
    col_ok = lax.broadcasted_iota(jnp.int32, (L, 1), 0) < valid
    row_ok = lax.broadcasted_iota(jnp.int32, (1, L), 1) < valid
    scale = ML_DQK ** -0.5

    for b in range(nb):
        for h in range(ML_HEADS):
            q = zqk_ref[b, :, h * ML_DQK:(h + 1) * ML_DQK]
            k = zqk_ref[b, :, (ML_HEADS + h) * ML_DQK:(ML_HEADS + h + 1) * ML_DQK]
            v = zv_ref[b, :, h * ML_DV:(h + 1) * ML_DV]
            qf = q.astype(f32)
            kf = k.astype(f32)
            vf = v.astype(f32)
            logi_col = zg_ref[b, :, h:h + 1] + bi_ref[h]
            logi_row = zgt_ref[b, h:h + 1, :] + bi_ref[h]
            logf_col = jnp.where(col_ok, _log_sigmoid(zg_ref[b, :, ML_HEADS + h:ML_HEADS + h + 1] + bf_ref[h]), 0.0)
            logf_row = jnp.where(row_ok, _log_sigmoid(zgt_ref[b, ML_HEADS + h:ML_HEADS + h + 1, :] + bf_ref[h]), 0.0)
            c0 = c_ref[b, h]
            n0 = n_ref[b, h:h + 1, :]
            m0 = m_ref[b, h:h + 1, 0:1]

            b_col = jnp.sum(jnp.where(causal, logf_row, 0.0), axis=1, keepdims=True)
            b_row = jnp.sum(jnp.where(upper, logf_col, 0.0), axis=0, keepdims=True)
            d = jnp.where(causal, b_col - b_row + logi_row, NEG)
            inter = b_col + m0
            m_col = jnp.maximum(inter, jnp.max(d, axis=1, keepdims=True))
            w_intra = jnp.exp(d - m_col) * scale
            w_inter = jnp.exp(inter - m_col) * scale
            s = _dot_nt(q.astype(cd), k.astype(cd)) * w_intra
            num = _dot(s.astype(cd), v.astype(cd)) + w_inter * _dot_nt(q.astype(cd), c0.astype(cd))
            den = jnp.sum(s, axis=1, keepdims=True) + w_inter * jnp.sum(qf * n0, axis=1, keepdims=True)
            hh = num / jnp.maximum(jnp.abs(den), jnp.exp(-m_col))
            hh = _rms(hh, mlg_ref[h:h + 1, :])
            og = zog_ref[b, :, h * ML_DV:(h + 1) * ML_DV].astype(f32)
            hml_ref[b, :, h * ML_DV:(h + 1) * ML_DV] = (_sigmoid(og) * hh).astype(hml_ref.dtype)

            b_last = b_row[:, valid - 1:valid]
            wend = jnp.where(col_ok, b_last - b_col + logi_col, NEG)
            m_new = jnp.maximum(b_last + m0, jnp.max(wend, axis=0, keepdims=True))
            decay = jnp.exp(b_last + m0 - m_new)
            wend = jnp.exp(wend - m_new)
            c_ref[b, h] = decay * c0 + _dot_tn((vf * wend).astype(cd), k.astype(cd))
            n_ref[b, h:h + 1, :] = decay * n0 + jnp.sum(kf * wend, axis=0, keepdims=True)
            m_ref[b, h:h + 1, :] = jnp.broadcast_to(m_new, (1, LANES))

        for g in range(GM_GROUPS):
            sl = slice(g * GM_DG, (g + 1) * GM_DG)
            u = _gelu_tanh(zu_ref[b, :, sl].astype(f32))
            gv = _rms(_gelu_tanh(zgv_ref[b, :, sl].astype(f32)), gmg_ref[g:g + 1, :])
            if maybe_v_ref:
                maybe_v_ref[0][b, :, sl] = gv
            w = jnp.where(ss <= tt, ws_ref[g], 0.0)
            mixed = _dot(w.astype(cd), gv.astype(cd)) + bst_ref[:, g:g + 1]
            hgm_ref[b, :, sl] = (u * mixed).astype(hgm_ref.dtype)


def _mixer_core(z3, zg3, zgt3, c0, n0, m0, b_i, b_f, ml_g, gm_g, ws, bst, *, nb, L, valid, cd, emit_v, h_dtype):
    nbatch, t, _ = z3.shape
    nc = t // L
    zspec = lambda blk: pl.BlockSpec((nb, L, D_MODEL), lambda b, c, blk=blk: (b, c, blk))
    full = lambda shape: pl.BlockSpec(shape, lambda b, c: (0,) * len(shape))
    smem = pl.BlockSpec(memory_space=pltpu.SMEM)
    state_specs = [
        pl.BlockSpec((nb, ML_HEADS, ML_DV, ML_DQK), lambda b, c: (b, 0, 0, 0)),
        pl.BlockSpec((nb, ML_HEADS, ML_DQK), lambda b, c: (b, 0, 0)),
        pl.BlockSpec((nb, ML_HEADS, LANES), lambda b, c: (b, 0, 0)),
    ]
    tok_spec = pl.BlockSpec((nb, L, D_MODEL), lambda b, c: (b, c, 0))
    out_specs = [tok_spec, tok_spec] + state_specs
    out_shape = [
        jax.ShapeDtypeStruct((nbatch, t, D_MODEL), h_dtype),
        jax.ShapeDtypeStruct((nbatch, t, D_MODEL), h_dtype),
        jax.ShapeDtypeStruct((nbatch, ML_HEADS, ML_DV, ML_DQK), f32),
        jax.ShapeDtypeStruct((nbatch, ML_HEADS, ML_DQK), f32),
        jax.ShapeDtypeStruct((nbatch, ML_HEADS, LANES), f32),
    ]
    if emit_v:
        out_specs.append(tok_spec)
        out_shape.append(jax.ShapeDtypeStruct((nbatch, t, D_MODEL), f32))
    return pl.pallas_call(
        functools.partial(_mixer_core_kernel, nb=nb, L=L, valid=valid, cd=cd),
        grid=(nbatch // nb, nc),
        in_specs=[smem, smem,
                  zspec(ZB_QK), zspec(ZB_V), zspec(ZB_OG), zspec(ZB_U), zspec(ZB_GV),
                  pl.BlockSpec((nb, L, LANES), lambda b, c: (b, c, 0)),
                  pl.BlockSpec((nb, 2 * ML_HEADS, L), lambda b, c: (b, 0, c)),
                  *state_specs,
                  full((ML_HEADS, ML_DV)), full((GM_GROUPS, GM_DG)), full((GM_GROUPS, L, L)),
                  full((L, GM_GROUPS))],
        out_specs=out_specs,
        out_shape=out_shape,
        compiler_params=pltpu.CompilerParams(
            dimension_semantics=("parallel", "arbitrary"), vmem_limit_bytes=VMEM_LIMIT),
        name="mixer_core",
    )(b_i, b_f, z3, z3, z3, z3, z3, zg3, zgt3, c0, n0, m0, ml_g, gm_g, ws, bst)


def _proj_router_kernel(hml_ref, hgm_ref, ga_ref, gb_ref, x_ref, pa_ref, pb_ref, wo_ref, g_ref, wr_ref, br_ref,
                        x2_ref, xn_ref, comb_ref):
    a = _dot(hml_ref[...].astype(bf16), pa_ref[...])
    b = _dot(hgm_ref[...].astype(bf16), pb_ref[...])
    merged = _sigmoid(ga_ref[...].astype(f32)) * a + _sigmoid(gb_ref[...].astype(f32)) * b
    x2 = x_ref[...] + _dot(merged.astype(bf16), wo_ref[...])
    x2_ref[...] = x2
    xn = _rms(x2, g_ref[...]).astype(bf16)
    xn_ref[...] = xn

    lg = _dot(xn, wr_ref[...]) + br_ref[...]
    lane = lax.broadcasted_iota(jnp.int32, lg.shape, 1).astype(f32)
    cmask = (lane >= MOE_EXPERTS) & (lane < MOE_EXPERTS + MOE_GROUPS)
    cl = jnp.where(cmask, lg, NEG)
    cmax = jnp.max(cl, axis=1, keepdims=True)
    p_grp = 1.0 / jnp.sum(jnp.where(cmask, jnp.exp(cl - cmax), 0.0), axis=1, keepdims=True)
    grp = jnp.min(jnp.where(cl == cmax, lane, 2.0 * LANES), axis=1, keepdims=True) - MOE_EXPERTS
    fmask = (lane >= grp * MOE_PER_GROUP) & (lane < (grp + 1.0) * MOE_PER_GROUP)
    fl = jnp.where(fmask, lg, NEG)
    v1 = jnp.max(fl, axis=1, keepdims=True)
    i1 = jnp.min(jnp.where(fl == v1, lane, 2.0 * LANES), axis=1, keepdims=True)
    fl2 = jnp.where(lane == i1, NEG, fl)
    v2 = jnp.max(fl2, axis=1, keepdims=True)
    i2 = jnp.min(jnp.where(fl2 == v2, lane, 2.0 * LANES), axis=1, keepdims=True)
    e2 = jnp.exp(v2 - v1)
    g1 = p_grp / (1.0 + e2)
    g2 = p_grp * e2 / (1.0 + e2)
    comb_ref[...] = jnp.where(lane == i1, g1, 0.0) + jnp.where(lane == i2, g2, 0.0)


def _proj_router(hml, hgm, z, x, pa, pb, wo, g, wr, br, tm):
    n = x.shape[0]
    row = lambda blk=0: pl.BlockSpec((tm, D_MODEL), lambda i, blk=blk: (i, blk))
    wfull = pl.BlockSpec((D_MODEL, D_MODEL), lambda i: (0, 0))
    return pl.pallas_call(
        _proj_router_kernel,
        grid=(n // tm,),
        in_specs=[row(), row(), row(ZB_GA), row(ZB_GB), row(), wfull, wfull, wfull,
                  pl.BlockSpec((1, D_MODEL), lambda i: (0, 0)),
                  pl.BlockSpec((D_MODEL, LANES), lambda i: (0, 0)),
                  pl.BlockSpec((1, LANES), lambda i: (0, 0))],
        out_specs=[row(), row(), pl.BlockSpec((tm, LANES), lambda i: (i, 0))],
        out_shape=[jax.ShapeDtypeStruct((n, D_MODEL), f32),
                   jax.ShapeDtypeStruct((n, D_MODEL), bf16),
                   jax.ShapeDtypeStruct((n, LANES), f32)],
        compiler_params=pltpu.CompilerParams(
            dimension_semantics=("parallel",), vmem_limit_bytes=VMEM_LIMIT),
        name="proj_router",
    )(hml, hgm, z, z, x, pa, pb, wo, g, wr, br)


def _moe_final_kernel(xn_ref, comb_ref, x2_ref, wg_ref, wu_ref, wd_ref, g_ref, y_ref, acc_ref):
    grp = pl.program_id(1)

    @pl.when(grp == 0)
    def _():
        acc_ref[...] = jnp.zeros_like(acc_ref)

    xn = xn_ref[...]
    comb = comb_ref[...]
    lane = lax.broadcasted_iota(jnp.int32, comb.shape, 1)
    acc = acc_ref[...]
    for e in range(MOE_PER_GROUP):
        ce = jnp.sum(jnp.where(lane == grp * MOE_PER_GROUP + e, comb, 0.0), axis=1, keepdims=True)
        a = _dot(xn, wg_ref[e])
        u = _dot(xn, wu_ref[e])
        hid = a * _sigmoid(a) * u * ce
        acc = acc + _dot(hid.astype(bf16), wd_ref[e])
    acc_ref[...] = acc

    @pl.when(grp == MOE_GROUPS - 1)
    def _():
        y_ref[...] = _rms(x2_ref[...] + acc, g_ref[...])


def _moe_final(xn, comb, x2, wg, wu, wd, g, tm):
    n = x2.shape[0]
    return pl.pallas_call(
        _moe_final_kernel,
        grid=(n // tm, MOE_GROUPS),
        in_specs=[pl.BlockSpec((tm, D_MODEL), lambda i, j: (i, 0)),
                  pl.BlockSpec((tm, LANES), lambda i, j: (i, 0)),
                  pl.BlockSpec((tm, D_MODEL), lambda i, j: (i, 0)),
                  pl.BlockSpec((MOE_PER_GROUP, D_MODEL, MOE_HIDDEN), lambda i, j: (j, 0, 0)),
                  pl.BlockSpec((MOE_PER_GROUP, D_MODEL, MOE_HIDDEN), lambda i, j: (j, 0, 0)),
                  pl.BlockSpec((MOE_PER_GROUP, MOE_HIDDEN, D_MODEL), lambda i, j: (j, 0, 0)),
                  pl.BlockSpec((1, D_MODEL), lambda i, j: (0, 0))],
        out_specs=pl.BlockSpec((tm, D_MODEL), lambda i, j: (i, 0)),
        out_shape=jax.ShapeDtypeStruct((n, D_MODEL), f32),
        scratch_shapes=[pltpu.VMEM((tm, D_MODEL), f32)],
        compiler_params=pltpu.CompilerParams(
            dimension_semantics=("parallel", "arbitrary"), vmem_limit_bytes=VMEM_LIMIT),
        name="moe_final",
    )(xn, comb, x2, wg, wu, wd, g)


def _layer(x3, state, lw, *, prompt):
    nbatch, t, _ = x3.shape
    n = nbatch * t
    x = x3.reshape(n, D_MODEL)
    if prompt:
        L, valid, nb, cd, z_dtype, tm = ML_CHUNK, ML_CHUNK, 1, bf16, bf16, 1024
    else:
        L, valid, nb, cd, z_dtype, tm = t, lw["valid"], 8, f32, f32, min(n, 512)

    z, zg, zgt = _in_proj(x, lw["norm_mix_g"], lw["w_main"], lw["w_gate"], tm, z_dtype)
    z3 = z.reshape(nbatch, t, N_ZB * D_MODEL)
    zg3 = zg.reshape(nbatch, t, LANES)
    zgt3 = zgt.reshape(2 * ML_HEADS, nbatch, t).transpose(1, 0, 2)
    c0, n0, m0 = state
    m0 = jnp.broadcast_to(m0[:, :, None], (nbatch, ML_HEADS, LANES))
    outs = _mixer_core(z3, zg3, zgt3, c0, n0, m0, lw["ml_b_i"], lw["ml_b_f"], lw["ml_norm_g"], lw["gm_norm_g"],
                       lw["gm_ws"][:, :L, :L], lw["gm_bs"][:, :L].T,
                       nb=nb, L=L, valid=valid, cd=cd, emit_v=not prompt, h_dtype=z_dtype)
    hml, hgm, c1, n1, m1 = outs[:5]
    v_rows = None if prompt else outs[5]
    x2, xn, comb = _proj_router(hml.reshape(n, D_MODEL), hgm.reshape(n, D_MODEL), z, x,
                                lw["p_a"], lw["p_b"], lw["w_out"], lw["norm_ffn_g"], lw["w_router"], lw["b_router"],
                                min(n, 512))
    y = _moe_final(xn, comb, x2, lw["e_wg"], lw["e_wu"], lw["e_wd"], lw["out_g"], min(n, 512))
    return y.reshape(nbatch, t, D_MODEL), (c1, n1, m1[:, :, 0]), v_rows


def kernel(x_prompt, x_sample, state_mlstm_C, state_mlstm_n, state_mlstm_m, norm_mix_g, w_in, ml_b_i, ml_b_f, ml_norm_g, gm_norm_g, gm_ws, gm_bs, p_a, p_b, w_out, norm_ffn_g, rc_w, rc_b, rf_w, rf_b, e_wg, e_wu, e_wd, final_norm_g):
    depth = w_in.shape[0]
    assert depth == 1, "the final norm is fused into the last layer's MoE kernel; only depth 1 is wired up"
    nbp = x_prompt.shape[0]
    nbs, ts, _ = x_sample.shape

    def layer_weights(l):
        w = w_in[l]
        gates = w[:, GATE_OFF:GATE_OFF + 2 * ML_HEADS]
        return dict(
            norm_mix_g=norm_mix_g[l][None, :],
            w_main=jnp.concatenate([w[:, :GATE_OFF], w[:, GATE_OFF + 2 * ML_HEADS:]], axis=1).astype(bf16),
            w_gate=jnp.pad(gates, ((0, 0), (0, LANES - 2 * ML_HEADS))).astype(bf16),
            ml_b_i=ml_b_i[l], ml_b_f=ml_b_f[l], ml_norm_g=ml_norm_g[l], gm_norm_g=gm_norm_g[l],
            gm_ws=gm_ws[l], gm_bs=gm_bs[l],
            p_a=p_a[l].astype(bf16), p_b=p_b[l].astype(bf16), w_out=w_out[l].astype(bf16),
            norm_ffn_g=norm_ffn_g[l][None, :],
            w_router=jnp.pad(
                jnp.concatenate([rf_w[l].transpose(1, 0, 2).reshape(D_MODEL, MOE_EXPERTS), rc_w[l]], axis=1),
                ((0, 0), (0, LANES - MOE_EXPERTS - MOE_GROUPS))).astype(bf16),
            b_router=jnp.pad(jnp.concatenate([rf_b[l].reshape(-1), rc_b[l]]),
                             (0, LANES - MOE_EXPERTS - MOE_GROUPS))[None, :],
            e_wg=e_wg[l].astype(bf16), e_wu=e_wu[l].astype(bf16), e_wd=e_wd[l].astype(bf16),
            out_g=final_norm_g[None, :],
        )

    lw = layer_weights(0)

    zero_state = (jnp.zeros((nbp, ML_HEADS, ML_DV, ML_DQK), f32), jnp.zeros((nbp, ML_HEADS, ML_DQK), f32),
                  jnp.zeros((nbp, ML_HEADS), f32))
    y_p, (c_p, n_p, m_p), _ = _layer(x_prompt, zero_state, lw, prompt=True)

    xs = jnp.pad(x_sample, ((0, 0), (0, SAMPLE_PAD_T - ts), (0, 0)))
    y_s, (c_s, n_s, m_s), v_s = _layer(xs, (state_mlstm_C[0], state_mlstm_n[0], state_mlstm_m[0]),
                                       dict(lw, valid=ts), prompt=False)
    y_s = y_s[:, :ts]
    v_s = v_s[:, :ts].reshape(nbs, ts, GM_GROUPS, GM_DG)
    return (y_p, y_s, c_p[None], n_p[None], m_p[None], c_s[None], n_s[None], m_s[None], v_s[None])
```

```python
import functools

import numpy as np
import jax
import jax.numpy as jnp
from jax import lax
from jax.experimental import pallas as pl
from jax.experimental.pallas import tpu as pltpu

D_MODEL = 1024
ML_HEADS = 4
ML_DQK = 128
ML_DV = 256
ML_CHUNK = 128
GM_GROUPS = 4
GM_DG = 256
MOE_GROUPS = 4
MOE_PER_GROUP = 8
MOE_EXPERTS = MOE_GROUPS * MOE_PER_GROUP
MOE_HIDDEN = 256
EPS = 1e-6

LANES = 128
SAMPLE_PAD_T = 8
NEG = -1e30
MOE_TILE = 128
MOE_BLOCK = 1024
GRP_LANE = MOE_EXPERTS
VMEM_LIMIT = 56 * 1024 * 1024

ZB_QK, ZB_V, ZB_OG, ZB_U, ZB_GV, ZB_GA, ZB_GB = range(7)
N_ZB = 7
GATE_OFF = 2 * ML_HEADS * ML_DQK + 2 * ML_HEADS * ML_DV

f32 = jnp.float32
bf16 = jnp.bfloat16


def _sigmoid(x):
    return 1.0 / (1.0 + jnp.exp(-x))


def _log_sigmoid(x):
    return jnp.minimum(x, 0.0) - jnp.log1p(jnp.exp(-jnp.abs(x)))


def _gelu_tanh(x):
    return 0.5 * x * (1.0 + jnp.tanh(np.sqrt(2.0 / np.pi) * (x + 0.044715 * (x * x * x))))


def _rms(x, g):
    return x * lax.rsqrt(jnp.mean(x * x, axis=-1, keepdims=True) + EPS) * g


def _dot(a, b):
    return jnp.dot(a, b, preferred_element_type=f32)


def _dot_nt(a, b):
    return lax.dot_general(a, b, (((1,), (1,)), ((), ())), preferred_element_type=f32)


def _dot_tn(a, b):
    return lax.dot_general(a, b, (((0,), (0,)), ((), ())), preferred_element_type=f32)


def _in_proj_kernel(x_ref, g_ref, w_ref, wg_ref, z_ref, zg_ref, zgt_ref, xn_scr):
    j = pl.program_id(1)

    @pl.when(j == 0)
    def _():
        xn = _rms(x_ref[...], g_ref[...]).astype(bf16)
        xn_scr[...] = xn
        zg = _dot(xn, wg_ref[...])
        zg_ref[...] = zg
        zgt_ref[...] = zg.T[:2 * ML_HEADS, :]

    z_ref[...] = _dot(xn_scr[...], w_ref[...]).astype(z_ref.dtype)


def _in_proj(x, g, w_main, w_gate, tm, z_dtype):
    n = x.shape[0]
    return pl.pallas_call(
        _in_proj_kernel,
        grid=(n // tm, N_ZB),
        in_specs=[
            pl.BlockSpec((tm, D_MODEL), lambda i, j: (i, 0)),
            pl.BlockSpec((1, D_MODEL), lambda i, j: (0, 0)),
            pl.BlockSpec((D_MODEL, D_MODEL), lambda i, j: (0, j)),
            pl.BlockSpec((D_MODEL, LANES), lambda i, j: (0, 0)),
        ],
        out_specs=[
            pl.BlockSpec((tm, D_MODEL), lambda i, j: (i, j)),
            pl.BlockSpec((tm, LANES), lambda i, j: (i, 0)),
            pl.BlockSpec((2 * ML_HEADS, tm), lambda i, j: (0, i)),
        ],
        out_shape=[
            jax.ShapeDtypeStruct((n, N_ZB * D_MODEL), z_dtype),
            jax.ShapeDtypeStruct((n, LANES), f32),
            jax.ShapeDtypeStruct((2 * ML_HEADS, n), f32),
        ],
        scratch_shapes=[pltpu.VMEM((tm, D_MODEL), bf16)],
        compiler_params=pltpu.CompilerParams(
            dimension_semantics=("parallel", "arbitrary"), vmem_limit_bytes=VMEM_LIMIT),
        name="in_proj",
    )(x, g, w_main, w_gate)


def _mixer_core_kernel(bi_ref, bf_ref, zqk_ref, zv_ref, zog_ref, zu_ref, zgv_ref, zg_ref, zgt_ref,
                       c0_ref, n0_ref, m0_ref, mlg_ref, gmg_ref, ws_ref, bst_ref,
                       hml_ref, hgm_ref, c_ref, n_ref, m_ref, *maybe_v_ref, nb, L, valid, cd):
    c = pl.program_id(1)

    @pl.when(c == 0)
    def _():
        c_ref[...] = c0_ref[...]
        n_ref[...] = n0_ref[...]
        m_ref[...] = m0_ref[...]

    tt = lax.broadcasted_iota(jnp.int32, (L, L), 0)
    ss = lax.broadcasted_iota(jnp.int32, (L, L), 1)
    s_ok = ss < valid
    causal = (ss <= tt) & s_ok
    upper = tt <= ss
    col_ok = lax.broadcasted_iota(jnp.int32, (L, 1), 0) < valid
    row_ok = lax.broadcasted_iota(jnp.int32, (1, L), 1) < valid
    scale = ML_DQK ** -0.5

    for b in range(nb):
        for h in range(ML_HEADS):
            q = zqk_ref[b, :, h * ML_DQK:(h + 1) * ML_DQK]
            k = zqk_ref[b, :, (ML_HEADS + h) * ML_DQK:(ML_HEADS + h + 1) * ML_DQK]
            v = zv_ref[b, :, h * ML_DV:(h + 1) * ML_DV]
            qf = q.astype(f32)
            kf = k.astype(f32)
            vf = v.astype(f32)
            logi_col = zg_ref[b, :, h:h + 1] + bi_ref[h]
            logi_row = zgt_ref[b, h:h + 1, :] + bi_ref[h]
            logf_col = jnp.where(col_ok, _log_sigmoid(zg_ref[b, :, ML_HEADS + h:ML_HEADS + h + 1] + bf_ref[h]), 0.0)
            logf_row = jnp.where(row_ok, _log_sigmoid(zgt_ref[b, ML_HEADS + h:ML_HEADS + h + 1, :] + bf_ref[h]), 0.0)
            c0 = c_ref[b, h]
            n0 = n_ref[b, h:h + 1, :]
            m0 = m_ref[b, h:h + 1, 0:1]

            b_col = jnp.sum(jnp.where(causal, logf_row, 0.0), axis=1, keepdims=True)
            b_row = jnp.sum(jnp.where(upper, logf_col, 0.0), axis=0, keepdims=True)
            d = jnp.where(causal, b_col - b_row + logi_row, NEG)
            inter = b_col + m0
            m_col = jnp.maximum(inter, jnp.max(d, axis=1, keepdims=True))
            w_intra = jnp.exp(d - m_col) * scale
            w_inter = jnp.exp(inter - m_col) * scale
            s = _dot_nt(q.astype(cd), k.astype(cd)) * w_intra
            num = _dot(s.astype(cd), v.astype(cd)) + w_inter * _dot_nt(q.astype(cd), c0.astype(cd))
            den = jnp.sum(s, axis=1, keepdims=True) + w_inter * jnp.sum(qf * n0, axis=1, keepdims=True)
            hh = num / jnp.maximum(jnp.abs(den), jnp.exp(-m_col))
            hh = _rms(hh, mlg_ref[h:h + 1, :])
            og = zog_ref[b, :, h * ML_DV:(h + 1) * ML_DV].astype(f32)
            hml_ref[b, :, h * ML_DV:(h + 1) * ML_DV] = (_sigmoid(og) * hh).astype(hml_ref.dtype)

            b_last = b_row[:, valid - 1:valid]
            wend = jnp.where(col_ok, b_last - b_col + logi_col, NEG)
            m_new = jnp.maximum(b_last + m0, jnp.max(wend, axis=0, keepdims=True))
            decay = jnp.exp(b_last + m0 - m_new)
            wend = jnp.exp(wend - m_new)
            c_ref[b, h] = decay * c0 + _dot_tn((vf * wend).astype(cd), k.astype(cd))
            n_ref[b, h:h + 1, :] = decay * n0 + jnp.sum(kf * wend, axis=0, keepdims=True)
            m_ref[b, h:h + 1, :] = jnp.broadcast_to(m_new, (1, LANES))

        for g in range(GM_GROUPS):
            sl = slice(g * GM_DG, (g + 1) * GM_DG)
            u = _gelu_tanh(zu_ref[b, :, sl].astype(f32))
            gv = _rms(_gelu_tanh(zgv_ref[b, :, sl].astype(f32)), gmg_ref[g:g + 1, :])
            if maybe_v_ref:
                maybe_v_ref[0][b, :, sl] = gv
            w = jnp.where(ss <= tt, ws_ref[g], 0.0)
            mixed = _dot(w.astype(cd), gv.astype(cd)) + bst_ref[:, g:g + 1]
            hgm_ref[b, :, sl] = (u * mixed).astype(hgm_ref.dtype)


def _mixer_core(z3, zg3, zgt3, c0, n0, m0, b_i, b_f, ml_g, gm_g, ws, bst, *, nb, L, valid, cd, emit_v, h_dtype):
    nbatch, t, _ = z3.shape
    nc = t // L
    zspec = lambda blk: pl.BlockSpec((nb, L, D_MODEL), lambda b, c, blk=blk: (b, c, blk))
    full = lambda shape: pl.BlockSpec(shape, lambda b, c: (0,) * len(shape))
    smem = pl.BlockSpec(memory_space=pltpu.SMEM)
    state_specs = [
        pl.BlockSpec((nb, ML_HEADS, ML_DV, ML_DQK), lambda b, c: (b, 0, 0, 0)),
        pl.BlockSpec((nb, ML_HEADS, ML_DQK), lambda b, c: (b, 0, 0)),
        pl.BlockSpec((nb, ML_HEADS, LANES), lambda b, c: (b, 0, 0)),
    ]
    tok_spec = pl.BlockSpec((nb, L, D_MODEL), lambda b, c: (b, c, 0))
    out_specs = [tok_spec, tok_spec] + state_specs
    out_shape = [
        jax.ShapeDtypeStruct((nbatch, t, D_MODEL), h_dtype),
        jax.ShapeDtypeStruct((nbatch, t, D_MODEL), h_dtype),
        jax.ShapeDtypeStruct((nbatch, ML_HEADS, ML_DV, ML_DQK), f32),
        jax.ShapeDtypeStruct((nbatch, ML_HEADS, ML_DQK), f32),
        jax.ShapeDtypeStruct((nbatch, ML_HEADS, LANES), f32),
    ]
    if emit_v:
        out_specs.append(tok_spec)
        out_shape.append(jax.ShapeDtypeStruct((nbatch, t, D_MODEL), f32))
    return pl.pallas_call(
        functools.partial(_mixer_core_kernel, nb=nb, L=L, valid=valid, cd=cd),
        grid=(nbatch // nb, nc),
        in_specs=[smem, smem,
                  zspec(ZB_QK), zspec(ZB_V), zspec(ZB_OG), zspec(ZB_U), zspec(ZB_GV),
                  pl.BlockSpec((nb, L, LANES), lambda b, c: (b, c, 0)),
                  pl.BlockSpec((nb, 2 * ML_HEADS, L), lambda b, c: (b, 0, c)),
                  *state_specs,
                  full((ML_HEADS, ML_DV)), full((GM_GROUPS, GM_DG)), full((GM_GROUPS, L, L)),
                  full((L, GM_GROUPS))],
        out_specs=out_specs,
        out_shape=out_shape,
        compiler_params=pltpu.CompilerParams(
            dimension_semantics=("parallel", "arbitrary"), vmem_limit_bytes=VMEM_LIMIT),
        name="mixer_core",
    )(b_i, b_f, z3, z3, z3, z3, z3, zg3, zgt3, c0, n0, m0, ml_g, gm_g, ws, bst)


def _proj_router_kernel(hml_ref, hgm_ref, ga_ref, gb_ref, x_ref, pa_ref, pb_ref, wo_ref, g_ref, wr_ref, br_ref,
                        x2_ref, xn_ref, comb_ref):
    a = _dot(hml_ref[...].astype(bf16), pa_ref[...])
    b = _dot(hgm_ref[...].astype(bf16), pb_ref[...])
    merged = _sigmoid(ga_ref[...].astype(f32)) * a + _sigmoid(gb_ref[...].astype(f32)) * b
    x2 = x_ref[...] + _dot(merged.astype(bf16), wo_ref[...])
    x2_ref[...] = x2
    xn = _rms(x2, g_ref[...]).astype(bf16)
    xn_ref[...] = xn

    lg = _dot(xn, wr_ref[...]) + br_ref[...]
    lane = lax.broadcasted_iota(jnp.int32, lg.shape, 1).astype(f32)
    cmask = (lane >= MOE_EXPERTS) & (lane < MOE_EXPERTS + MOE_GROUPS)
    cl = jnp.where(cmask, lg, NEG)
    cmax = jnp.max(cl, axis=1, keepdims=True)
    p_grp = 1.0 / jnp.sum(jnp.where(cmask, jnp.exp(cl - cmax), 0.0), axis=1, keepdims=True)
    grp = jnp.min(jnp.where(cl == cmax, lane, 2.0 * LANES), axis=1, keepdims=True) - MOE_EXPERTS
    fmask = (lane >= grp * MOE_PER_GROUP) & (lane < (grp + 1.0) * MOE_PER_GROUP)
    fl = jnp.where(fmask, lg, NEG)
    v1 = jnp.max(fl, axis=1, keepdims=True)
    i1 = jnp.min(jnp.where(fl == v1, lane, 2.0 * LANES), axis=1, keepdims=True)
    fl2 = jnp.where(lane == i1, NEG, fl)
    v2 = jnp.max(fl2, axis=1, keepdims=True)
    i2 = jnp.min(jnp.where(fl2 == v2, lane, 2.0 * LANES), axis=1, keepdims=True)
    e2 = jnp.exp(v2 - v1)
    g1 = p_grp / (1.0 + e2)
    g2 = p_grp * e2 / (1.0 + e2)
    comb_ref[...] = (jnp.where(lane == i1, g1, 0.0) + jnp.where(lane == i2, g2, 0.0)
                     + jnp.where(lane == GRP_LANE, grp, 0.0))


def _proj_router(hml, hgm, z, x, pa, pb, wo, g, wr, br, tm):
    n = x.shape[0]
    row = lambda blk=0: pl.BlockSpec((tm, D_MODEL), lambda i, blk=blk: (i, blk))
    wfull = pl.BlockSpec((D_MODEL, D_MODEL), lambda i: (0, 0))
    return pl.pallas_call(
        _proj_router_kernel,
        grid=(n // tm,),
        in_specs=[row(), row(), row(ZB_GA), row(ZB_GB), row(), wfull, wfull, wfull,
                  pl.BlockSpec((1, D_MODEL), lambda i: (0, 0)),
                  pl.BlockSpec((D_MODEL, LANES), lambda i: (0, 0)),
                  pl.BlockSpec((1, LANES), lambda i: (0, 0))],
        out_specs=[row(), row(), pl.BlockSpec((tm, LANES), lambda i: (i, 0))],
        out_shape=[jax.ShapeDtypeStruct((n, D_MODEL), f32),
                   jax.ShapeDtypeStruct((n, D_MODEL), bf16),
                   jax.ShapeDtypeStruct((n, LANES), f32)],
        compiler_params=pltpu.CompilerParams(
            dimension_semantics=("parallel",), vmem_limit_bytes=VMEM_LIMIT),
        name="proj_router",
    )(hml, hgm, z, z, x, pa, pb, wo, g, wr, br)


def _moe_final_kernel(xn_ref, comb_ref, x2_ref, ltri_ref, wg_ref, wu_ref, wd_ref, g_ref, y_ref,
                      chl_ref, gp_col_ref, gp_row_ref, pos_col_ref, pos_row_ref):
    grp = pl.program_id(1)
    gf = grp.astype(f32)
    tb = xn_ref.shape[0]

    @pl.when(grp == 0)
    def _():
        y_ref[...] = jnp.zeros_like(y_ref)
        comb = comb_ref[...]
        lane = lax.broadcasted_iota(jnp.int32, comb.shape, 1)
        gcol = comb[:, GRP_LANE:GRP_LANE + 1]
        onehot = jnp.where(lane.astype(f32) == gcol, 1.0, 0.0)
        before = _dot(ltri_ref[...], onehot.astype(bf16))
        pos = jnp.sum(onehot * before, axis=1, keepdims=True)
        gp = jnp.where(lane == 0, gcol, 0.0) + jnp.where(lane == 1, pos, 0.0)
        gp_col_ref[...] = gp
        gp_row_ref[...] = gp.T[:8, :]
        chi = comb.astype(bf16)
        chl_ref[:, :LANES] = chi
        chl_ref[:, LANES:] = (comb - chi.astype(f32)).astype(bf16)

    in_grp_col = gp_col_ref[:, 0:1] == gf
    pos_col_ref[...] = jnp.broadcast_to(jnp.where(in_grp_col, gp_col_ref[:, 1:2], -1.0), (tb, MOE_TILE))
    pos_row_ref[...] = jnp.broadcast_to(jnp.where(gp_row_ref[0:1, :] == gf, gp_row_ref[1:2, :], -1.0), (8, tb))
    cnt = jnp.sum(jnp.where(in_grp_col, 1.0, 0.0)).astype(jnp.int32)

    def tile(t, carry):
        base = (t * MOE_TILE).astype(f32)
        r_iota = lax.broadcasted_iota(jnp.int32, (MOE_TILE, tb), 0).astype(f32)
        gather = jnp.where(pos_row_ref[0:1, :] - base == r_iota, 1.0, 0.0).astype(bf16)
        x = _dot(gather, xn_ref[...]).astype(bf16)
        c2 = _dot(gather, chl_ref[...])
        c = c2[:, :LANES] + c2[:, LANES:]
        lane = lax.broadcasted_iota(jnp.int32, c.shape, 1)
        hid = []
        for e in range(MOE_PER_GROUP):
            ce = jnp.sum(jnp.where(lane == grp * MOE_PER_GROUP + e, c, 0.0), axis=1, keepdims=True)
            a = _dot(x, wg_ref[e])
            u = _dot(x, wu_ref[e])
            hid.append((a * _sigmoid(a) * u * ce).astype(bf16))
        y = _dot(jnp.concatenate(hid, axis=1), wd_ref[...]).astype(bf16)
        l_iota = lax.broadcasted_iota(jnp.int32, (tb, MOE_TILE), 1).astype(f32)
        scatter = jnp.where(pos_col_ref[...] - base == l_iota, 1.0, 0.0).astype(bf16)
        y_ref[...] += _dot(scatter, y)
        return carry

    lax.fori_loop(0, (cnt + MOE_TILE - 1) // MOE_TILE, tile, 0)

    @pl.when(grp == MOE_GROUPS - 1)
    def _():
        y_ref[...] = _rms(x2_ref[...] + y_ref[...], g_ref[...])


def _moe_final(xn, comb, x2, wg, wu, wd, g, tb):
    n = x2.shape[0]
    ltri = jnp.tril(jnp.ones((tb, tb), bf16), -1)
    return pl.pallas_call(
        _moe_final_kernel,
        grid=(n // tb, MOE_GROUPS),
        in_specs=[pl.BlockSpec((tb, D_MODEL), lambda i, j: (i, 0)),
                  pl.BlockSpec((tb, LANES), lambda i, j: (i, 0)),
                  pl.BlockSpec((tb, D_MODEL), lambda i, j: (i, 0)),
                  pl.BlockSpec((tb, tb), lambda i, j: (0, 0)),
                  pl.BlockSpec((MOE_PER_GROUP, D_MODEL, MOE_HIDDEN), lambda i, j: (j, 0, 0)),
                  pl.BlockSpec((MOE_PER_GROUP, D_MODEL, MOE_HIDDEN), lambda i, j: (j, 0, 0)),
                  pl.BlockSpec((None, MOE_PER_GROUP * MOE_HIDDEN, D_MODEL), lambda i, j: (j, 0, 0)),
                  pl.BlockSpec((1, D_MODEL), lambda i, j: (0, 0))],
        out_specs=pl.BlockSpec((tb, D_MODEL), lambda i, j: (i, 0)),
        out_shape=jax.ShapeDtypeStruct((n, D_MODEL), f32),
        scratch_shapes=[pltpu.VMEM((tb, 2 * LANES), bf16),
                        pltpu.VMEM((tb, LANES), f32), pltpu.VMEM((8, tb), f32),
                        pltpu.VMEM((tb, MOE_TILE), f32), pltpu.VMEM((8, tb), f32)],
        compiler_params=pltpu.CompilerParams(
            dimension_semantics=("parallel", "arbitrary"), vmem_limit_bytes=VMEM_LIMIT),
        name="moe_final",
    )(xn, comb, x2, ltri, wg, wu, wd.reshape(MOE_GROUPS, MOE_PER_GROUP * MOE_HIDDEN, D_MODEL), g)


def _layer(x3, state, lw, *, prompt):
    nbatch, t, _ = x3.shape
    n = nbatch * t
    x = x3.reshape(n, D_MODEL)
    if prompt:
        L, valid, nb, cd, z_dtype, tm = ML_CHUNK, ML_CHUNK, 1, bf16, bf16, 1024
    else:
        L, valid, nb, cd, z_dtype, tm = t, lw["valid"], 8, f32, f32, min(n, 512)

    z, zg, zgt = _in_proj(x, lw["norm_mix_g"], lw["w_main"], lw["w_gate"], tm, z_dtype)
    z3 = z.reshape(nbatch, t, N_ZB * D_MODEL)
    zg3 = zg.reshape(nbatch, t, LANES)
    zgt3 = zgt.reshape(2 * ML_HEADS, nbatch, t).transpose(1, 0, 2)
    c0, n0, m0 = state
    m0 = jnp.broadcast_to(m0[:, :, None], (nbatch, ML_HEADS, LANES))
    outs = _mixer_core(z3, zg3, zgt3, c0, n0, m0, lw["ml_b_i"], lw["ml_b_f"], lw["ml_norm_g"], lw["gm_norm_g"],
                       lw["gm_ws"][:, :L, :L], lw["gm_bs"][:, :L].T,
                       nb=nb, L=L, valid=valid, cd=cd, emit_v=not prompt, h_dtype=z_dtype)
    hml, hgm, c1, n1, m1 = outs[:5]
    v_rows = None if prompt else outs[5]
    x2, xn, comb = _proj_router(hml.reshape(n, D_MODEL), hgm.reshape(n, D_MODEL), z, x,
                                lw["p_a"], lw["p_b"], lw["w_out"], lw["norm_ffn_g"], lw["w_router"], lw["b_router"],
                                min(n, 512))
    y = _moe_final(xn, comb, x2, lw["e_wg"], lw["e_wu"], lw["e_wd"], lw["out_g"], MOE_BLOCK)
    return y.reshape(nbatch, t, D_MODEL), (c1, n1, m1[:, :, 0]), v_rows


def kernel(x_prompt, x_sample, state_mlstm_C, state_mlstm_n, state_mlstm_m, norm_mix_g, w_in, ml_b_i, ml_b_f, ml_norm_g, gm_norm_g, gm_ws, gm_bs, p_a, p_b, w_out, norm_ffn_g, rc_w, rc_b, rf_w, rf_b, e_wg, e_wu, e_wd, final_norm_g):
    depth = w_in.shape[0]
    assert depth == 1, "the final norm is fused into the last layer's MoE kernel; only depth 1 is wired up"
    nbp = x_prompt.shape[0]
    nbs, ts, _ = x_sample.shape

    def layer_weights(l):
        w = w_in[l]
        gates = w[:, GATE_OFF:GATE_OFF + 2 * ML_HEADS]
        return dict(
            norm_mix_g=norm_mix_g[l][None, :],
            w_main=jnp.concatenate([w[:, :GATE_OFF], w[:, GATE_OFF + 2 * ML_HEADS:]], axis=1).astype(bf16),
            w_gate=jnp.pad(gates, ((0, 0), (0, LANES - 2 * ML_HEADS))).astype(bf16),
            ml_b_i=ml_b_i[l], ml_b_f=ml_b_f[l], ml_norm_g=ml_norm_g[l], gm_norm_g=gm_norm_g[l],
            gm_ws=gm_ws[l], gm_bs=gm_bs[l],
            p_a=p_a[l].astype(bf16), p_b=p_b[l].astype(bf16), w_out=w_out[l].astype(bf16),
            norm_ffn_g=norm_ffn_g[l][None, :],
            w_router=jnp.pad(
                jnp.concatenate([rf_w[l].transpose(1, 0, 2).reshape(D_MODEL, MOE_EXPERTS), rc_w[l]], axis=1),
                ((0, 0), (0, LANES - MOE_EXPERTS - MOE_GROUPS))).astype(bf16),
            b_router=jnp.pad(jnp.concatenate([rf_b[l].reshape(-1), rc_b[l]]),
                             (0, LANES - MOE_EXPERTS - MOE_GROUPS))[None, :],
            e_wg=e_wg[l].astype(bf16), e_wu=e_wu[l].astype(bf16), e_wd=e_wd[l].astype(bf16),
            out_g=final_norm_g[None, :],
        )

    lw = layer_weights(0)

    zero_state = (jnp.zeros((nbp, ML_HEADS, ML_DV, ML_DQK), f32), jnp.zeros((nbp, ML_HEADS, ML_DQK), f32),
                  jnp.zeros((nbp, ML_HEADS), f32))
    y_p, (c_p, n_p, m_p), _ = _layer(x_prompt, zero_state, lw, prompt=True)

    xs = jnp.pad(x_sample, ((0, 0), (0, SAMPLE_PAD_T - ts), (0, 0)))
    y_s, (c_s, n_s, m_s), v_s = _layer(xs, (state_mlstm_C[0], state_mlstm_n[0], state_mlstm_m[0]),
                                       dict(lw, valid=ts), prompt=False)
    y_s = y_s[:, :ts]
    v_s = v_s[:, :ts].reshape(nbs, ts, GM_GROUPS, GM_DG)
    return (y_p, y_s, c_p[None], n_p[None], m_p[None], c_s[None], n_s[None], m_s[None], v_s[None])
```

```python
import functools

import numpy as np
import jax
import jax.numpy as jnp
from jax import lax
from jax.experimental import pallas as pl
from jax.experimental.pallas import tpu as pltpu

D_MODEL = 1024
ML_HEADS = 4
ML_DQK = 128
ML_DV = 256
ML_CHUNK = 128
GM_GROUPS = 4
GM_DG = 256
MOE_GROUPS = 4
MOE_PER_GROUP = 8
MOE_EXPERTS = MOE_GROUPS * MOE_PER_GROUP
MOE_HIDDEN = 256
EPS = 1e-6

LANES = 128
SAMPLE_PAD_T = 8
ROWS = 128
NEG = -1e30
MOE_TILE = 128
MOE_BLOCK = 1024
GRP_LANE = MOE_EXPERTS
VMEM_LIMIT = 56 * 1024 * 1024

ZB_QK, ZB_V, ZB_OG, ZB_U, ZB_GV, ZB_GA, ZB_GB = range(7)
N_ZB = 7
GATE_OFF = 2 * ML_HEADS * ML_DQK + 2 * ML_HEADS * ML_DV

f32 = jnp.float32
bf16 = jnp.bfloat16


def _sigmoid(x):
    return 1.0 / (1.0 + jnp.exp(-x))


def _log_sigmoid(x):
    return jnp.minimum(x, 0.0) - jnp.log1p(jnp.exp(-jnp.abs(x)))


def _gelu_tanh(x):
    return 0.5 * x * (1.0 + jnp.tanh(np.sqrt(2.0 / np.pi) * (x + 0.044715 * (x * x * x))))


def _rms(x, g):
    return x * lax.rsqrt(jnp.mean(x * x, axis=-1, keepdims=True) + EPS) * g


def _dot(a, b):
    return jnp.dot(a, b, preferred_element_type=f32)


def _dot_nt(a, b):
    return lax.dot_general(a, b, (((1,), (1,)), ((), ())), preferred_element_type=f32)


def _dot_tn(a, b):
    return lax.dot_general(a, b, (((0,), (0,)), ((), ())), preferred_element_type=f32)


def _in_proj_kernel(x_ref, g_ref, w_ref, wg_ref, gmg_ref, z_ref, zg_ref, zgt_ref, xn_scr):
    j = pl.program_id(1)

    @pl.when(j == 0)
    def _():
        xn = _rms(x_ref[...], g_ref[...]).astype(bf16)
        xn_scr[...] = xn
        zg = _dot(xn, wg_ref[...])
        zg_ref[...] = zg
        zgt_ref[...] = zg.T[:2 * ML_HEADS, :]

    def emit(blocks, act):
        @pl.when(functools.reduce(jnp.logical_or, [j == blk for blk in blocks]))
        def _():
            z_ref[...] = act(_dot(xn_scr[...], w_ref[...])).astype(z_ref.dtype)

    def gelu_group_rms(a):
        a = _gelu_tanh(a)
        return jnp.concatenate(
            [_rms(a[:, g * GM_DG:(g + 1) * GM_DG], gmg_ref[:, g * GM_DG:(g + 1) * GM_DG]) for g in range(GM_GROUPS)],
            axis=1)

    emit((ZB_QK, ZB_V), lambda a: a)
    emit((ZB_OG, ZB_GA, ZB_GB), _sigmoid)
    emit((ZB_U,), _gelu_tanh)
    emit((ZB_GV,), gelu_group_rms)


def _in_proj(x, g, w_main, w_gate, gm_g, tm, z_dtype):
    n = x.shape[0]
    return pl.pallas_call(
        _in_proj_kernel,
        grid=(n // tm, N_ZB),
        in_specs=[
            pl.BlockSpec((tm, D_MODEL), lambda i, j: (i, 0)),
            pl.BlockSpec((1, D_MODEL), lambda i, j: (0, 0)),
            pl.BlockSpec((D_MODEL, D_MODEL), lambda i, j: (0, j)),
            pl.BlockSpec((D_MODEL, LANES), lambda i, j: (0, 0)),
            pl.BlockSpec((1, D_MODEL), lambda i, j: (0, 0)),
        ],
        out_specs=[
            pl.BlockSpec((tm, D_MODEL), lambda i, j: (i, j)),
            pl.BlockSpec((tm, LANES), lambda i, j: (i, 0)),
            pl.BlockSpec((2 * ML_HEADS, tm), lambda i, j: (0, i)),
        ],
        out_shape=[
            jax.ShapeDtypeStruct((n, N_ZB * D_MODEL), z_dtype),
            jax.ShapeDtypeStruct((n, LANES), f32),
            jax.ShapeDtypeStruct((2 * ML_HEADS, n), f32),
        ],
        scratch_shapes=[pltpu.VMEM((tm, D_MODEL), bf16)],
        compiler_params=pltpu.CompilerParams(
            dimension_semantics=("parallel", "arbitrary"), vmem_limit_bytes=VMEM_LIMIT),
        name="in_proj",
    )(x, g, w_main, w_gate, gm_g)


def _mixer_core_kernel(bi_ref, bf_ref, zqk_ref, zv_ref, zog_ref, zu_ref, zgv_ref, zg_ref, zgt_ref,
                       c0_ref, n0_ref, m0_ref, mlg_ref, wsbd_ref, bcol_ref, eye_ref,
                       hml_ref, hgm_ref, c_ref, n_ref, m_ref, *, nb, L, valid, single_chunk):
    if single_chunk:
        cs_ref, ns_ref, ms_ref = c0_ref, n0_ref, m0_ref
    else:
        cs_ref, ns_ref, ms_ref = c_ref, n_ref, m_ref

        @pl.when(pl.program_id(1) == 0)
        def _():
            c_ref[...] = c0_ref[...]
            n_ref[...] = n0_ref[...]
            m_ref[...] = m0_ref[...]

    shift = L.bit_length() - 1
    pp = lax.broadcasted_iota(jnp.int32, (ROWS, ROWS), 0)
    qq = lax.broadcasted_iota(jnp.int32, (ROWS, ROWS), 1)
    same = (pp >> shift) == (qq >> shift)
    p_col = lax.broadcasted_iota(jnp.int32, (ROWS, 1), 0)
    p_row = lax.broadcasted_iota(jnp.int32, (1, ROWS), 1)
    col_ok = (p_col & (L - 1)) < valid
    row_ok = (p_row & (L - 1)) < valid
    causal = same & (qq <= pp) & row_ok
    upper = same & (pp <= qq)
    scale = ML_DQK ** -0.5

    def per_seq(fn):
        return jnp.concatenate([fn(b) for b in range(nb)], axis=0) if nb > 1 else fn(0)

    stores = []
    H = range(ML_HEADS)
    q = [zqk_ref[:, h * ML_DQK:(h + 1) * ML_DQK] for h in H]
    k = [zqk_ref[:, (ML_HEADS + h) * ML_DQK:(ML_HEADS + h + 1) * ML_DQK] for h in H]
    v = [zv_ref[:, h * ML_DV:(h + 1) * ML_DV] for h in H]
    qc = [x.astype(bf16) for x in q]
    kc = [x.astype(bf16) for x in k]
    c0 = [cs_ref[:, h] for h in H]
    m0 = [per_seq(lambda b: jnp.broadcast_to(ms_ref[b, h:h + 1, 0:1], (L, 1))) for h in H]
    n0 = [per_seq(lambda b: jnp.broadcast_to(ns_ref[b, h:h + 1, :], (L, ML_DQK))) for h in H]
    logi_col = [zg_ref[:, h:h + 1] + bi_ref[h] for h in H]
    logi_row = [zgt_ref[h:h + 1, :] + bi_ref[h] for h in H]
    logf_col = [jnp.where(col_ok, _log_sigmoid(zg_ref[:, ML_HEADS + h:ML_HEADS + h + 1] + bf_ref[h]), 0.0) for h in H]
    logf_row = [jnp.where(row_ok, _log_sigmoid(zgt_ref[ML_HEADS + h:ML_HEADS + h + 1, :] + bf_ref[h]), 0.0) for h in H]

    qkc = [_dot_nt(qc[h], jnp.concatenate([kc[h], c0[h].reshape(nb * ML_DV, ML_DQK).astype(bf16)], axis=0)) for h in H]
    b_col = [jnp.sum(jnp.where(causal, logf_row[h], 0.0), axis=1, keepdims=True) for h in H]
    b_row = [jnp.sum(jnp.where(upper, logf_col[h], 0.0), axis=0, keepdims=True) for h in H]
    b_last_col = [jnp.sum(jnp.where(same, logf_row[h], 0.0), axis=1, keepdims=True) for h in H]
    b_last_row = [jnp.sum(jnp.where(same, logf_col[h], 0.0), axis=0, keepdims=True) for h in H]
    d = [jnp.where(causal, b_col[h] - b_row[h] + logi_row[h], NEG) for h in H]
    inter = [b_col[h] + m0[h] for h in H]
    m_col = [jnp.maximum(inter[h], jnp.max(d[h], axis=1, keepdims=True)) for h in H]
    w_intra = [jnp.exp(d[h] - m_col[h]) * scale for h in H]
    w_inter = [jnp.exp(inter[h] - m_col[h]) * scale for h in H]
    s = [qkc[h][:, :ROWS] * w_intra[h] for h in H]
    q_mem = [per_seq(lambda b: qkc[h][b * L:(b + 1) * L, ROWS + b * ML_DV:ROWS + (b + 1) * ML_DV]) for h in H]
    num = [_dot(s[h].astype(bf16), v[h].astype(bf16)) + w_inter[h] * q_mem[h] for h in H]
    den = [jnp.sum(s[h], axis=1, keepdims=True)
           + w_inter[h] * jnp.sum(q[h].astype(f32) * n0[h], axis=1, keepdims=True) for h in H]
    hh = [_rms(num[h] / jnp.maximum(jnp.abs(den[h]), jnp.exp(-m_col[h])), mlg_ref[h:h + 1, :]) for h in H]
    for h in H:
        sog = zog_ref[:, h * ML_DV:(h + 1) * ML_DV].astype(f32)
        stores.append((hml_ref, (slice(None), slice(h * ML_DV, (h + 1) * ML_DV)), (sog * hh[h]).astype(hml_ref.dtype)))

    wend_col = [jnp.where(col_ok, b_last_col[h] - b_col[h] + logi_col[h], NEG) for h in H]
    wend_row = [jnp.where(row_ok, b_last_row[h] - b_row[h] + logi_row[h], NEG) for h in H]
    m_new = [jnp.maximum(b_last_col[h] + m0[h], jnp.max(jnp.where(same, wend_row[h], NEG), axis=1, keepdims=True))
             for h in H]
    decay = [jnp.exp(b_last_col[h] + m0[h] - m_new[h]) for h in H]
    wend = [jnp.exp(wend_col[h] - m_new[h]) for h in H]
    vw = [(v[h].astype(f32) * wend[h]).astype(bf16) for h in H]
    kw = [k[h].astype(f32) * wend[h] for h in H]
    if nb == 1:
        upd = [_dot_tn(vw[h], kc[h]) for h in H]
    else:
        vw_t = [_dot_nt(eye_ref[...], vw[h]).astype(bf16) for h in H]
        seq_of_lane = lax.broadcasted_iota(jnp.int32, (ML_DV, ROWS), 1) >> shift
        upd = [_dot(jnp.concatenate([jnp.where(seq_of_lane == b, vw_t[h], jnp.zeros_like(vw_t[h])) for b in range(nb)],
                                    axis=0), kc[h]) for h in H]
    for h in H:
        for b in range(nb):
            dec = decay[h][b * L:b * L + 1, :]
            stores.append((c_ref, (b, h), dec * c0[h][b] + upd[h][b * ML_DV:(b + 1) * ML_DV]))
            stores.append((n_ref, (b, slice(h, h + 1), slice(None)),
                           dec * ns_ref[b, h:h + 1, :] + jnp.sum(kw[h][b * L:(b + 1) * L], axis=0, keepdims=True)))
            stores.append((m_ref, (b, slice(h, h + 1), slice(None)),
                           jnp.broadcast_to(m_new[h][b * L:b * L + 1, :], (1, LANES))))

    gv = jnp.concatenate([zgv_ref[:, g * GM_DG:(g + 1) * GM_DG].astype(bf16) for g in range(GM_GROUPS)], axis=0)
    mixed = _dot(wsbd_ref[...], gv) + bcol_ref[...]
    for g in range(GM_GROUPS):
        u = zu_ref[:, g * GM_DG:(g + 1) * GM_DG].astype(f32)
        stores.append((hgm_ref, (slice(None), slice(g * GM_DG, (g + 1) * GM_DG)),
                       (u * mixed[g * ROWS:(g + 1) * ROWS]).astype(hgm_ref.dtype)))

    for ref, idx, val in stores:
        ref[idx] = val


def _mixer_core(z, zg, zgt, c0, n0, m0, b_i, b_f, ml_g, ws_bd, b_col, *, nb, L, valid, h_dtype):
    n = z.shape[0]
    nbatch = c0.shape[0]
    nc = n // (nbatch // nb) // ROWS
    zspec = lambda blk: pl.BlockSpec((ROWS, D_MODEL), lambda b, c, blk=blk: (b * nc + c, blk))
    full = lambda shape: pl.BlockSpec(shape, lambda b, c: (0,) * len(shape))
    smem = pl.BlockSpec(memory_space=pltpu.SMEM)
    state_specs = [
        pl.BlockSpec((nb, ML_HEADS, ML_DV, ML_DQK), lambda b, c: (b, 0, 0, 0)),
        pl.BlockSpec((nb, ML_HEADS, ML_DQK), lambda b, c: (b, 0, 0)),
        pl.BlockSpec((nb, ML_HEADS, LANES), lambda b, c: (b, 0, 0)),
    ]
    tok_spec = zspec(0)
    eye = jnp.eye(ML_DV, dtype=bf16)
    return pl.pallas_call(
        functools.partial(_mixer_core_kernel, nb=nb, L=L, valid=valid, single_chunk=(nc == 1)),
        grid=(nbatch // nb, nc),
        in_specs=[smem, smem,
                  zspec(ZB_QK), zspec(ZB_V), zspec(ZB_OG), zspec(ZB_U), zspec(ZB_GV),
                  pl.BlockSpec((ROWS, LANES), lambda b, c: (b * nc + c, 0)),
                  pl.BlockSpec((2 * ML_HEADS, ROWS), lambda b, c: (0, b * nc + c)),
                  *state_specs,
                  full((ML_HEADS, ML_DV)), full((GM_GROUPS * ROWS, GM_GROUPS * ROWS)), full((GM_GROUPS * ROWS, 1)),
                  full((ML_DV, ML_DV))],
        out_specs=[tok_spec, tok_spec] + state_specs,
        out_shape=[
            jax.ShapeDtypeStruct((n, D_MODEL), h_dtype),
            jax.ShapeDtypeStruct((n, D_MODEL), h_dtype),
            jax.ShapeDtypeStruct((nbatch, ML_HEADS, ML_DV, ML_DQK), f32),
            jax.ShapeDtypeStruct((nbatch, ML_HEADS, ML_DQK), f32),
            jax.ShapeDtypeStruct((nbatch, ML_HEADS, LANES), f32),
        ],
        compiler_params=pltpu.CompilerParams(
            dimension_semantics=("parallel", "arbitrary"), vmem_limit_bytes=VMEM_LIMIT),
        name="mixer_core",
    )(b_i, b_f, z, z, z, z, z, zg, zgt, c0, n0, m0, ml_g, ws_bd, b_col, eye)


def _gmlp_mixing(gm_ws, gm_bs, L):
    reps = ROWS // L
    tril = jnp.tril(jnp.ones((L, L), bool))
    blocks = [jnp.kron(jnp.eye(reps, dtype=f32), jnp.where(tril, gm_ws[g, :L, :L], 0.0)) for g in range(GM_GROUPS)]
    ws_bd = jax.scipy.linalg.block_diag(*blocks).astype(bf16)
    b_col = jnp.concatenate([jnp.tile(gm_bs[g, :L], reps) for g in range(GM_GROUPS)])[:, None]
    return ws_bd, b_col


def _proj_router_kernel(hml_ref, hgm_ref, sga_ref, sgb_ref, x_ref, pa_ref, pb_ref, wo_ref, g_ref, wr_ref, br_ref,
                        x2_ref, xn_ref, comb_ref):
    a = _dot(hml_ref[...].astype(bf16), pa_ref[...])
    b = _dot(hgm_ref[...].astype(bf16), pb_ref[...])
    merged = sga_ref[...].astype(f32) * a + sgb_ref[...].astype(f32) * b
    x2 = x_ref[...] + _dot(merged.astype(bf16), wo_ref[...])
    x2_ref[...] = x2
    xn = _rms(x2, g_ref[...]).astype(bf16)
    xn_ref[...] = xn

    lg = _dot(xn, wr_ref[...]) + br_ref[...]
    lane = lax.broadcasted_iota(jnp.int32, lg.shape, 1).astype(f32)
    cmask = (lane >= MOE_EXPERTS) & (lane < MOE_EXPERTS + MOE_GROUPS)
    cl = jnp.where(cmask, lg, NEG)
    cmax = jnp.max(cl, axis=1, keepdims=True)
    p_grp = 1.0 / jnp.sum(jnp.where(cmask, jnp.exp(cl - cmax), 0.0), axis=1, keepdims=True)
    grp = jnp.min(jnp.where(cl == cmax, lane, 2.0 * LANES), axis=1, keepdims=True) - MOE_EXPERTS
    fmask = (lane >= grp * MOE_PER_GROUP) & (lane < (grp + 1.0) * MOE_PER_GROUP)
    fl = jnp.where(fmask, lg, NEG)
    v1 = jnp.max(fl, axis=1, keepdims=True)
    i1 = jnp.min(jnp.where(fl == v1, lane, 2.0 * LANES), axis=1, keepdims=True)
    fl2 = jnp.where(lane == i1, NEG, fl)
    v2 = jnp.max(fl2, axis=1, keepdims=True)
    i2 = jnp.min(jnp.where(fl2 == v2, lane, 2.0 * LANES), axis=1, keepdims=True)
    e2 = jnp.exp(v2 - v1)
    g1 = p_grp / (1.0 + e2)
    g2 = p_grp * e2 / (1.0 + e2)
    comb_ref[...] = (jnp.where(lane == i1, g1, 0.0) + jnp.where(lane == i2, g2, 0.0)
                     + jnp.where(lane == GRP_LANE, grp, 0.0))


def _proj_router(hml, hgm, z, x, pa, pb, wo, g, wr, br, tm):
    n = x.shape[0]
    row = lambda blk=0: pl.BlockSpec((tm, D_MODEL), lambda i, blk=blk: (i, blk))
    wfull = pl.BlockSpec((D_MODEL, D_MODEL), lambda i: (0, 0))
    return pl.pallas_call(
        _proj_router_kernel,
        grid=(n // tm,),
        in_specs=[row(), row(), row(ZB_GA), row(ZB_GB), row(), wfull, wfull, wfull,
                  pl.BlockSpec((1, D_MODEL), lambda i: (0, 0)),
                  pl.BlockSpec((D_MODEL, LANES), lambda i: (0, 0)),
                  pl.BlockSpec((1, LANES), lambda i: (0, 0))],
        out_specs=[row(), row(), pl.BlockSpec((tm, LANES), lambda i: (i, 0))],
        out_shape=[jax.ShapeDtypeStruct((n, D_MODEL), f32),
                   jax.ShapeDtypeStruct((n, D_MODEL), bf16),
                   jax.ShapeDtypeStruct((n, LANES), f32)],
        compiler_params=pltpu.CompilerParams(
            dimension_semantics=("parallel",), vmem_limit_bytes=VMEM_LIMIT),
        name="proj_router",
    )(hml, hgm, z, z, x, pa, pb, wo, g, wr, br)


def _moe_final_kernel(xn_ref, comb_ref, x2_ref, ltri_ref, wg_ref, wu_ref, wd_ref, g_ref, y_ref,
                      chl_ref, gp_col_ref, gp_row_ref, pos_col_ref, pos_row_ref):
    grp = pl.program_id(1)
    gf = grp.astype(f32)
    tb = xn_ref.shape[0]

    @pl.when(grp == 0)
    def _():
        y_ref[...] = jnp.zeros_like(y_ref)
        comb = comb_ref[...]
        lane = lax.broadcasted_iota(jnp.int32, comb.shape, 1)
        gcol = comb[:, GRP_LANE:GRP_LANE + 1]
        onehot = jnp.where(lane.astype(f32) == gcol, 1.0, 0.0)
        before = _dot(ltri_ref[...], onehot.astype(bf16))
        pos = jnp.sum(onehot * before, axis=1, keepdims=True)
        gp = jnp.where(lane == 0, gcol, 0.0) + jnp.where(lane == 1, pos, 0.0)
        gp_col_ref[...] = gp
        gp_row_ref[...] = gp.T[:8, :]
        chi = comb.astype(bf16)
        chl_ref[:, :LANES] = chi
        chl_ref[:, LANES:] = (comb - chi.astype(f32)).astype(bf16)

    in_grp_col = gp_col_ref[:, 0:1] == gf
    pos_col_ref[...] = jnp.broadcast_to(jnp.where(in_grp_col, gp_col_ref[:, 1:2], -1.0), (tb, MOE_TILE))
    pos_row_ref[...] = jnp.broadcast_to(jnp.where(gp_row_ref[0:1, :] == gf, gp_row_ref[1:2, :], -1.0), (8, tb))
    cnt = jnp.sum(jnp.where(in_grp_col, 1.0, 0.0)).astype(jnp.int32)

    def tile(t, carry):
        base = (t * MOE_TILE).astype(f32)
        r_iota = lax.broadcasted_iota(jnp.int32, (MOE_TILE, tb), 0).astype(f32)
        gather = jnp.where(pos_row_ref[0:1, :] - base == r_iota, 1.0, 0.0).astype(bf16)
        x = _dot(gather, xn_ref[...]).astype(bf16)
        c2 = _dot(gather, chl_ref[...])
        c = c2[:, :LANES] + c2[:, LANES:]
        lane = lax.broadcasted_iota(jnp.int32, c.shape, 1)
        hid = []
        for e in range(MOE_PER_GROUP):
            ce = jnp.sum(jnp.where(lane == grp * MOE_PER_GROUP + e, c, 0.0), axis=1, keepdims=True)
            a = _dot(x, wg_ref[e])
            u = _dot(x, wu_ref[e])
            hid.append((a * _sigmoid(a) * u * ce).astype(bf16))
        y = _dot(jnp.concatenate(hid, axis=1), wd_ref[...]).astype(bf16)
        l_iota = lax.broadcasted_iota(jnp.int32, (tb, MOE_TILE), 1).astype(f32)
        scatter = jnp.where(pos_col_ref[...] - base == l_iota, 1.0, 0.0).astype(bf16)
        y_ref[...] += _dot(scatter, y)
        return carry

    lax.fori_loop(0, (cnt + MOE_TILE - 1) // MOE_TILE, tile, 0)

    @pl.when(grp == MOE_GROUPS - 1)
    def _():
        y_ref[...] = _rms(x2_ref[...] + y_ref[...], g_ref[...])


def _moe_final(xn, comb, x2, wg, wu, wd, g, tb):
    n = x2.shape[0]
    ltri = jnp.tril(jnp.ones((tb, tb), bf16), -1)
    return pl.pallas_call(
        _moe_final_kernel,
        grid=(n // tb, MOE_GROUPS),
        in_specs=[pl.BlockSpec((tb, D_MODEL), lambda i, j: (i, 0)),
                  pl.BlockSpec((tb, LANES), lambda i, j: (i, 0)),
                  pl.BlockSpec((tb, D_MODEL), lambda i, j: (i, 0)),
                  pl.BlockSpec((tb, tb), lambda i, j: (0, 0)),
                  pl.BlockSpec((MOE_PER_GROUP, D_MODEL, MOE_HIDDEN), lambda i, j: (j, 0, 0)),
                  pl.BlockSpec((MOE_PER_GROUP, D_MODEL, MOE_HIDDEN), lambda i, j: (j, 0, 0)),
                  pl.BlockSpec((None, MOE_PER_GROUP * MOE_HIDDEN, D_MODEL), lambda i, j: (j, 0, 0)),
                  pl.BlockSpec((1, D_MODEL), lambda i, j: (0, 0))],
        out_specs=pl.BlockSpec((tb, D_MODEL), lambda i, j: (i, 0)),
        out_shape=jax.ShapeDtypeStruct((n, D_MODEL), f32),
        scratch_shapes=[pltpu.VMEM((tb, 2 * LANES), bf16),
                        pltpu.VMEM((tb, LANES), f32), pltpu.VMEM((8, tb), f32),
                        pltpu.VMEM((tb, MOE_TILE), f32), pltpu.VMEM((8, tb), f32)],
        compiler_params=pltpu.CompilerParams(
            dimension_semantics=("parallel", "arbitrary"), vmem_limit_bytes=VMEM_LIMIT),
        name="moe_final",
    )(xn, comb, x2, ltri, wg, wu, wd.reshape(MOE_GROUPS, MOE_PER_GROUP * MOE_HIDDEN, D_MODEL), g)


def _layer(x3, state, lw, *, prompt):
    nbatch, t, _ = x3.shape
    n = nbatch * t
    x = x3.reshape(n, D_MODEL)
    if prompt:
        L, valid, z_dtype, tm = ML_CHUNK, ML_CHUNK, bf16, 1024
    else:
        L, valid, z_dtype, tm = t, lw["valid"], f32, min(n, 512)
    nb = ROWS // L

    z, zg, zgt = _in_proj(x, lw["norm_mix_g"], lw["w_main"], lw["w_gate"], lw["gm_norm_g"], tm, z_dtype)
    c0, n0, m0 = state
    m0 = jnp.broadcast_to(m0[:, :, None], (nbatch, ML_HEADS, LANES))
    ws_bd, b_col = _gmlp_mixing(lw["gm_ws"], lw["gm_bs"], L)
    hml, hgm, c1, n1, m1 = _mixer_core(z, zg, zgt, c0, n0, m0, lw["ml_b_i"], lw["ml_b_f"], lw["ml_norm_g"],
                                       ws_bd, b_col, nb=nb, L=L, valid=valid, h_dtype=z_dtype)
    x2, xn, comb = _proj_router(hml, hgm, z, x, lw["p_a"], lw["p_b"], lw["w_out"], lw["norm_ffn_g"],
                                lw["w_router"], lw["b_router"], min(n, 512))
    y = _moe_final(xn, comb, x2, lw["e_wg"], lw["e_wu"], lw["e_wd"], lw["out_g"], MOE_BLOCK)
    gv = z[:, ZB_GV * D_MODEL:(ZB_GV + 1) * D_MODEL].reshape(nbatch, t, D_MODEL)
    return y.reshape(nbatch, t, D_MODEL), (c1, n1, m1[:, :, 0]), gv


def kernel(x_prompt, x_sample, state_mlstm_C, state_mlstm_n, state_mlstm_m, norm_mix_g, w_in, ml_b_i, ml_b_f, ml_norm_g, gm_norm_g, gm_ws, gm_bs, p_a, p_b, w_out, norm_ffn_g, rc_w, rc_b, rf_w, rf_b, e_wg, e_wu, e_wd, final_norm_g):
    depth = w_in.shape[0]
    assert depth == 1, "the final norm is fused into the last layer's MoE kernel; only depth 1 is wired up"
    nbp = x_prompt.shape[0]
    nbs, ts, _ = x_sample.shape

    def layer_weights(l):
        w = w_in[l]
        gates = w[:, GATE_OFF:GATE_OFF + 2 * ML_HEADS]
        return dict(
            norm_mix_g=norm_mix_g[l][None, :],
            w_main=jnp.concatenate([w[:, :GATE_OFF], w[:, GATE_OFF + 2 * ML_HEADS:]], axis=1).astype(bf16),
            w_gate=jnp.pad(gates, ((0, 0), (0, LANES - 2 * ML_HEADS))).astype(bf16),
            ml_b_i=ml_b_i[l], ml_b_f=ml_b_f[l], ml_norm_g=ml_norm_g[l],
            gm_norm_g=gm_norm_g[l].reshape(1, GM_GROUPS * GM_DG),
            gm_ws=gm_ws[l], gm_bs=gm_bs[l],
            p_a=p_a[l].astype(bf16), p_b=p_b[l].astype(bf16), w_out=w_out[l].astype(bf16),
            norm_ffn_g=norm_ffn_g[l][None, :],
            w_router=jnp.pad(
                jnp.concatenate([rf_w[l].transpose(1, 0, 2).reshape(D_MODEL, MOE_EXPERTS), rc_w[l]], axis=1),
                ((0, 0), (0, LANES - MOE_EXPERTS - MOE_GROUPS))).astype(bf16),
            b_router=jnp.pad(jnp.concatenate([rf_b[l].reshape(-1), rc_b[l]]),
                             (0, LANES - MOE_EXPERTS - MOE_GROUPS))[None, :],
            e_wg=e_wg[l].astype(bf16), e_wu=e_wu[l].astype(bf16), e_wd=e_wd[l].astype(bf16),
            out_g=final_norm_g[None, :],
        )

    lw = layer_weights(0)

    zero_state = (jnp.zeros((nbp, ML_HEADS, ML_DV, ML_DQK), f32), jnp.zeros((nbp, ML_HEADS, ML_DQK), f32),
                  jnp.zeros((nbp, ML_HEADS), f32))
    y_p, (c_p, n_p, m_p), _ = _layer(x_prompt, zero_state, lw, prompt=True)

    xs = jnp.pad(x_sample, ((0, 0), (0, SAMPLE_PAD_T - ts), (0, 0)))
    y_s, (c_s, n_s, m_s), v_s = _layer(xs, (state_mlstm_C[0], state_mlstm_n[0], state_mlstm_m[0]),
                                       dict(lw, valid=ts), prompt=False)
    y_s = y_s[:, :ts]
    v_s = v_s[:, :ts].reshape(nbs, ts, GM_GROUPS, GM_DG)
    return (y_p, y_s, c_p[None], n_p[None], m_p[None], c_s[None], n_s[None], m_s[None], v_s[None])
```

```python
import functools

import numpy as np
import jax
import jax.numpy as jnp
from jax import lax
from jax.experimental import pallas as pl
from jax.experimental.pallas import tpu as pltpu

D_MODEL = 1024
ML_HEADS = 4
ML_DQK = 128
ML_DV = 256
ML_CHUNK = 128
GM_GROUPS = 4
GM_DG = 256
MOE_GROUPS = 4
MOE_PER_GROUP = 8
MOE_EXPERTS = MOE_GROUPS * MOE_PER_GROUP
MOE_HIDDEN = 256
EPS = 1e-6

LANES = 128
SAMPLE_PAD_T = 8
ROWS = 128
NEG = -1e30
MOE_TILE = 128
MOE_BLOCK = 1024
PROJ_TILE = 1024
PROJ_SUB = 256
GRP_LANE = MOE_EXPERTS
VMEM_LIMIT = 56 * 1024 * 1024

ZB_QK, ZB_V, ZB_OG, ZB_U, ZB_GV, ZB_GA, ZB_GB = range(7)
N_ZB = 7
GATE_OFF = 2 * ML_HEADS * ML_DQK + 2 * ML_HEADS * ML_DV

f32 = jnp.float32
bf16 = jnp.bfloat16


def _sigmoid(x):
    return 1.0 / (1.0 + jnp.exp(-x))


def _log_sigmoid(x):
    return jnp.minimum(x, 0.0) - jnp.log1p(jnp.exp(-jnp.abs(x)))


def _gelu_tanh(x):
    return 0.5 * x * (1.0 + jnp.tanh(np.sqrt(2.0 / np.pi) * (x + 0.044715 * (x * x * x))))


def _rms(x, g):
    return x * lax.rsqrt(jnp.mean(x * x, axis=-1, keepdims=True) + EPS) * g


def _dot(a, b):
    return jnp.dot(a, b, preferred_element_type=f32)


def _dot_nt(a, b):
    return lax.dot_general(a, b, (((1,), (1,)), ((), ())), preferred_element_type=f32)


def _dot_tn(a, b):
    return lax.dot_general(a, b, (((0,), (0,)), ((), ())), preferred_element_type=f32)


def _in_proj_kernel(x_ref, g_ref, w_ref, wg_ref, gmg_ref, z_ref, zg_ref, zgt_ref):
    xn = _rms(x_ref[...], g_ref[...]).astype(bf16)
    zg = _dot(xn, wg_ref[...])
    zg_ref[...] = zg
    zgt_ref[...] = zg.T[:2 * ML_HEADS, :]

    def gelu_group_rms(a, g):
        return _rms(_gelu_tanh(a), gmg_ref[:, g * GM_DG:(g + 1) * GM_DG])

    act = {ZB_QK: None, ZB_V: None, ZB_OG: _sigmoid, ZB_GA: _sigmoid, ZB_GB: _sigmoid,
           ZB_U: _gelu_tanh, ZB_GV: gelu_group_rms}
    per_blk = D_MODEL // GM_DG
    for blk in range(N_ZB):
        for g in range(per_blk):
            cols = slice(blk * D_MODEL + g * GM_DG, blk * D_MODEL + (g + 1) * GM_DG)
            a = _dot(xn, w_ref[:, cols])
            if blk == ZB_GV:
                a = gelu_group_rms(a, g)
            elif act[blk] is not None:
                a = act[blk](a)
            z_ref[:, cols] = a.astype(z_ref.dtype)


def _in_proj(x, g, w_main, w_gate, gm_g, tm, z_dtype):
    n = x.shape[0]
    const = lambda shape: pl.BlockSpec(shape, lambda i: (0, 0), pipeline_mode=pl.Buffered(1))
    return pl.pallas_call(
        _in_proj_kernel,
        grid=(n // tm,),
        in_specs=[
            pl.BlockSpec((tm, D_MODEL), lambda i: (i, 0)),
            const((1, D_MODEL)),
            const((D_MODEL, N_ZB * D_MODEL)),
            const((D_MODEL, LANES)),
            const((1, D_MODEL)),
        ],
        out_specs=[
            pl.BlockSpec((tm, N_ZB * D_MODEL), lambda i: (i, 0)),
            pl.BlockSpec((tm, LANES), lambda i: (i, 0)),
            pl.BlockSpec((2 * ML_HEADS, tm), lambda i: (0, i)),
        ],
        out_shape=[
            jax.ShapeDtypeStruct((n, N_ZB * D_MODEL), z_dtype),
            jax.ShapeDtypeStruct((n, LANES), f32),
            jax.ShapeDtypeStruct((2 * ML_HEADS, n), f32),
        ],
        compiler_params=pltpu.CompilerParams(
            dimension_semantics=("parallel",), vmem_limit_bytes=VMEM_LIMIT),
        name="in_proj",
    )(x, g, w_main, w_gate, gm_g)


def _mixer_core_kernel(bi_ref, bf_ref, zqk_ref, zv_ref, zog_ref, zu_ref, zgv_ref, zg_ref, zgt_ref,
                       c0_ref, n0_ref, m0_ref, mlg_ref, wsbd_ref, bcol_ref, eye_ref,
                       hml_ref, hgm_ref, c_ref, n_ref, m_ref, *, nblk, nb, L, valid, single_chunk):
    if single_chunk:
        cs_ref, ns_ref, ms_ref = c0_ref, n0_ref, m0_ref
    else:
        cs_ref, ns_ref, ms_ref = c_ref, n_ref, m_ref

        @pl.when(pl.program_id(1) == 0)
        def _():
            c_ref[...] = c0_ref[...]
            n_ref[...] = n0_ref[...]
            m_ref[...] = m0_ref[...]

    shift = L.bit_length() - 1
    pp = lax.broadcasted_iota(jnp.int32, (ROWS, ROWS), 0)
    qq = lax.broadcasted_iota(jnp.int32, (ROWS, ROWS), 1)
    same = (pp >> shift) == (qq >> shift)
    p_col = lax.broadcasted_iota(jnp.int32, (ROWS, 1), 0)
    p_row = lax.broadcasted_iota(jnp.int32, (1, ROWS), 1)
    col_ok = (p_col & (L - 1)) < valid
    row_ok = (p_row & (L - 1)) < valid
    causal = same & (qq <= pp) & row_ok
    upper = same & (pp <= qq)
    scale = ML_DQK ** -0.5

    def per_seq(fn):
        return jnp.concatenate([fn(b) for b in range(nb)], axis=0) if nb > 1 else fn(0)

    stores = []
    chains = [(r, h) for r in range(nblk) for h in range(ML_HEADS)]
    each = lambda fn: [fn(i, r, h) for i, (r, h) in enumerate(chains)]

    q = each(lambda i, r, h: zqk_ref[r, :, h * ML_DQK:(h + 1) * ML_DQK])
    k = each(lambda i, r, h: zqk_ref[r, :, (ML_HEADS + h) * ML_DQK:(ML_HEADS + h + 1) * ML_DQK])
    v = each(lambda i, r, h: zv_ref[r, :, h * ML_DV:(h + 1) * ML_DV])
    qc = [x.astype(bf16) for x in q]
    kc = [x.astype(bf16) for x in k]
    c0 = each(lambda i, r, h: cs_ref[r * nb:(r + 1) * nb, h])
    m0 = each(lambda i, r, h: per_seq(lambda b: jnp.broadcast_to(ms_ref[r * nb + b, h:h + 1, 0:1], (L, 1))))
    n0 = each(lambda i, r, h: per_seq(lambda b: jnp.broadcast_to(ns_ref[r * nb + b, h:h + 1, :], (L, ML_DQK))))
    logi_col = each(lambda i, r, h: zg_ref[r, :, h:h + 1] + bi_ref[h])
    logi_row = each(lambda i, r, h: zgt_ref[r, h:h + 1, :] + bi_ref[h])
    logf_col = each(lambda i, r, h: jnp.where(
        col_ok, _log_sigmoid(zg_ref[r, :, ML_HEADS + h:ML_HEADS + h + 1] + bf_ref[h]), 0.0))
    logf_row = each(lambda i, r, h: jnp.where(
        row_ok, _log_sigmoid(zgt_ref[r, ML_HEADS + h:ML_HEADS + h + 1, :] + bf_ref[h]), 0.0))

    qkc = each(lambda i, r, h: _dot_nt(
        qc[i], jnp.concatenate([kc[i], c0[i].reshape(nb * ML_DV, ML_DQK).astype(bf16)], axis=0)))
    b_col = each(lambda i, r, h: jnp.sum(jnp.where(causal, logf_row[i], 0.0), axis=1, keepdims=True))
    b_row = each(lambda i, r, h: jnp.sum(jnp.where(upper, logf_col[i], 0.0), axis=0, keepdims=True))
    b_last_col = each(lambda i, r, h: jnp.sum(jnp.where(same, logf_row[i], 0.0), axis=1, keepdims=True))
    b_last_row = each(lambda i, r, h: jnp.sum(jnp.where(same, logf_col[i], 0.0), axis=0, keepdims=True))
    d = each(lambda i, r, h: jnp.where(causal, b_col[i] - b_row[i] + logi_row[i], NEG))
    inter = each(lambda i, r, h: b_col[i] + m0[i])
    m_col = each(lambda i, r, h: jnp.maximum(inter[i], jnp.max(d[i], axis=1, keepdims=True)))
    w_intra = each(lambda i, r, h: jnp.exp(d[i] - m_col[i]) * scale)
    w_inter = each(lambda i, r, h: jnp.exp(inter[i] - m_col[i]) * scale)
    s = each(lambda i, r, h: qkc[i][:, :ROWS] * w_intra[i])
    q_mem = each(lambda i, r, h: per_seq(
        lambda b: qkc[i][b * L:(b + 1) * L, ROWS + b * ML_DV:ROWS + (b + 1) * ML_DV]))
    num = each(lambda i, r, h: _dot(s[i].astype(bf16), v[i].astype(bf16)) + w_inter[i] * q_mem[i])
    den = each(lambda i, r, h: jnp.sum(s[i], axis=1, keepdims=True)
               + w_inter[i] * jnp.sum(q[i].astype(f32) * n0[i], axis=1, keepdims=True))
    hh = each(lambda i, r, h: _rms(num[i] / jnp.maximum(jnp.abs(den[i]), jnp.exp(-m_col[i])), mlg_ref[h:h + 1, :]))
    for i, (r, h) in enumerate(chains):
        sog = zog_ref[r, :, h * ML_DV:(h + 1) * ML_DV].astype(f32)
        stores.append((hml_ref, (r, slice(None), slice(h * ML_DV, (h + 1) * ML_DV)), (sog * hh[i]).astype(hml_ref.dtype)))

    wend_col = each(lambda i, r, h: jnp.where(col_ok, b_last_col[i] - b_col[i] + logi_col[i], NEG))
    wend_row = each(lambda i, r, h: jnp.where(row_ok, b_last_row[i] - b_row[i] + logi_row[i], NEG))
    m_new = each(lambda i, r, h: jnp.maximum(
        b_last_col[i] + m0[i], jnp.max(jnp.where(same, wend_row[i], NEG), axis=1, keepdims=True)))
    decay = each(lambda i, r, h: jnp.exp(b_last_col[i] + m0[i] - m_new[i]))
    wend = each(lambda i, r, h: jnp.exp(wend_col[i] - m_new[i]))
    vw = each(lambda i, r, h: (v[i].astype(f32) * wend[i]).astype(bf16))
    kw = each(lambda i, r, h: k[i].astype(f32) * wend[i])
    if nb == 1:
        upd = each(lambda i, r, h: _dot_tn(vw[i], kc[i]))
    else:
        vw_t = each(lambda i, r, h: _dot_nt(eye_ref[...], vw[i]).astype(bf16))
        seq_of_lane = lax.broadcasted_iota(jnp.int32, (ML_DV, ROWS), 1) >> shift
        upd = each(lambda i, r, h: _dot(jnp.concatenate(
            [jnp.where(seq_of_lane == b, vw_t[i], jnp.zeros_like(vw_t[i])) for b in range(nb)], axis=0), kc[i]))
    for i, (r, h) in enumerate(chains):
        for b in range(nb):
            slot = r * nb + b
            dec = decay[i][b * L:b * L + 1, :]
            stores.append((c_ref, (slot, h), dec * c0[i][b] + upd[i][b * ML_DV:(b + 1) * ML_DV]))
            stores.append((n_ref, (slot, slice(h, h + 1), slice(None)),
                           dec * ns_ref[slot, h:h + 1, :] + jnp.sum(kw[i][b * L:(b + 1) * L], axis=0, keepdims=True)))
            stores.append((m_ref, (slot, slice(h, h + 1), slice(None)),
                           jnp.broadcast_to(m_new[i][b * L:b * L + 1, :], (1, LANES))))

    for r in range(nblk):
        gv = jnp.concatenate([zgv_ref[r, :, g * GM_DG:(g + 1) * GM_DG].astype(bf16) for g in range(GM_GROUPS)], axis=0)
        mixed = _dot(wsbd_ref[...], gv) + bcol_ref[...]
        for g in range(GM_GROUPS):
            u = zu_ref[r, :, g * GM_DG:(g + 1) * GM_DG].astype(f32)
            stores.append((hgm_ref, (r, slice(None), slice(g * GM_DG, (g + 1) * GM_DG)),
                           (u * mixed[g * ROWS:(g + 1) * ROWS]).astype(hgm_ref.dtype)))

    for ref, idx, val in stores:
        ref[idx] = val


def _mixer_core(z, zg, zgt, c0, n0, m0, b_i, b_f, ml_g, ws_bd, b_col, *, nblk, nb, L, valid, h_dtype):
    n = z.shape[0]
    nbatch = c0.shape[0]
    groups = nbatch // nb
    nc = n // groups // ROWS
    z4 = z.reshape(groups, nc, ROWS, N_ZB * D_MODEL)
    zg4 = zg.reshape(groups, nc, ROWS, LANES)
    zgt4 = zgt.reshape(2 * ML_HEADS, groups, nc, ROWS).transpose(1, 2, 0, 3)
    zspec = lambda blk: pl.BlockSpec((nblk, None, ROWS, D_MODEL), lambda b, c, blk=blk: (b, c, 0, blk))
    full = lambda shape: pl.BlockSpec(shape, lambda b, c: (0,) * len(shape))
    smem = pl.BlockSpec(memory_space=pltpu.SMEM)
    state_specs = [
        pl.BlockSpec((nblk * nb, ML_HEADS, ML_DV, ML_DQK), lambda b, c: (b, 0, 0, 0)),
        pl.BlockSpec((nblk * nb, ML_HEADS, ML_DQK), lambda b, c: (b, 0, 0)),
        pl.BlockSpec((nblk * nb, ML_HEADS, LANES), lambda b, c: (b, 0, 0)),
    ]
    tok_spec = zspec(0)
    eye = jnp.eye(ML_DV, dtype=bf16)
    hml, hgm, c1, n1, m1 = pl.pallas_call(
        functools.partial(_mixer_core_kernel, nblk=nblk, nb=nb, L=L, valid=valid, single_chunk=(nc == 1)),
        grid=(groups // nblk, nc),
        in_specs=[smem, smem,
                  zspec(ZB_QK), zspec(ZB_V), zspec(ZB_OG), zspec(ZB_U), zspec(ZB_GV),
                  pl.BlockSpec((nblk, None, ROWS, LANES), lambda b, c: (b, c, 0, 0)),
                  pl.BlockSpec((nblk, None, 2 * ML_HEADS, ROWS), lambda b, c: (b, c, 0, 0)),
                  *state_specs,
                  full((ML_HEADS, ML_DV)), full((GM_GROUPS * ROWS, GM_GROUPS * ROWS)), full((GM_GROUPS * ROWS, 1)),
                  full((ML_DV, ML_DV))],
        out_specs=[tok_spec, tok_spec] + state_specs,
        out_shape=[
            jax.ShapeDtypeStruct((groups, nc, ROWS, D_MODEL), h_dtype),
            jax.ShapeDtypeStruct((groups, nc, ROWS, D_MODEL), h_dtype),
            jax.ShapeDtypeStruct((nbatch, ML_HEADS, ML_DV, ML_DQK), f32),
            jax.ShapeDtypeStruct((nbatch, ML_HEADS, ML_DQK), f32),
            jax.ShapeDtypeStruct((nbatch, ML_HEADS, LANES), f32),
        ],
        compiler_params=pltpu.CompilerParams(
            dimension_semantics=("parallel", "arbitrary"), vmem_limit_bytes=VMEM_LIMIT),
        name="mixer_core",
    )(b_i, b_f, z4, z4, z4, z4, z4, zg4, zgt4, c0, n0, m0, ml_g, ws_bd, b_col, eye)
    return hml.reshape(n, D_MODEL), hgm.reshape(n, D_MODEL), c1, n1, m1


def _gmlp_mixing(gm_ws, gm_bs, L):
    reps = ROWS // L
    tril = jnp.tril(jnp.ones((L, L), bool))
    blocks = [jnp.kron(jnp.eye(reps, dtype=f32), jnp.where(tril, gm_ws[g, :L, :L], 0.0)) for g in range(GM_GROUPS)]
    ws_bd = jax.scipy.linalg.block_diag(*blocks).astype(bf16)
    b_col = jnp.concatenate([jnp.tile(gm_bs[g, :L], reps) for g in range(GM_GROUPS)])[:, None]
    return ws_bd, b_col


def _proj_router_kernel(hml_ref, hgm_ref, sga_ref, sgb_ref, x_ref, pa_ref, pb_ref, wo_ref, g_ref, wr_ref, br_ref,
                        x2_ref, xn_ref, comb_ref):
    tm = x_ref.shape[0]
    subs = [slice(r, r + PROJ_SUB) for r in range(0, tm, PROJ_SUB)]
    a = [_dot(hml_ref[r, :].astype(bf16), pa_ref[...]) for r in subs]
    b = [_dot(hgm_ref[r, :].astype(bf16), pb_ref[...]) for r in subs]
    merged = [sga_ref[r, :].astype(f32) * a[i] + sgb_ref[r, :].astype(f32) * b[i] for i, r in enumerate(subs)]
    x2 = [x_ref[r, :] + _dot(merged[i].astype(bf16), wo_ref[...]) for i, r in enumerate(subs)]
    xn = [_rms(x2[i], g_ref[...]).astype(bf16) for i in range(len(subs))]
    for i, r in enumerate(subs):
        x2_ref[r, :] = x2[i]
        xn_ref[r, :] = xn[i]

    lg = jnp.concatenate([_dot(xn[i], wr_ref[...]) for i in range(len(subs))], axis=0) + br_ref[...]
    lane = lax.broadcasted_iota(jnp.int32, lg.shape, 1).astype(f32)
    cmask = (lane >= MOE_EXPERTS) & (lane < MOE_EXPERTS + MOE_GROUPS)
    cl = jnp.where(cmask, lg, NEG)
    cmax = jnp.max(cl, axis=1, keepdims=True)
    p_grp = 1.0 / jnp.sum(jnp.where(cmask, jnp.exp(cl - cmax), 0.0), axis=1, keepdims=True)
    grp = jnp.min(jnp.where(cl == cmax, lane, 2.0 * LANES), axis=1, keepdims=True) - MOE_EXPERTS
    fmask = (lane >= grp * MOE_PER_GROUP) & (lane < (grp + 1.0) * MOE_PER_GROUP)
    fl = jnp.where(fmask, lg, NEG)
    v1 = jnp.max(fl, axis=1, keepdims=True)
    i1 = jnp.min(jnp.where(fl == v1, lane, 2.0 * LANES), axis=1, keepdims=True)
    fl2 = jnp.where(lane == i1, NEG, fl)
    v2 = jnp.max(fl2, axis=1, keepdims=True)
    i2 = jnp.min(jnp.where(fl2 == v2, lane, 2.0 * LANES), axis=1, keepdims=True)
    e2 = jnp.exp(v2 - v1)
    g1 = p_grp / (1.0 + e2)
    g2 = p_grp * e2 / (1.0 + e2)
    comb_ref[...] = (jnp.where(lane == i1, g1, 0.0) + jnp.where(lane == i2, g2, 0.0)
                     + jnp.where(lane == GRP_LANE, grp, 0.0))


def _proj_router(hml, hgm, z, x, pa, pb, wo, g, wr, br, tm):
    n = x.shape[0]
    row = lambda blk=0: pl.BlockSpec((tm, D_MODEL), lambda i, blk=blk: (i, blk))
    const = lambda shape: pl.BlockSpec(shape, lambda i: (0, 0), pipeline_mode=pl.Buffered(1))
    wfull = const((D_MODEL, D_MODEL))
    return pl.pallas_call(
        _proj_router_kernel,
        grid=(n // tm,),
        in_specs=[row(), row(), row(ZB_GA), row(ZB_GB), row(), wfull, wfull, wfull,
                  const((1, D_MODEL)), const((D_MODEL, LANES)), const((1, LANES))],
        out_specs=[row(), row(), pl.BlockSpec((tm, LANES), lambda i: (i, 0))],
        out_shape=[jax.ShapeDtypeStruct((n, D_MODEL), f32),
                   jax.ShapeDtypeStruct((n, D_MODEL), bf16),
                   jax.ShapeDtypeStruct((n, LANES), f32)],
        compiler_params=pltpu.CompilerParams(
            dimension_semantics=("parallel",), vmem_limit_bytes=VMEM_LIMIT),
        name="proj_router",
    )(hml, hgm, z, z, x, pa, pb, wo, g, wr, br)


def _moe_final_kernel(xn_ref, comb_ref, x2_ref, ltri_ref, wg_ref, wu_ref, wd_ref, g_ref, y_ref,
                      chl_ref, gp_col_ref, gp_row_ref, pos_col_ref, pos_row_ref):
    grp = pl.program_id(1)
    gf = grp.astype(f32)
    tb = xn_ref.shape[0]

    @pl.when(grp == 0)
    def _():
        y_ref[...] = jnp.zeros_like(y_ref)
        comb = comb_ref[...]
        lane = lax.broadcasted_iota(jnp.int32, comb.shape, 1)
        gcol = comb[:, GRP_LANE:GRP_LANE + 1]
        onehot = jnp.where(lane.astype(f32) == gcol, 1.0, 0.0)
        before = _dot(ltri_ref[...], onehot.astype(bf16))
        pos = jnp.sum(onehot * before, axis=1, keepdims=True)
        gp = jnp.where(lane == 0, gcol, 0.0) + jnp.where(lane == 1, pos, 0.0)
        gp_col_ref[...] = gp
        gp_row_ref[...] = gp.T[:8, :]
        chi = comb.astype(bf16)
        chl_ref[:, :LANES] = chi
        chl_ref[:, LANES:] = (comb - chi.astype(f32)).astype(bf16)

    in_grp_col = gp_col_ref[:, 0:1] == gf
    pos_col_ref[...] = jnp.broadcast_to(jnp.where(in_grp_col, gp_col_ref[:, 1:2], -1.0), (tb, MOE_TILE))
    pos_row_ref[...] = jnp.broadcast_to(jnp.where(gp_row_ref[0:1, :] == gf, gp_row_ref[1:2, :], -1.0), (8, tb))
    cnt = jnp.sum(jnp.where(in_grp_col, 1.0, 0.0)).astype(jnp.int32)

    def tile(t, carry):
        base = (t * MOE_TILE).astype(f32)
        r_iota = lax.broadcasted_iota(jnp.int32, (MOE_TILE, tb), 0).astype(f32)
        gather = jnp.where(pos_row_ref[0:1, :] - base == r_iota, 1.0, 0.0).astype(bf16)
        x = _dot(gather, xn_ref[...]).astype(bf16)
        c2 = _dot(gather, chl_ref[...])
        c = c2[:, :LANES] + c2[:, LANES:]
        lane = lax.broadcasted_iota(jnp.int32, c.shape, 1)
        hid = []
        for e in range(MOE_PER_GROUP):
            ce = jnp.sum(jnp.where(lane == grp * MOE_PER_GROUP + e, c, 0.0), axis=1, keepdims=True)
            a = _dot(x, wg_ref[e])
            u = _dot(x, wu_ref[e])
            hid.append((a * _sigmoid(a) * u * ce).astype(bf16))
        y = _dot(jnp.concatenate(hid, axis=1), wd_ref[...]).astype(bf16)
        l_iota = lax.broadcasted_iota(jnp.int32, (tb, MOE_TILE), 1).astype(f32)
        scatter = jnp.where(pos_col_ref[...] - base == l_iota, 1.0, 0.0).astype(bf16)
        y_ref[...] += _dot(scatter, y)
        return carry

    lax.fori_loop(0, (cnt + MOE_TILE - 1) // MOE_TILE, tile, 0)

    @pl.when(grp == MOE_GROUPS - 1)
    def _():
        y_ref[...] = _rms(x2_ref[...] + y_ref[...], g_ref[...])


def _moe_final(xn, comb, x2, wg, wu, wd, g, tb):
    n = x2.shape[0]
    ltri = jnp.tril(jnp.ones((tb, tb), bf16), -1)
    return pl.pallas_call(
        _moe_final_kernel,
        grid=(n // tb, MOE_GROUPS),
        in_specs=[pl.BlockSpec((tb, D_MODEL), lambda i, j: (i, 0)),
                  pl.BlockSpec((tb, LANES), lambda i, j: (i, 0)),
                  pl.BlockSpec((tb, D_MODEL), lambda i, j: (i, 0)),
                  pl.BlockSpec((tb, tb), lambda i, j: (0, 0)),
                  pl.BlockSpec((MOE_PER_GROUP, D_MODEL, MOE_HIDDEN), lambda i, j: (j, 0, 0)),
                  pl.BlockSpec((MOE_PER_GROUP, D_MODEL, MOE_HIDDEN), lambda i, j: (j, 0, 0)),
                  pl.BlockSpec((None, MOE_PER_GROUP * MOE_HIDDEN, D_MODEL), lambda i, j: (j, 0, 0)),
                  pl.BlockSpec((1, D_MODEL), lambda i, j: (0, 0))],
        out_specs=pl.BlockSpec((tb, D_MODEL), lambda i, j: (i, 0)),
        out_shape=jax.ShapeDtypeStruct((n, D_MODEL), f32),
        scratch_shapes=[pltpu.VMEM((tb, 2 * LANES), bf16),
                        pltpu.VMEM((tb, LANES), f32), pltpu.VMEM((8, tb), f32),
                        pltpu.VMEM((tb, MOE_TILE), f32), pltpu.VMEM((8, tb), f32)],
        compiler_params=pltpu.CompilerParams(
            dimension_semantics=("parallel", "arbitrary"), vmem_limit_bytes=VMEM_LIMIT),
        name="moe_final",
    )(xn, comb, x2, ltri, wg, wu, wd.reshape(MOE_GROUPS, MOE_PER_GROUP * MOE_HIDDEN, D_MODEL), g)


def _layer(x3, state, lw, *, prompt):
    nbatch, t, _ = x3.shape
    n = nbatch * t
    x = x3.reshape(n, D_MODEL)
    if prompt:
        L, valid, z_dtype, tm, nblk = ML_CHUNK, ML_CHUNK, bf16, 512, 4
    else:
        L, valid, z_dtype, tm, nblk = t, lw["valid"], f32, 256, 1
    nb = ROWS // L

    z, zg, zgt = _in_proj(x, lw["norm_mix_g"], lw["w_main"], lw["w_gate"], lw["gm_norm_g"], tm, z_dtype)
    c0, n0, m0 = state
    m0 = jnp.broadcast_to(m0[:, :, None], (nbatch, ML_HEADS, LANES))
    ws_bd, b_col = _gmlp_mixing(lw["gm_ws"], lw["gm_bs"], L)
    hml, hgm, c1, n1, m1 = _mixer_core(z, zg, zgt, c0, n0, m0, lw["ml_b_i"], lw["ml_b_f"], lw["ml_norm_g"],
                                       ws_bd, b_col, nblk=nblk, nb=nb, L=L, valid=valid, h_dtype=z_dtype)
    x2, xn, comb = _proj_router(hml, hgm, z, x, lw["p_a"], lw["p_b"], lw["w_out"], lw["norm_ffn_g"],
                                lw["w_router"], lw["b_router"], PROJ_TILE)
    y = _moe_final(xn, comb, x2, lw["e_wg"], lw["e_wu"], lw["e_wd"], lw["out_g"], MOE_BLOCK)
    gv = z[:, ZB_GV * D_MODEL:(ZB_GV + 1) * D_MODEL].reshape(nbatch, t, D_MODEL)
    return y.reshape(nbatch, t, D_MODEL), (c1, n1, m1[:, :, 0]), gv


def kernel(x_prompt, x_sample, state_mlstm_C, state_mlstm_n, state_mlstm_m, norm_mix_g, w_in, ml_b_i, ml_b_f, ml_norm_g, gm_norm_g, gm_ws, gm_bs, p_a, p_b, w_out, norm_ffn_g, rc_w, rc_b, rf_w, rf_b, e_wg, e_wu, e_wd, final_norm_g):
    depth = w_in.shape[0]
    assert depth == 1, "the final norm is fused into the last layer's MoE kernel; only depth 1 is wired up"
    nbp = x_prompt.shape[0]
    nbs, ts, _ = x_sample.shape

    def layer_weights(l):
        w = w_in[l]
        gates = w[:, GATE_OFF:GATE_OFF + 2 * ML_HEADS]
        return dict(
            norm_mix_g=norm_mix_g[l][None, :],
            w_main=jnp.concatenate([w[:, :GATE_OFF], w[:, GATE_OFF + 2 * ML_HEADS:]], axis=1).astype(bf16),
            w_gate=jnp.pad(gates, ((0, 0), (0, LANES - 2 * ML_HEADS))).astype(bf16),
            ml_b_i=ml_b_i[l], ml_b_f=ml_b_f[l], ml_norm_g=ml_norm_g[l],
            gm_norm_g=gm_norm_g[l].reshape(1, GM_GROUPS * GM_DG),
            gm_ws=gm_ws[l], gm_bs=gm_bs[l],
            p_a=p_a[l].astype(bf16), p_b=p_b[l].astype(bf16), w_out=w_out[l].astype(bf16),
            norm_ffn_g=norm_ffn_g[l][None, :],
            w_router=jnp.pad(
                jnp.concatenate([rf_w[l].transpose(1, 0, 2).reshape(D_MODEL, MOE_EXPERTS), rc_w[l]], axis=1),
                ((0, 0), (0, LANES - MOE_EXPERTS - MOE_GROUPS))).astype(bf16),
            b_router=jnp.pad(jnp.concatenate([rf_b[l].reshape(-1), rc_b[l]]),
                             (0, LANES - MOE_EXPERTS - MOE_GROUPS))[None, :],
            e_wg=e_wg[l].astype(bf16), e_wu=e_wu[l].astype(bf16), e_wd=e_wd[l].astype(bf16),
            out_g=final_norm_g[None, :],
        )

    lw = layer_weights(0)

    zero_state = (jnp.zeros((nbp, ML_HEADS, ML_DV, ML_DQK), f32), jnp.zeros((nbp, ML_HEADS, ML_DQK), f32),
                  jnp.zeros((nbp, ML_HEADS), f32))
    y_p, (c_p, n_p, m_p), _ = _layer(x_prompt, zero_state, lw, prompt=True)

    xs = jnp.pad(x_sample, ((0, 0), (0, SAMPLE_PAD_T - ts), (0, 0)))
    y_s, (c_s, n_s, m_s), v_s = _layer(xs, (state_mlstm_C[0], state_mlstm_n[0], state_mlstm_m[0]),
                                       dict(lw, valid=ts), prompt=False)
    y_s = y_s[:, :ts]
    v_s = v_s[:, :ts].reshape(nbs, ts, GM_GROUPS, GM_DG)
    return (y_p, y_s, c_p[None], n_p[None], m_p[None], c_s[None], n_s[None], m_s[None], v_s[None])
```

```python
import functools

import numpy as np
import jax
import jax.numpy as jnp
from jax import lax
from jax.experimental import pallas as pl
from jax.experimental.pallas import tpu as pltpu

D_MODEL = 1024
ML_HEADS = 4
ML_DQK = 128
ML_DV = 256
ML_CHUNK = 128
GM_GROUPS = 4
GM_DG = 256
MOE_GROUPS = 4
MOE_PER_GROUP = 8
MOE_EXPERTS = MOE_GROUPS * MOE_PER_GROUP
MOE_HIDDEN = 256
EPS = 1e-6

LANES = 128
SAMPLE_PAD_T = 8
ROWS = 128
N_PAD_ROWS = 16
NEG = -1e30
MOE_TILE = 128
MOE_BLOCK = 1024
PROJ_TILE = 1024
PROJ_SUB = 256
GRP_LANE = MOE_EXPERTS
VMEM_LIMIT = 56 * 1024 * 1024

ZB_QK, ZB_V, ZB_OG, ZB_U, ZB_GV, ZB_GA, ZB_GB = range(7)
N_ZB = 7
N_ZB_A = 3
GATE_OFF = 2 * ML_HEADS * ML_DQK + 2 * ML_HEADS * ML_DV

f32 = jnp.float32
bf16 = jnp.bfloat16


def _sigmoid(x):
    return 0.5 * jnp.tanh(0.5 * x) + 0.5


def _log_sigmoid(x):
    return jnp.minimum(x, 0.0) - jnp.log1p(jnp.exp(-jnp.abs(x)))


def _gelu_tanh(x):
    return 0.5 * x * (1.0 + jnp.tanh(np.sqrt(2.0 / np.pi) * (x + 0.044715 * (x * x * x))))


def _rms(x, g):
    return x * lax.rsqrt(jnp.mean(x * x, axis=-1, keepdims=True) + EPS) * g


def _dot(a, b):
    return jnp.dot(a, b, preferred_element_type=f32)


def _dot_nt(a, b):
    return lax.dot_general(a, b, (((1,), (1,)), ((), ())), preferred_element_type=f32)


def _dot_tn(a, b):
    return lax.dot_general(a, b, (((0,), (0,)), ((), ())), preferred_element_type=f32)


def _in_proj_kernel(x_ref, g_ref, wa_ref, wb_ref, wg_ref, gb_ref, gmg_ref, z_ref, zg_ref, zgt_ref):
    xn = _rms(x_ref[...], g_ref[...]).astype(bf16)
    zg = _dot(xn, wg_ref[...]) + gb_ref[...]
    lane = lax.broadcasted_iota(jnp.int32, zg.shape, 1)
    zg = jnp.where((lane >= ML_HEADS) & (lane < 2 * ML_HEADS), _log_sigmoid(zg), zg)
    zg_ref[...] = zg
    zgt_ref[...] = zg.T[:2 * ML_HEADS, :]

    def gelu_group_rms(a, g):
        return _rms(_gelu_tanh(a), gmg_ref[:, g * GM_DG:(g + 1) * GM_DG])

    act = {ZB_QK: None, ZB_V: None, ZB_OG: _sigmoid, ZB_GA: _sigmoid, ZB_GB: _sigmoid,
           ZB_U: _gelu_tanh, ZB_GV: gelu_group_rms}
    per_blk = D_MODEL // GM_DG
    for blk in range(N_ZB):
        for g in range(per_blk):
            cols = slice(blk * D_MODEL + g * GM_DG, blk * D_MODEL + (g + 1) * GM_DG)
            if blk < N_ZB_A:
                a = _dot(xn, wa_ref[:, cols])
            else:
                a = _dot(xn, wb_ref[:, cols.start - N_ZB_A * D_MODEL:cols.stop - N_ZB_A * D_MODEL])
            if blk == ZB_GV:
                a = gelu_group_rms(a, g)
            elif act[blk] is not None:
                a = act[blk](a)
            z_ref[:, cols] = a.astype(z_ref.dtype)


def _in_proj(x, g, w_a, w_b, w_gate, gate_bias, gm_g, tm, z_dtype):
    n = x.shape[0]
    const = lambda shape: pl.BlockSpec(shape, lambda i: (0, 0), pipeline_mode=pl.Buffered(1))
    return pl.pallas_call(
        _in_proj_kernel,
        grid=(n // tm,),
        in_specs=[
            pl.BlockSpec((tm, D_MODEL), lambda i: (i, 0)),
            const((1, D_MODEL)),
            const((D_MODEL, N_ZB_A * D_MODEL)),
            const((D_MODEL, (N_ZB - N_ZB_A) * D_MODEL)),
            const((D_MODEL, LANES)),
            const((1, LANES)),
            const((1, D_MODEL)),
        ],
        out_specs=[
            pl.BlockSpec((tm, N_ZB * D_MODEL), lambda i: (i, 0)),
            pl.BlockSpec((tm, LANES), lambda i: (i, 0)),
            pl.BlockSpec((2 * ML_HEADS, tm), lambda i: (0, i)),
        ],
        out_shape=[
            jax.ShapeDtypeStruct((n, N_ZB * D_MODEL), z_dtype),
            jax.ShapeDtypeStruct((n, LANES), f32),
            jax.ShapeDtypeStruct((2 * ML_HEADS, n), f32),
        ],
        compiler_params=pltpu.CompilerParams(
            dimension_semantics=("parallel",), vmem_limit_bytes=VMEM_LIMIT),
        name="in_proj",
    )(x, g, w_a, w_b, w_gate, gate_bias, gm_g)


def _mixer_core_kernel(zqk_ref, zv_ref, zog_ref, zu_ref, zgv_ref, zg_ref, zgt_ref,
                       c0_ref, n0_ref, m0_ref, mlg_ref, wsbd_ref, bcol_ref, eye_ref,
                       hml_ref, hgm_ref, c_ref, n_ref, m_ref, *, nblk, nb, L, valid, single_chunk):
    if single_chunk:
        cs_ref, ns_ref, ms_ref = c0_ref, n0_ref, m0_ref
    else:
        cs_ref, ns_ref, ms_ref = c_ref, n_ref, m_ref

        @pl.when(pl.program_id(1) == 0)
        def _():
            c_ref[...] = c0_ref[...]
            n_ref[...] = n0_ref[...]
            m_ref[...] = m0_ref[...]

    shift = L.bit_length() - 1
    pp = lax.broadcasted_iota(jnp.int32, (ROWS, ROWS), 0)
    qq = lax.broadcasted_iota(jnp.int32, (ROWS, ROWS), 1)
    same = (pp >> shift) == (qq >> shift)
    p_col = lax.broadcasted_iota(jnp.int32, (ROWS, 1), 0)
    p_row = lax.broadcasted_iota(jnp.int32, (1, ROWS), 1)
    col_ok = (p_col & (L - 1)) < valid
    row_ok = (p_row & (L - 1)) < valid
    causal = same & (qq <= pp) & row_ok
    upper = same & (pp <= qq)
    scale = ML_DQK ** -0.5

    def per_seq(fn):
        return jnp.concatenate([fn(b) for b in range(nb)], axis=0) if nb > 1 else fn(0)

    stores = []
    chains = [(r, h) for r in range(nblk) for h in range(ML_HEADS)]
    each = lambda fn: [fn(i, r, h) for i, (r, h) in enumerate(chains)]

    q = each(lambda i, r, h: zqk_ref[r, :, h * ML_DQK:(h + 1) * ML_DQK])
    k = each(lambda i, r, h: zqk_ref[r, :, (ML_HEADS + h) * ML_DQK:(ML_HEADS + h + 1) * ML_DQK])
    v = each(lambda i, r, h: zv_ref[r, :, h * ML_DV:(h + 1) * ML_DV])
    qc = [x.astype(bf16) for x in q]
    kc = [x.astype(bf16) for x in k]
    c0 = each(lambda i, r, h: cs_ref[r * nb:(r + 1) * nb, h])
    one_seq = nb == 1
    if one_seq:
        m0 = each(lambda i, r, h: ms_ref[r, h:h + 1, 0:1])
    else:
        m0 = each(lambda i, r, h: per_seq(lambda b: jnp.broadcast_to(ms_ref[r * nb + b, h:h + 1, 0:1], (L, 1))))
    n0 = each(lambda i, r, h: per_seq(lambda b: jnp.broadcast_to(ns_ref[r * nb + b, h:h + 1, :], (L, ML_DQK))))
    logi_col = each(lambda i, r, h: zg_ref[r, :, h:h + 1])
    logi_row = each(lambda i, r, h: zgt_ref[r, h:h + 1, :])
    logf_col = each(lambda i, r, h: jnp.where(col_ok, zg_ref[r, :, ML_HEADS + h:ML_HEADS + h + 1], 0.0))
    logf_row = each(lambda i, r, h: jnp.where(row_ok, zgt_ref[r, ML_HEADS + h:ML_HEADS + h + 1, :], 0.0))

    if one_seq:
        n_rows = each(lambda i, r, h: jnp.broadcast_to(ns_ref[r, h:h + 1, :], (N_PAD_ROWS, ML_DQK)).astype(bf16))
        qkc = each(lambda i, r, h: _dot_nt(
            qc[i], jnp.concatenate([kc[i], c0[i].reshape(ML_DV, ML_DQK).astype(bf16), n_rows[i]], axis=0)))
    else:
        qkc = each(lambda i, r, h: _dot_nt(
            qc[i], jnp.concatenate([kc[i], c0[i].reshape(nb * ML_DV, ML_DQK).astype(bf16)], axis=0)))
    b_col = each(lambda i, r, h: jnp.sum(jnp.where(causal, logf_row[i], 0.0), axis=1, keepdims=True))
    b_row = each(lambda i, r, h: jnp.sum(jnp.where(upper, logf_col[i], 0.0), axis=0, keepdims=True))
    if one_seq:
        b_last_col = each(lambda i, r, h: jnp.sum(logf_row[i], axis=1, keepdims=True))
        b_last_row = b_last_col
    else:
        b_last_col = each(lambda i, r, h: jnp.sum(jnp.where(same, logf_row[i], 0.0), axis=1, keepdims=True))
        b_last_row = each(lambda i, r, h: jnp.sum(jnp.where(same, logf_col[i], 0.0), axis=0, keepdims=True))
    d = each(lambda i, r, h: jnp.where(causal, b_col[i] - b_row[i] + logi_row[i], NEG))
    inter = each(lambda i, r, h: b_col[i] + m0[i])
    m_col = each(lambda i, r, h: jnp.maximum(inter[i], jnp.max(d[i], axis=1, keepdims=True)))
    w_intra = each(lambda i, r, h: jnp.exp(d[i] - m_col[i]) * scale)
    w_inter = each(lambda i, r, h: jnp.exp(inter[i] - m_col[i]) * scale)
    s = each(lambda i, r, h: qkc[i][:, :ROWS] * w_intra[i])
    q_mem = each(lambda i, r, h: per_seq(
        lambda b: qkc[i][b * L:(b + 1) * L, ROWS + b * ML_DV:ROWS + (b + 1) * ML_DV]))
    num = each(lambda i, r, h: _dot(s[i].astype(bf16), v[i].astype(bf16)) + w_inter[i] * q_mem[i])
    if one_seq:
        q_n = each(lambda i, r, h: qkc[i][:, ROWS + ML_DV:ROWS + ML_DV + 1])
    else:
        q_n = each(lambda i, r, h: jnp.sum(q[i].astype(f32) * n0[i], axis=1, keepdims=True))
    den = each(lambda i, r, h: jnp.sum(s[i], axis=1, keepdims=True) + w_inter[i] * q_n[i])
    hh = each(lambda i, r, h: _rms(num[i] / jnp.maximum(jnp.abs(den[i]), jnp.exp(-m_col[i])), mlg_ref[h:h + 1, :]))
    for i, (r, h) in enumerate(chains):
        sog = zog_ref[r, :, h * ML_DV:(h + 1) * ML_DV].astype(f32)
        stores.append((hml_ref, (r, slice(None), slice(h * ML_DV, (h + 1) * ML_DV)), (sog * hh[i]).astype(hml_ref.dtype)))

    wend_col = each(lambda i, r, h: jnp.where(col_ok, b_last_col[i] - b_col[i] + logi_col[i], NEG))
    wend_row = each(lambda i, r, h: jnp.where(row_ok, b_last_row[i] - b_row[i] + logi_row[i], NEG))
    if one_seq:
        m_new = each(lambda i, r, h: jnp.maximum(b_last_col[i] + m0[i], jnp.max(wend_row[i], axis=1, keepdims=True)))
    else:
        m_new = each(lambda i, r, h: jnp.maximum(
            b_last_col[i] + m0[i], jnp.max(jnp.where(same, wend_row[i], NEG), axis=1, keepdims=True)))
    decay = each(lambda i, r, h: jnp.exp(b_last_col[i] + m0[i] - m_new[i]))
    wend = each(lambda i, r, h: jnp.exp(wend_col[i] - m_new[i]))
    vw = each(lambda i, r, h: (v[i].astype(f32) * wend[i]).astype(bf16))
    kw = each(lambda i, r, h: k[i].astype(f32) * wend[i])
    if nb == 1:
        upd = each(lambda i, r, h: _dot_tn(vw[i], kc[i]))
    else:
        vw_t = each(lambda i, r, h: _dot_nt(eye_ref[...], vw[i]).astype(bf16))
        seq_of_lane = lax.broadcasted_iota(jnp.int32, (ML_DV, ROWS), 1) >> shift
        upd = each(lambda i, r, h: _dot(jnp.concatenate(
            [jnp.where(seq_of_lane == b, vw_t[i], jnp.zeros_like(vw_t[i])) for b in range(nb)], axis=0), kc[i]))
    for i, (r, h) in enumerate(chains):
        for b in range(nb):
            slot = r * nb + b
            dec = decay[i][b * L:b * L + 1, :]
            stores.append((c_ref, (slot, h), dec * c0[i][b] + upd[i][b * ML_DV:(b + 1) * ML_DV]))
            stores.append((n_ref, (slot, slice(h, h + 1), slice(None)),
                           dec * ns_ref[slot, h:h + 1, :] + jnp.sum(kw[i][b * L:(b + 1) * L], axis=0, keepdims=True)))
            stores.append((m_ref, (slot, slice(h, h + 1), slice(None)),
                           jnp.broadcast_to(m_new[i][b * L:b * L + 1, :], (1, LANES))))

    for r in range(nblk):
        gv = jnp.concatenate([zgv_ref[r, :, g * GM_DG:(g + 1) * GM_DG].astype(bf16) for g in range(GM_GROUPS)], axis=0)
        mixed = _dot(wsbd_ref[...], gv) + bcol_ref[...]
        for g in range(GM_GROUPS):
            u = zu_ref[r, :, g * GM_DG:(g + 1) * GM_DG].astype(f32)
            stores.append((hgm_ref, (r, slice(None), slice(g * GM_DG, (g + 1) * GM_DG)),
                           (u * mixed[g * ROWS:(g + 1) * ROWS]).astype(hgm_ref.dtype)))

    for ref, idx, val in stores:
        ref[idx] = val


def _mixer_core(z, zg, zgt, c0, n0, m0, ml_g, ws_bd, b_col, *, nblk, nb, L, valid, h_dtype):
    n = z.shape[0]
    nbatch = c0.shape[0]
    groups = nbatch // nb
    nc = n // groups // ROWS
    z4 = z.reshape(groups, nc, ROWS, N_ZB * D_MODEL)
    zg4 = zg.reshape(groups, nc, ROWS, LANES)
    zgt4 = zgt.reshape(2 * ML_HEADS, groups, nc, ROWS).transpose(1, 2, 0, 3)
    zspec = lambda blk: pl.BlockSpec((nblk, None, ROWS, D_MODEL), lambda b, c, blk=blk: (b, c, 0, blk))
    full = lambda shape: pl.BlockSpec(shape, lambda b, c: (0,) * len(shape))
    state_specs = [
        pl.BlockSpec((nblk * nb, ML_HEADS, ML_DV, ML_DQK), lambda b, c: (b, 0, 0, 0)),
        pl.BlockSpec((nblk * nb, ML_HEADS, ML_DQK), lambda b, c: (b, 0, 0)),
        pl.BlockSpec((nblk * nb, ML_HEADS, LANES), lambda b, c: (b, 0, 0)),
    ]
    tok_spec = zspec(0)
    eye = jnp.eye(ML_DV, dtype=bf16)
    hml, hgm, c1, n1, m1 = pl.pallas_call(
        functools.partial(_mixer_core_kernel, nblk=nblk, nb=nb, L=L, valid=valid, single_chunk=(nc == 1)),
        grid=(groups // nblk, nc),
        in_specs=[zspec(ZB_QK), zspec(ZB_V), zspec(ZB_OG), zspec(ZB_U), zspec(ZB_GV),
                  pl.BlockSpec((nblk, None, ROWS, LANES), lambda b, c: (b, c, 0, 0)),
                  pl.BlockSpec((nblk, None, 2 * ML_HEADS, ROWS), lambda b, c: (b, c, 0, 0)),
                  *state_specs,
                  full((ML_HEADS, ML_DV)), full((GM_GROUPS * ROWS, GM_GROUPS * ROWS)), full((GM_GROUPS * ROWS, 1)),
                  full((ML_DV, ML_DV))],
        out_specs=[tok_spec, tok_spec] + state_specs,
        out_shape=[
            jax.ShapeDtypeStruct((groups, nc, ROWS, D_MODEL), h_dtype),
            jax.ShapeDtypeStruct((groups, nc, ROWS, D_MODEL), h_dtype),
            jax.ShapeDtypeStruct((nbatch, ML_HEADS, ML_DV, ML_DQK), f32),
            jax.ShapeDtypeStruct((nbatch, ML_HEADS, ML_DQK), f32),
            jax.ShapeDtypeStruct((nbatch, ML_HEADS, LANES), f32),
        ],
        compiler_params=pltpu.CompilerParams(
            dimension_semantics=("parallel", "arbitrary"), vmem_limit_bytes=VMEM_LIMIT),
        name="mixer_core",
    )(z4, z4, z4, z4, z4, zg4, zgt4, c0, n0, m0, ml_g, ws_bd, b_col, eye)
    return hml.reshape(n, D_MODEL), hgm.reshape(n, D_MODEL), c1, n1, m1


def _gmlp_mixing(gm_ws, gm_bs, L):
    reps = ROWS // L
    tril = jnp.tril(jnp.ones((L, L), bool))
    blocks = [jnp.kron(jnp.eye(reps, dtype=f32), jnp.where(tril, gm_ws[g, :L, :L], 0.0)) for g in range(GM_GROUPS)]
    ws_bd = jax.scipy.linalg.block_diag(*blocks).astype(bf16)
    b_col = jnp.concatenate([jnp.tile(gm_bs[g, :L], reps) for g in range(GM_GROUPS)])[:, None]
    return ws_bd, b_col


def _proj_router_kernel(hml_ref, hgm_ref, sga_ref, sgb_ref, x_ref, pa_ref, pb_ref, wo_ref, g_ref, wr_ref, br_ref,
                        x2_ref, xn_ref, comb_ref):
    tm = x_ref.shape[0]
    subs = [slice(r, r + PROJ_SUB) for r in range(0, tm, PROJ_SUB)]
    a = [_dot(hml_ref[r, :].astype(bf16), pa_ref[...]) for r in subs]
    b = [_dot(hgm_ref[r, :].astype(bf16), pb_ref[...]) for r in subs]
    merged = [sga_ref[r, :].astype(f32) * a[i] + sgb_ref[r, :].astype(f32) * b[i] for i, r in enumerate(subs)]
    x2 = [x_ref[r, :] + _dot(merged[i].astype(bf16), wo_ref[...]) for i, r in enumerate(subs)]
    xn = [_rms(x2[i], g_ref[...]).astype(bf16) for i in range(len(subs))]
    for i, r in enumerate(subs):
        x2_ref[r, :] = x2[i]
        xn_ref[r, :] = xn[i]

    lg = jnp.concatenate([_dot(xn[i], wr_ref[...]) for i in range(len(subs))], axis=0) + br_ref[...]
    lane = lax.broadcasted_iota(jnp.int32, lg.shape, 1).astype(f32)
    cmask = (lane >= MOE_EXPERTS) & (lane < MOE_EXPERTS + MOE_GROUPS)
    cl = jnp.where(cmask, lg, NEG)
    cmax = jnp.max(cl, axis=1, keepdims=True)
    p_grp = 1.0 / jnp.sum(jnp.where(cmask, jnp.exp(cl - cmax), 0.0), axis=1, keepdims=True)
    grp = jnp.min(jnp.where(cl == cmax, lane, 2.0 * LANES), axis=1, keepdims=True) - MOE_EXPERTS
    fmask = (lane >= grp * MOE_PER_GROUP) & (lane < (grp + 1.0) * MOE_PER_GROUP)
    fl = jnp.where(fmask, lg, NEG)
    v1 = jnp.max(fl, axis=1, keepdims=True)
    i1 = jnp.min(jnp.where(fl == v1, lane, 2.0 * LANES), axis=1, keepdims=True)
    fl2 = jnp.where(lane == i1, NEG, fl)
    v2 = jnp.max(fl2, axis=1, keepdims=True)
    i2 = jnp.min(jnp.where(fl2 == v2, lane, 2.0 * LANES), axis=1, keepdims=True)
    e2 = jnp.exp(v2 - v1)
    g1 = p_grp / (1.0 + e2)
    g2 = p_grp * e2 / (1.0 + e2)
    comb_ref[...] = (jnp.where(lane == i1, g1, 0.0) + jnp.where(lane == i2, g2, 0.0)
                     + jnp.where(lane == GRP_LANE, grp, 0.0))


def _proj_router(hml, hgm, z, x, pa, pb, wo, g, wr, br, tm):
    n = x.shape[0]
    row = lambda blk=0: pl.BlockSpec((tm, D_MODEL), lambda i, blk=blk: (i, blk))
    const = lambda shape: pl.BlockSpec(shape, lambda i: (0, 0), pipeline_mode=pl.Buffered(1))
    wfull = const((D_MODEL, D_MODEL))
    return pl.pallas_call(
        _proj_router_kernel,
        grid=(n // tm,),
        in_specs=[row(), row(), row(ZB_GA), row(ZB_GB), row(), wfull, wfull, wfull,
                  const((1, D_MODEL)), const((D_MODEL, LANES)), const((1, LANES))],
        out_specs=[row(), row(), pl.BlockSpec((tm, LANES), lambda i: (i, 0))],
        out_shape=[jax.ShapeDtypeStruct((n, D_MODEL), f32),
                   jax.ShapeDtypeStruct((n, D_MODEL), bf16),
                   jax.ShapeDtypeStruct((n, LANES), f32)],
        compiler_params=pltpu.CompilerParams(
            dimension_semantics=("parallel",), vmem_limit_bytes=VMEM_LIMIT),
        name="proj_router",
    )(hml, hgm, z, z, x, pa, pb, wo, g, wr, br)


def _moe_final_kernel(xn_ref, comb_ref, x2_ref, ltri_ref, wg_ref, wu_ref, wd_ref, g_ref, y_ref,
                      chl_ref, gp_col_ref, gp_row_ref, pos_col_ref, pos_row_ref):
    grp = pl.program_id(1)
    gf = grp.astype(f32)
    tb = xn_ref.shape[0]

    @pl.when(grp == 0)
    def _():
        y_ref[...] = jnp.zeros_like(y_ref)
        comb = comb_ref[...]
        lane = lax.broadcasted_iota(jnp.int32, comb.shape, 1)
        gcol = comb[:, GRP_LANE:GRP_LANE + 1]
        onehot = jnp.where(lane.astype(f32) == gcol, 1.0, 0.0)
        before = _dot(ltri_ref[...], onehot.astype(bf16))
        pos = jnp.sum(onehot * before, axis=1, keepdims=True)
        gp = jnp.where(lane == 0, gcol, 0.0) + jnp.where(lane == 1, pos, 0.0)
        gp_col_ref[...] = gp
        gp_row_ref[...] = gp.T[:8, :]
        chi = comb.astype(bf16)
        chl_ref[:, :LANES] = chi
        chl_ref[:, LANES:] = (comb - chi.astype(f32)).astype(bf16)

    in_grp_col = gp_col_ref[:, 0:1] == gf
    pos_col_ref[...] = jnp.broadcast_to(jnp.where(in_grp_col, gp_col_ref[:, 1:2], -1.0), (tb, MOE_TILE))
    pos_row_ref[...] = jnp.broadcast_to(jnp.where(gp_row_ref[0:1, :] == gf, gp_row_ref[1:2, :], -1.0), (8, tb))
    cnt = jnp.sum(jnp.where(in_grp_col, 1.0, 0.0)).astype(jnp.int32)

    def tile(t, carry):
        base = (t * MOE_TILE).astype(f32)
        r_iota = lax.broadcasted_iota(jnp.int32, (MOE_TILE, tb), 0).astype(f32)
        gather = jnp.where(pos_row_ref[0:1, :] - base == r_iota, 1.0, 0.0).astype(bf16)
        x = _dot(gather, xn_ref[...]).astype(bf16)
        c2 = _dot(gather, chl_ref[...])
        c = c2[:, :LANES] + c2[:, LANES:]
        lane = lax.broadcasted_iota(jnp.int32, c.shape, 1)
        hid = []
        for e in range(MOE_PER_GROUP):
            ce = jnp.sum(jnp.where(lane == grp * MOE_PER_GROUP + e, c, 0.0), axis=1, keepdims=True)
            a = _dot(x, wg_ref[e])
            u = _dot(x, wu_ref[e])
            hid.append((a * _sigmoid(a) * u * ce).astype(bf16))
        y = _dot(jnp.concatenate(hid, axis=1), wd_ref[...]).astype(bf16)
        l_iota = lax.broadcasted_iota(jnp.int32, (tb, MOE_TILE), 1).astype(f32)
        scatter = jnp.where(pos_col_ref[...] - base == l_iota, 1.0, 0.0).astype(bf16)
        y_ref[...] += _dot(scatter, y)
        return carry

    lax.fori_loop(0, (cnt + MOE_TILE - 1) // MOE_TILE, tile, 0)

    @pl.when(grp == MOE_GROUPS - 1)
    def _():
        y_ref[...] = _rms(x2_ref[...] + y_ref[...], g_ref[...])


def _moe_final(xn, comb, x2, wg, wu, wd, g, tb):
    n = x2.shape[0]
    ltri = jnp.tril(jnp.ones((tb, tb), bf16), -1)
    return pl.pallas_call(
        _moe_final_kernel,
        grid=(n // tb, MOE_GROUPS),
        in_specs=[pl.BlockSpec((tb, D_MODEL), lambda i, j: (i, 0)),
                  pl.BlockSpec((tb, LANES), lambda i, j: (i, 0)),
                  pl.BlockSpec((tb, D_MODEL), lambda i, j: (i, 0)),
                  pl.BlockSpec((tb, tb), lambda i, j: (0, 0)),
                  pl.BlockSpec((MOE_PER_GROUP, D_MODEL, MOE_HIDDEN), lambda i, j: (j, 0, 0)),
                  pl.BlockSpec((MOE_PER_GROUP, D_MODEL, MOE_HIDDEN), lambda i, j: (j, 0, 0)),
                  pl.BlockSpec((None, MOE_PER_GROUP * MOE_HIDDEN, D_MODEL), lambda i, j: (j, 0, 0)),
                  pl.BlockSpec((1, D_MODEL), lambda i, j: (0, 0))],
        out_specs=pl.BlockSpec((tb, D_MODEL), lambda i, j: (i, 0)),
        out_shape=jax.ShapeDtypeStruct((n, D_MODEL), f32),
        scratch_shapes=[pltpu.VMEM((tb, 2 * LANES), bf16),
                        pltpu.VMEM((tb, LANES), f32), pltpu.VMEM((8, tb), f32),
                        pltpu.VMEM((tb, MOE_TILE), f32), pltpu.VMEM((8, tb), f32)],
        compiler_params=pltpu.CompilerParams(
            dimension_semantics=("parallel", "arbitrary"), vmem_limit_bytes=VMEM_LIMIT),
        name="moe_final",
    )(xn, comb, x2, ltri, wg, wu, wd.reshape(MOE_GROUPS, MOE_PER_GROUP * MOE_HIDDEN, D_MODEL), g)


def _layer(x3, state, lw, *, prompt):
    nbatch, t, _ = x3.shape
    n = nbatch * t
    x = x3.reshape(n, D_MODEL)
    if prompt:
        L, valid, z_dtype, tm, nblk = ML_CHUNK, ML_CHUNK, bf16, 512, 4
    else:
        L, valid, z_dtype, tm, nblk = t, lw["valid"], f32, 256, 1
    nb = ROWS // L

    z, zg, zgt = _in_proj(x, lw["norm_mix_g"], lw["w_a"], lw["w_b"], lw["w_gate"], lw["gate_bias"],
                          lw["gm_norm_g"], tm, z_dtype)
    c0, n0, m0 = state
    m0 = jnp.broadcast_to(m0[:, :, None], (nbatch, ML_HEADS, LANES))
    ws_bd, b_col = _gmlp_mixing(lw["gm_ws"], lw["gm_bs"], L)
    hml, hgm, c1, n1, m1 = _mixer_core(z, zg, zgt, c0, n0, m0, lw["ml_norm_g"],
                                       ws_bd, b_col, nblk=nblk, nb=nb, L=L, valid=valid, h_dtype=z_dtype)
    x2, xn, comb = _proj_router(hml, hgm, z, x, lw["p_a"], lw["p_b"], lw["w_out"], lw["norm_ffn_g"],
                                lw["w_router"], lw["b_router"], PROJ_TILE)
    y = _moe_final(xn, comb, x2, lw["e_wg"], lw["e_wu"], lw["e_wd"], lw["out_g"], MOE_BLOCK)
    gv = z[:, ZB_GV * D_MODEL:(ZB_GV + 1) * D_MODEL].reshape(nbatch, t, D_MODEL)
    return y.reshape(nbatch, t, D_MODEL), (c1, n1, m1[:, :, 0]), gv


def kernel(x_prompt, x_sample, state_mlstm_C, state_mlstm_n, state_mlstm_m, norm_mix_g, w_in, ml_b_i, ml_b_f, ml_norm_g, gm_norm_g, gm_ws, gm_bs, p_a, p_b, w_out, norm_ffn_g, rc_w, rc_b, rf_w, rf_b, e_wg, e_wu, e_wd, final_norm_g):
    depth = w_in.shape[0]
    assert depth == 1, "the final norm is fused into the last layer's MoE kernel; only depth 1 is wired up"
    nbp = x_prompt.shape[0]
    nbs, ts, _ = x_sample.shape

    def layer_weights(l):
        w = w_in[l]
        gates = w[:, GATE_OFF:GATE_OFF + 2 * ML_HEADS]
        return dict(
            norm_mix_g=norm_mix_g[l][None, :],
            w_a=w[:, :GATE_OFF].astype(bf16), w_b=w[:, GATE_OFF + 2 * ML_HEADS:].astype(bf16),
            w_gate=jnp.pad(gates, ((0, 0), (0, LANES - 2 * ML_HEADS))).astype(bf16),
            gate_bias=jnp.pad(jnp.concatenate([ml_b_i[l], ml_b_f[l]]), (0, LANES - 2 * ML_HEADS))[None, :],
            ml_norm_g=ml_norm_g[l],
            gm_norm_g=gm_norm_g[l].reshape(1, GM_GROUPS * GM_DG),
            gm_ws=gm_ws[l], gm_bs=gm_bs[l],
            p_a=p_a[l].astype(bf16), p_b=p_b[l].astype(bf16), w_out=w_out[l].astype(bf16),
            norm_ffn_g=norm_ffn_g[l][None, :],
            w_router=jnp.pad(
                jnp.concatenate([rf_w[l].transpose(1, 0, 2).reshape(D_MODEL, MOE_EXPERTS), rc_w[l]], axis=1),
                ((0, 0), (0, LANES - MOE_EXPERTS - MOE_GROUPS))).astype(bf16),
            b_router=jnp.pad(jnp.concatenate([rf_b[l].reshape(-1), rc_b[l]]),
                             (0, LANES - MOE_EXPERTS - MOE_GROUPS))[None, :],
            e_wg=e_wg[l].astype(bf16), e_wu=e_wu[l].astype(bf16), e_wd=e_wd[l].astype(bf16),
            out_g=final_norm_g[None, :],
        )

    lw = layer_weights(0)

    zero_state = (jnp.zeros((nbp, ML_HEADS, ML_DV, ML_DQK), f32), jnp.zeros((nbp, ML_HEADS, ML_DQK), f32),
                  jnp.zeros((nbp, ML_HEADS), f32))
    y_p, (c_p, n_p, m_p), _ = _layer(x_prompt, zero_state, lw, prompt=True)

    xs = jnp.pad(x_sample, ((0, 0), (0, SAMPLE_PAD_T - ts), (0, 0)))
    y_s, (c_s, n_s, m_s), v_s = _layer(xs, (state_mlstm_C[0], state_mlstm_n[0], state_mlstm_m[0]),
                                       dict(lw, valid=ts), prompt=False)
    y_s = y_s[:, :ts]
    v_s = v_s[:, :ts].reshape(nbs, ts, GM_GROUPS, GM_DG)
    return (y_p, y_s, c_p[None], n_p[None], m_p[None], c_s[None], n_s[None], m_s[None], v_s[None])
```

```python
import functools

import numpy as np
import jax
import jax.numpy as jnp
from jax import lax
from jax.experimental import pallas as pl
from jax.experimental.pallas import tpu as pltpu

D_MODEL = 1024
ML_HEADS = 4
ML_DQK = 128
ML_DV = 256
ML_CHUNK = 128
GM_GROUPS = 4
GM_DG = 256
MOE_GROUPS = 4
MOE_PER_GROUP = 8
MOE_EXPERTS = MOE_GROUPS * MOE_PER_GROUP
MOE_HIDDEN = 256
EPS = 1e-6

LANES = 128
SAMPLE_PAD_T = 8
ROWS = 128
N_PAD_ROWS = 16
NEG = -1e30
MOE_TILE = 144
MOE_BLOCK = 1024
PROJ_TILE = 1024
PROJ_SUB = 256
GRP_LANE = MOE_EXPERTS
VMEM_LIMIT = 56 * 1024 * 1024

ZB_QK, ZB_V, ZB_OG, ZB_U, ZB_GV, ZB_GA, ZB_GB = range(7)
N_ZB = 7
N_ZB_A = 3
GATE_OFF = 2 * ML_HEADS * ML_DQK + 2 * ML_HEADS * ML_DV

f32 = jnp.float32
bf16 = jnp.bfloat16


def _sigmoid(x):
    return 0.5 * jnp.tanh(0.5 * x) + 0.5


def _log_sigmoid(x):
    return jnp.minimum(x, 0.0) - jnp.log1p(jnp.exp(-jnp.abs(x)))


def _gelu_tanh(x):
    return 0.5 * x * (1.0 + jnp.tanh(np.sqrt(2.0 / np.pi) * (x + 0.044715 * (x * x * x))))


def _rms(x, g):
    return x * lax.rsqrt(jnp.mean(x * x, axis=-1, keepdims=True) + EPS) * g


def _dot(a, b):
    return jnp.dot(a, b, preferred_element_type=f32)


def _dot_nt(a, b):
    return lax.dot_general(a, b, (((1,), (1,)), ((), ())), preferred_element_type=f32)


def _dot_tn(a, b):
    return lax.dot_general(a, b, (((0,), (0,)), ((), ())), preferred_element_type=f32)


def _in_proj_kernel(x_ref, g_ref, wa_ref, wb_ref, wg_ref, gb_ref, gmg_ref, z_ref, zg_ref, zgt_ref):
    xn = _rms(x_ref[...], g_ref[...]).astype(bf16)
    zg = _dot(xn, wg_ref[...]) + gb_ref[...]
    lane = lax.broadcasted_iota(jnp.int32, zg.shape, 1)
    zg = jnp.where((lane >= ML_HEADS) & (lane < 2 * ML_HEADS), _log_sigmoid(zg), zg)
    zg_ref[...] = zg
    zgt_ref[...] = zg.T[:2 * ML_HEADS, :]

    def gelu_group_rms(a, g):
        return _rms(_gelu_tanh(a), gmg_ref[:, g * GM_DG:(g + 1) * GM_DG])

    act = {ZB_QK: None, ZB_V: None, ZB_OG: _sigmoid, ZB_GA: _sigmoid, ZB_GB: _sigmoid,
           ZB_U: _gelu_tanh, ZB_GV: gelu_group_rms}
    per_blk = D_MODEL // GM_DG
    for blk in range(N_ZB):
        for g in range(per_blk):
            cols = slice(blk * D_MODEL + g * GM_DG, blk * D_MODEL + (g + 1) * GM_DG)
            if blk < N_ZB_A:
                a = _dot(xn, wa_ref[:, cols])
            else:
                a = _dot(xn, wb_ref[:, cols.start - N_ZB_A * D_MODEL:cols.stop - N_ZB_A * D_MODEL])
            if blk == ZB_GV:
                a = gelu_group_rms(a, g)
            elif act[blk] is not None:
                a = act[blk](a)
            z_ref[:, cols] = a.astype(z_ref.dtype)


def _in_proj(x, g, w_a, w_b, w_gate, gate_bias, gm_g, tm, z_dtype):
    n = x.shape[0]
    const = lambda shape: pl.BlockSpec(shape, lambda i: (0, 0), pipeline_mode=pl.Buffered(1))
    return pl.pallas_call(
        _in_proj_kernel,
        grid=(n // tm,),
        in_specs=[
            pl.BlockSpec((tm, D_MODEL), lambda i: (i, 0)),
            const((1, D_MODEL)),
            const((D_MODEL, N_ZB_A * D_MODEL)),
            const((D_MODEL, (N_ZB - N_ZB_A) * D_MODEL)),
            const((D_MODEL, LANES)),
            const((1, LANES)),
            const((1, D_MODEL)),
        ],
        out_specs=[
            pl.BlockSpec((tm, N_ZB * D_MODEL), lambda i: (i, 0)),
            pl.BlockSpec((tm, LANES), lambda i: (i, 0)),
            pl.BlockSpec((2 * ML_HEADS, tm), lambda i: (0, i)),
        ],
        out_shape=[
            jax.ShapeDtypeStruct((n, N_ZB * D_MODEL), z_dtype),
            jax.ShapeDtypeStruct((n, LANES), f32),
            jax.ShapeDtypeStruct((2 * ML_HEADS, n), f32),
        ],
        compiler_params=pltpu.CompilerParams(
            dimension_semantics=("parallel",), vmem_limit_bytes=VMEM_LIMIT),
        name="in_proj",
    )(x, g, w_a, w_b, w_gate, gate_bias, gm_g)


def _mixer_core_kernel(zqk_ref, zv_ref, zog_ref, zu_ref, zgv_ref, zg_ref, zgt_ref,
                       c0_ref, n0_ref, m0_ref, mlg_ref, wsbd_ref, bcol_ref, eye_ref,
                       hml_ref, hgm_ref, c_ref, n_ref, m_ref, *, nblk, nb, L, valid, single_chunk):
    if single_chunk:
        cs_ref, ns_ref, ms_ref = c0_ref, n0_ref, m0_ref
    else:
        cs_ref, ns_ref, ms_ref = c_ref, n_ref, m_ref

        @pl.when(pl.program_id(1) == 0)
        def _():
            c_ref[...] = c0_ref[...]
            n_ref[...] = n0_ref[...]
            m_ref[...] = m0_ref[...]

    shift = L.bit_length() - 1
    pp = lax.broadcasted_iota(jnp.int32, (ROWS, ROWS), 0)
    qq = lax.broadcasted_iota(jnp.int32, (ROWS, ROWS), 1)
    same = (pp >> shift) == (qq >> shift)
    p_col = lax.broadcasted_iota(jnp.int32, (ROWS, 1), 0)
    p_row = lax.broadcasted_iota(jnp.int32, (1, ROWS), 1)
    col_ok = (p_col & (L - 1)) < valid
    row_ok = (p_row & (L - 1)) < valid
    causal = same & (qq <= pp) & row_ok
    upper = same & (pp <= qq)
    scale = ML_DQK ** -0.5

    def per_seq(fn):
        return jnp.concatenate([fn(b) for b in range(nb)], axis=0) if nb > 1 else fn(0)

    stores = []
    chains = [(r, h) for r in range(nblk) for h in range(ML_HEADS)]
    each = lambda fn: [fn(i, r, h) for i, (r, h) in enumerate(chains)]

    q = each(lambda i, r, h: zqk_ref[r, :, h * ML_DQK:(h + 1) * ML_DQK])
    k = each(lambda i, r, h: zqk_ref[r, :, (ML_HEADS + h) * ML_DQK:(ML_HEADS + h + 1) * ML_DQK])
    v = each(lambda i, r, h: zv_ref[r, :, h * ML_DV:(h + 1) * ML_DV])
    qc = [x.astype(bf16) for x in q]
    kc = [x.astype(bf16) for x in k]
    c0 = each(lambda i, r, h: cs_ref[r * nb:(r + 1) * nb, h])
    one_seq = nb == 1
    if one_seq:
        m0 = each(lambda i, r, h: ms_ref[r, h:h + 1, 0:1])
    else:
        m0 = each(lambda i, r, h: per_seq(lambda b: jnp.broadcast_to(ms_ref[r * nb + b, h:h + 1, 0:1], (L, 1))))
    n0 = each(lambda i, r, h: per_seq(lambda b: jnp.broadcast_to(ns_ref[r * nb + b, h:h + 1, :], (L, ML_DQK))))
    logi_col = each(lambda i, r, h: zg_ref[r, :, h:h + 1])
    logi_row = each(lambda i, r, h: zgt_ref[r, h:h + 1, :])
    logf_col = each(lambda i, r, h: jnp.where(col_ok, zg_ref[r, :, ML_HEADS + h:ML_HEADS + h + 1], 0.0))
    logf_row = each(lambda i, r, h: jnp.where(row_ok, zgt_ref[r, ML_HEADS + h:ML_HEADS + h + 1, :], 0.0))

    if one_seq:
        n_rows = each(lambda i, r, h: jnp.broadcast_to(ns_ref[r, h:h + 1, :], (N_PAD_ROWS, ML_DQK)).astype(bf16))
        qkc = each(lambda i, r, h: _dot_nt(
            qc[i], jnp.concatenate([kc[i], c0[i].reshape(ML_DV, ML_DQK).astype(bf16), n_rows[i]], axis=0)))
    else:
        qkc = each(lambda i, r, h: _dot_nt(
            qc[i], jnp.concatenate([kc[i], c0[i].reshape(nb * ML_DV, ML_DQK).astype(bf16)], axis=0)))
    b_col = each(lambda i, r, h: jnp.sum(jnp.where(causal, logf_row[i], 0.0), axis=1, keepdims=True))
    b_row = each(lambda i, r, h: jnp.sum(jnp.where(upper, logf_col[i], 0.0), axis=0, keepdims=True))
    if one_seq:
        b_last_col = each(lambda i, r, h: jnp.sum(logf_row[i], axis=1, keepdims=True))
        b_last_row = b_last_col
    else:
        b_last_col = each(lambda i, r, h: jnp.sum(jnp.where(same, logf_row[i], 0.0), axis=1, keepdims=True))
        b_last_row = each(lambda i, r, h: jnp.sum(jnp.where(same, logf_col[i], 0.0), axis=0, keepdims=True))
    d = each(lambda i, r, h: jnp.where(causal, b_col[i] - b_row[i] + logi_row[i], NEG))
    inter = each(lambda i, r, h: b_col[i] + m0[i])
    m_col = each(lambda i, r, h: jnp.maximum(inter[i], jnp.max(d[i], axis=1, keepdims=True)))
    w_intra = each(lambda i, r, h: jnp.exp(d[i] - m_col[i]) * scale)
    w_inter = each(lambda i, r, h: jnp.exp(inter[i] - m_col[i]) * scale)
    s = each(lambda i, r, h: qkc[i][:, :ROWS] * w_intra[i])
    q_mem = each(lambda i, r, h: per_seq(
        lambda b: qkc[i][b * L:(b + 1) * L, ROWS + b * ML_DV:ROWS + (b + 1) * ML_DV]))
    num = each(lambda i, r, h: _dot(s[i].astype(bf16), v[i].astype(bf16)) + w_inter[i] * q_mem[i])
    if one_seq:
        q_n = each(lambda i, r, h: qkc[i][:, ROWS + ML_DV:ROWS + ML_DV + 1])
    else:
        q_n = each(lambda i, r, h: jnp.sum(q[i].astype(f32) * n0[i], axis=1, keepdims=True))
    den = each(lambda i, r, h: jnp.sum(s[i], axis=1, keepdims=True) + w_inter[i] * q_n[i])
    hh = each(lambda i, r, h: _rms(num[i] / jnp.maximum(jnp.abs(den[i]), jnp.exp(-m_col[i])), mlg_ref[h:h + 1, :]))
    for i, (r, h) in enumerate(chains):
        sog = zog_ref[r, :, h * ML_DV:(h + 1) * ML_DV].astype(f32)
        stores.append((hml_ref, (r, slice(None), slice(h * ML_DV, (h + 1) * ML_DV)), (sog * hh[i]).astype(hml_ref.dtype)))

    wend_col = each(lambda i, r, h: jnp.where(col_ok, b_last_col[i] - b_col[i] + logi_col[i], NEG))
    wend_row = each(lambda i, r, h: jnp.where(row_ok, b_last_row[i] - b_row[i] + logi_row[i], NEG))
    if one_seq:
        m_new = each(lambda i, r, h: jnp.maximum(b_last_col[i] + m0[i], jnp.max(wend_row[i], axis=1, keepdims=True)))
    else:
        m_new = each(lambda i, r, h: jnp.maximum(
            b_last_col[i] + m0[i], jnp.max(jnp.where(same, wend_row[i], NEG), axis=1, keepdims=True)))
    decay = each(lambda i, r, h: jnp.exp(b_last_col[i] + m0[i] - m_new[i]))
    wend = each(lambda i, r, h: jnp.exp(wend_col[i] - m_new[i]))
    vw = each(lambda i, r, h: (v[i].astype(f32) * wend[i]).astype(bf16))
    kw = each(lambda i, r, h: k[i].astype(f32) * wend[i])
    if nb == 1:
        upd = each(lambda i, r, h: _dot_tn(vw[i], kc[i]))
    else:
        vw_t = each(lambda i, r, h: _dot_nt(eye_ref[...], vw[i]).astype(bf16))
        seq_of_lane = lax.broadcasted_iota(jnp.int32, (ML_DV, ROWS), 1) >> shift
        upd = each(lambda i, r, h: _dot(jnp.concatenate(
            [jnp.where(seq_of_lane == b, vw_t[i], jnp.zeros_like(vw_t[i])) for b in range(nb)], axis=0), kc[i]))
    for i, (r, h) in enumerate(chains):
        for b in range(nb):
            slot = r * nb + b
            dec = decay[i][b * L:b * L + 1, :]
            stores.append((c_ref, (slot, h), dec * c0[i][b] + upd[i][b * ML_DV:(b + 1) * ML_DV]))
            stores.append((n_ref, (slot, slice(h, h + 1), slice(None)),
                           dec * ns_ref[slot, h:h + 1, :] + jnp.sum(kw[i][b * L:(b + 1) * L], axis=0, keepdims=True)))
            stores.append((m_ref, (slot, slice(h, h + 1), slice(None)),
                           jnp.broadcast_to(m_new[i][b * L:b * L + 1, :], (1, LANES))))

    for r in range(nblk):
        gv = jnp.concatenate([zgv_ref[r, :, g * GM_DG:(g + 1) * GM_DG].astype(bf16) for g in range(GM_GROUPS)], axis=0)
        mixed = _dot(wsbd_ref[...], gv) + bcol_ref[...]
        for g in range(GM_GROUPS):
            u = zu_ref[r, :, g * GM_DG:(g + 1) * GM_DG].astype(f32)
            stores.append((hgm_ref, (r, slice(None), slice(g * GM_DG, (g + 1) * GM_DG)),
                           (u * mixed[g * ROWS:(g + 1) * ROWS]).astype(hgm_ref.dtype)))

    for ref, idx, val in stores:
        ref[idx] = val


def _mixer_core(z, zg, zgt, c0, n0, m0, ml_g, ws_bd, b_col, *, nblk, nb, L, valid, h_dtype):
    n = z.shape[0]
    nbatch = c0.shape[0]
    groups = nbatch // nb
    nc = n // groups // ROWS
    z4 = z.reshape(groups, nc, ROWS, N_ZB * D_MODEL)
    zg4 = zg.reshape(groups, nc, ROWS, LANES)
    zgt4 = zgt.reshape(2 * ML_HEADS, groups, nc, ROWS).transpose(1, 2, 0, 3)
    zspec = lambda blk: pl.BlockSpec((nblk, None, ROWS, D_MODEL), lambda b, c, blk=blk: (b, c, 0, blk))
    full = lambda shape: pl.BlockSpec(shape, lambda b, c: (0,) * len(shape))
    state_specs = [
        pl.BlockSpec((nblk * nb, ML_HEADS, ML_DV, ML_DQK), lambda b, c: (b, 0, 0, 0)),
        pl.BlockSpec((nblk * nb, ML_HEADS, ML_DQK), lambda b, c: (b, 0, 0)),
        pl.BlockSpec((nblk * nb, ML_HEADS, LANES), lambda b, c: (b, 0, 0)),
    ]
    tok_spec = zspec(0)
    eye = jnp.eye(ML_DV, dtype=bf16)
    hml, hgm, c1, n1, m1 = pl.pallas_call(
        functools.partial(_mixer_core_kernel, nblk=nblk, nb=nb, L=L, valid=valid, single_chunk=(nc == 1)),
        grid=(groups // nblk, nc),
        in_specs=[zspec(ZB_QK), zspec(ZB_V), zspec(ZB_OG), zspec(ZB_U), zspec(ZB_GV),
                  pl.BlockSpec((nblk, None, ROWS, LANES), lambda b, c: (b, c, 0, 0)),
                  pl.BlockSpec((nblk, None, 2 * ML_HEADS, ROWS), lambda b, c: (b, c, 0, 0)),
                  *state_specs,
                  full((ML_HEADS, ML_DV)), full((GM_GROUPS * ROWS, GM_GROUPS * ROWS)), full((GM_GROUPS * ROWS, 1)),
                  full((ML_DV, ML_DV))],
        out_specs=[tok_spec, tok_spec] + state_specs,
        out_shape=[
            jax.ShapeDtypeStruct((groups, nc, ROWS, D_MODEL), h_dtype),
            jax.ShapeDtypeStruct((groups, nc, ROWS, D_MODEL), h_dtype),
            jax.ShapeDtypeStruct((nbatch, ML_HEADS, ML_DV, ML_DQK), f32),
            jax.ShapeDtypeStruct((nbatch, ML_HEADS, ML_DQK), f32),
            jax.ShapeDtypeStruct((nbatch, ML_HEADS, LANES), f32),
        ],
        compiler_params=pltpu.CompilerParams(
            dimension_semantics=("parallel", "arbitrary"), vmem_limit_bytes=VMEM_LIMIT),
        name="mixer_core",
    )(z4, z4, z4, z4, z4, zg4, zgt4, c0, n0, m0, ml_g, ws_bd, b_col, eye)
    return hml.reshape(n, D_MODEL), hgm.reshape(n, D_MODEL), c1, n1, m1


def _gmlp_mixing(gm_ws, gm_bs, L):
    reps = ROWS // L
    tril = jnp.tril(jnp.ones((L, L), bool))
    blocks = [jnp.kron(jnp.eye(reps, dtype=f32), jnp.where(tril, gm_ws[g, :L, :L], 0.0)) for g in range(GM_GROUPS)]
    ws_bd = jax.scipy.linalg.block_diag(*blocks).astype(bf16)
    b_col = jnp.concatenate([jnp.tile(gm_bs[g, :L], reps) for g in range(GM_GROUPS)])[:, None]
    return ws_bd, b_col


def _proj_router_kernel(hml_ref, hgm_ref, sga_ref, sgb_ref, x_ref, pa_ref, pb_ref, wo_ref, g_ref, wr_ref, br_ref,
                        x2_ref, xn_ref, comb_ref):
    tm = x_ref.shape[0]
    subs = [slice(r, r + PROJ_SUB) for r in range(0, tm, PROJ_SUB)]
    a = [_dot(hml_ref[r, :].astype(bf16), pa_ref[...]) for r in subs]
    b = [_dot(hgm_ref[r, :].astype(bf16), pb_ref[...]) for r in subs]
    merged = [sga_ref[r, :].astype(f32) * a[i] + sgb_ref[r, :].astype(f32) * b[i] for i, r in enumerate(subs)]
    x2 = [x_ref[r, :] + _dot(merged[i].astype(bf16), wo_ref[...]) for i, r in enumerate(subs)]
    xn = [_rms(x2[i], g_ref[...]).astype(bf16) for i in range(len(subs))]
    for i, r in enumerate(subs):
        x2_ref[r, :] = x2[i]
        xn_ref[r, :] = xn[i]

    lg = jnp.concatenate([_dot(xn[i], wr_ref[...]) for i in range(len(subs))], axis=0) + br_ref[...]
    lane = lax.broadcasted_iota(jnp.int32, lg.shape, 1).astype(f32)
    cmask = (lane >= MOE_EXPERTS) & (lane < MOE_EXPERTS + MOE_GROUPS)
    cl = jnp.where(cmask, lg, NEG)
    cmax = jnp.max(cl, axis=1, keepdims=True)
    p_grp = 1.0 / jnp.sum(jnp.where(cmask, jnp.exp(cl - cmax), 0.0), axis=1, keepdims=True)
    grp = jnp.min(jnp.where(cl == cmax, lane, 2.0 * LANES), axis=1, keepdims=True) - MOE_EXPERTS
    fmask = (lane >= grp * MOE_PER_GROUP) & (lane < (grp + 1.0) * MOE_PER_GROUP)
    fl = jnp.where(fmask, lg, NEG)
    v1 = jnp.max(fl, axis=1, keepdims=True)
    i1 = jnp.min(jnp.where(fl == v1, lane, 2.0 * LANES), axis=1, keepdims=True)
    fl2 = jnp.where(lane == i1, NEG, fl)
    v2 = jnp.max(fl2, axis=1, keepdims=True)
    i2 = jnp.min(jnp.where(fl2 == v2, lane, 2.0 * LANES), axis=1, keepdims=True)
    e2 = jnp.exp(v2 - v1)
    g1 = p_grp / (1.0 + e2)
    g2 = p_grp * e2 / (1.0 + e2)
    comb_ref[...] = (jnp.where(lane == i1, g1, 0.0) + jnp.where(lane == i2, g2, 0.0)
                     + jnp.where(lane == GRP_LANE, grp, 0.0))


def _proj_router(hml, hgm, sga, sgb, ga_blk, gb_blk, x, pa, pb, wo, g, wr, br, tm):
    n = x.shape[0]
    row = lambda blk=0: pl.BlockSpec((tm, D_MODEL), lambda i, blk=blk: (i, blk))
    const = lambda shape: pl.BlockSpec(shape, lambda i: (0, 0), pipeline_mode=pl.Buffered(1))
    wfull = const((D_MODEL, D_MODEL))
    return pl.pallas_call(
        _proj_router_kernel,
        grid=(n // tm,),
        in_specs=[row(), row(), row(ga_blk), row(gb_blk), row(), wfull, wfull, wfull,
                  const((1, D_MODEL)), const((D_MODEL, LANES)), const((1, LANES))],
        out_specs=[row(), row(), pl.BlockSpec((tm, LANES), lambda i: (i, 0))],
        out_shape=[jax.ShapeDtypeStruct((n, D_MODEL), f32),
                   jax.ShapeDtypeStruct((n, D_MODEL), bf16),
                   jax.ShapeDtypeStruct((n, LANES), f32)],
        compiler_params=pltpu.CompilerParams(
            dimension_semantics=("parallel",), vmem_limit_bytes=VMEM_LIMIT),
        name="proj_router",
    )(hml, hgm, sga, sgb, x, pa, pb, wo, g, wr, br)


def _moe_final_kernel(xn_ref, comb_ref, x2_ref, ltri_ref, wg_ref, wu_ref, wd_ref, g_ref, y_ref,
                      chl_ref, gp_col_ref, gp_row_ref, pos_col_ref, pos_row_ref):
    grp = pl.program_id(1)
    gf = grp.astype(f32)
    tb = xn_ref.shape[0]

    @pl.when(grp == 0)
    def _():
        y_ref[...] = jnp.zeros_like(y_ref)
        comb = comb_ref[...]
        lane = lax.broadcasted_iota(jnp.int32, comb.shape, 1)
        gcol = comb[:, GRP_LANE:GRP_LANE + 1]
        onehot = jnp.where(lane.astype(f32) == gcol, 1.0, 0.0)
        before = _dot(ltri_ref[...], onehot.astype(bf16))
        pos = jnp.sum(onehot * before, axis=1, keepdims=True)
        gp = jnp.where(lane == 0, gcol, 0.0) + jnp.where(lane == 1, pos, 0.0)
        gp_col_ref[...] = gp
        gp_row_ref[...] = gp.T[:8, :]
        chi = comb.astype(bf16)
        chl_ref[:, :LANES] = chi
        chl_ref[:, LANES:] = (comb - chi.astype(f32)).astype(bf16)

    in_grp_col = gp_col_ref[:, 0:1] == gf
    pos_col_ref[...] = jnp.broadcast_to(jnp.where(in_grp_col, gp_col_ref[:, 1:2], -1.0), (tb, MOE_TILE))
    pos_row_ref[...] = jnp.broadcast_to(jnp.where(gp_row_ref[0:1, :] == gf, gp_row_ref[1:2, :], -1.0), (8, tb))
    cnt = jnp.sum(jnp.where(in_grp_col, 1.0, 0.0)).astype(jnp.int32)

    def tile(t, carry):
        base = (t * MOE_TILE).astype(f32)
        r_iota = lax.broadcasted_iota(jnp.int32, (MOE_TILE, tb), 0).astype(f32)
        gather = jnp.where(pos_row_ref[0:1, :] - base == r_iota, 1.0, 0.0).astype(bf16)
        x = _dot(gather, xn_ref[...]).astype(bf16)
        c2 = _dot(gather, chl_ref[...])
        c = c2[:, :LANES] + c2[:, LANES:]
        lane = lax.broadcasted_iota(jnp.int32, c.shape, 1)
        hid = []
        for e in range(MOE_PER_GROUP):
            ce = jnp.sum(jnp.where(lane == grp * MOE_PER_GROUP + e, c, 0.0), axis=1, keepdims=True)
            a = _dot(x, wg_ref[e])
            u = _dot(x, wu_ref[e])
            hid.append((a * _sigmoid(a) * u * ce).astype(bf16))
        y = _dot(jnp.concatenate(hid, axis=1), wd_ref[...]).astype(bf16)
        l_iota = lax.broadcasted_iota(jnp.int32, (tb, MOE_TILE), 1).astype(f32)
        scatter = jnp.where(pos_col_ref[...] - base == l_iota, 1.0, 0.0).astype(bf16)
        y_ref[...] += _dot(scatter, y)
        return carry

    lax.fori_loop(0, (cnt + MOE_TILE - 1) // MOE_TILE, tile, 0)

    @pl.when(grp == MOE_GROUPS - 1)
    def _():
        y_ref[...] = _rms(x2_ref[...] + y_ref[...], g_ref[...])


def _moe_final(xn, comb, x2, wg, wu, wd, g, tb):
    n = x2.shape[0]
    ltri = jnp.tril(jnp.ones((tb, tb), bf16), -1)
    return pl.pallas_call(
        _moe_final_kernel,
        grid=(n // tb, MOE_GROUPS),
        in_specs=[pl.BlockSpec((tb, D_MODEL), lambda i, j: (i, 0)),
                  pl.BlockSpec((tb, LANES), lambda i, j: (i, 0)),
                  pl.BlockSpec((tb, D_MODEL), lambda i, j: (i, 0)),
                  pl.BlockSpec((tb, tb), lambda i, j: (0, 0)),
                  pl.BlockSpec((MOE_PER_GROUP, D_MODEL, MOE_HIDDEN), lambda i, j: (j, 0, 0)),
                  pl.BlockSpec((MOE_PER_GROUP, D_MODEL, MOE_HIDDEN), lambda i, j: (j, 0, 0)),
                  pl.BlockSpec((None, MOE_PER_GROUP * MOE_HIDDEN, D_MODEL), lambda i, j: (j, 0, 0)),
                  pl.BlockSpec((1, D_MODEL), lambda i, j: (0, 0))],
        out_specs=pl.BlockSpec((tb, D_MODEL), lambda i, j: (i, 0)),
        out_shape=jax.ShapeDtypeStruct((n, D_MODEL), f32),
        scratch_shapes=[pltpu.VMEM((tb, 2 * LANES), bf16),
                        pltpu.VMEM((tb, LANES), f32), pltpu.VMEM((8, tb), f32),
                        pltpu.VMEM((tb, MOE_TILE), f32), pltpu.VMEM((8, tb), f32)],
        compiler_params=pltpu.CompilerParams(
            dimension_semantics=("parallel", "arbitrary"), vmem_limit_bytes=VMEM_LIMIT),
        name="moe_final",
    )(xn, comb, x2, ltri, wg, wu, wd.reshape(MOE_GROUPS, MOE_PER_GROUP * MOE_HIDDEN, D_MODEL), g)


def _layer(x3, state, lw, *, prompt):
    nbatch, t, _ = x3.shape
    n = nbatch * t
    x = x3.reshape(n, D_MODEL)
    if prompt:
        L, valid, z_dtype, tm, nblk = ML_CHUNK, ML_CHUNK, bf16, 512, 4
    else:
        L, valid, z_dtype, tm, nblk = t, lw["valid"], f32, 256, 1
    nb = ROWS // L

    z, zg, zgt = _in_proj(x, lw["norm_mix_g"], lw["w_a"], lw["w_b"], lw["w_gate"], lw["gate_bias"],
                          lw["gm_norm_g"], tm, z_dtype)
    c0, n0, m0 = state
    m0 = jnp.broadcast_to(m0[:, :, None], (nbatch, ML_HEADS, LANES))
    ws_bd, b_col = _gmlp_mixing(lw["gm_ws"], lw["gm_bs"], L)
    hml, hgm, c1, n1, m1 = _mixer_core(z, zg, zgt, c0, n0, m0, lw["ml_norm_g"],
                                       ws_bd, b_col, nblk=nblk, nb=nb, L=L, valid=valid, h_dtype=z_dtype)
    keep = t if prompt else valid
    if keep == t:
        sga, sgb, ga_blk, gb_blk = z, z, ZB_GA, ZB_GB
    else:
        rows = lambda a: a.reshape(nbatch, t, a.shape[-1])[:, :keep].reshape(nbatch * keep, a.shape[-1])
        hml, hgm, x = rows(hml), rows(hgm), rows(x)
        sga = rows(z[:, ZB_GA * D_MODEL:(ZB_GA + 1) * D_MODEL])
        sgb = rows(z[:, ZB_GB * D_MODEL:(ZB_GB + 1) * D_MODEL])
        ga_blk = gb_blk = 0
    x2, xn, comb = _proj_router(hml, hgm, sga, sgb, ga_blk, gb_blk, x, lw["p_a"], lw["p_b"], lw["w_out"],
                                lw["norm_ffn_g"], lw["w_router"], lw["b_router"], min(x.shape[0], PROJ_TILE))
    y = _moe_final(xn, comb, x2, lw["e_wg"], lw["e_wu"], lw["e_wd"], lw["out_g"], min(x.shape[0], MOE_BLOCK))
    gv = z[:, ZB_GV * D_MODEL:(ZB_GV + 1) * D_MODEL].reshape(nbatch, t, D_MODEL)
    return y.reshape(nbatch, keep, D_MODEL), (c1, n1, m1[:, :, 0]), gv


def kernel(x_prompt, x_sample, state_mlstm_C, state_mlstm_n, state_mlstm_m, norm_mix_g, w_in, ml_b_i, ml_b_f, ml_norm_g, gm_norm_g, gm_ws, gm_bs, p_a, p_b, w_out, norm_ffn_g, rc_w, rc_b, rf_w, rf_b, e_wg, e_wu, e_wd, final_norm_g):
    depth = w_in.shape[0]
    assert depth == 1, "the final norm is fused into the last layer's MoE kernel; only depth 1 is wired up"
    nbp = x_prompt.shape[0]
    nbs, ts, _ = x_sample.shape

    def layer_weights(l):
        w = w_in[l]
        gates = w[:, GATE_OFF:GATE_OFF + 2 * ML_HEADS]
        return dict(
            norm_mix_g=norm_mix_g[l][None, :],
            w_a=w[:, :GATE_OFF].astype(bf16), w_b=w[:, GATE_OFF + 2 * ML_HEADS:].astype(bf16),
            w_gate=jnp.pad(gates, ((0, 0), (0, LANES - 2 * ML_HEADS))).astype(bf16),
            gate_bias=jnp.pad(jnp.concatenate([ml_b_i[l], ml_b_f[l]]), (0, LANES - 2 * ML_HEADS))[None, :],
            ml_norm_g=ml_norm_g[l],
            gm_norm_g=gm_norm_g[l].reshape(1, GM_GROUPS * GM_DG),
            gm_ws=gm_ws[l], gm_bs=gm_bs[l],
            p_a=p_a[l].astype(bf16), p_b=p_b[l].astype(bf16), w_out=w_out[l].astype(bf16),
            norm_ffn_g=norm_ffn_g[l][None, :],
            w_router=jnp.pad(
                jnp.concatenate([rf_w[l].transpose(1, 0, 2).reshape(D_MODEL, MOE_EXPERTS), rc_w[l]], axis=1),
                ((0, 0), (0, LANES - MOE_EXPERTS - MOE_GROUPS))).astype(bf16),
            b_router=jnp.pad(jnp.concatenate([rf_b[l].reshape(-1), rc_b[l]]),
                             (0, LANES - MOE_EXPERTS - MOE_GROUPS))[None, :],
            e_wg=e_wg[l].astype(bf16), e_wu=e_wu[l].astype(bf16), e_wd=e_wd[l].astype(bf16),
            out_g=final_norm_g[None, :],
        )

    lw = layer_weights(0)

    zero_state = (jnp.zeros((nbp, ML_HEADS, ML_DV, ML_DQK), f32), jnp.zeros((nbp, ML_HEADS, ML_DQK), f32),
                  jnp.zeros((nbp, ML_HEADS), f32))
    y_p, (c_p, n_p, m_p), _ = _layer(x_prompt, zero_state, lw, prompt=True)

    xs = jnp.pad(x_sample, ((0, 0), (0, SAMPLE_PAD_T - ts), (0, 0)))
    y_s, (c_s, n_s, m_s), v_s = _layer(xs, (state_mlstm_C[0], state_mlstm_n[0], state_mlstm_m[0]),
                                       dict(lw, valid=ts), prompt=False)
    v_s = v_s[:, :ts].reshape(nbs, ts, GM_GROUPS, GM_DG)
    return (y_p, y_s, c_p[None], n_p[None], m_p[None], c_s[None], n_s[None], m_s[None], v_s[None])
```

```python
import functools

import numpy as np
import jax
import jax.numpy as jnp
from jax import lax
from jax.experimental import pallas as pl
from jax.experimental.pallas import tpu as pltpu

D_MODEL = 1024
ML_HEADS = 4
ML_DQK = 128
ML_DV = 256
ML_CHUNK = 128
GM_GROUPS = 4
GM_DG = 256
MOE_GROUPS = 4
MOE_PER_GROUP = 8
MOE_EXPERTS = MOE_GROUPS * MOE_PER_GROUP
MOE_HIDDEN = 256
EPS = 1e-6

LANES = 128
SAMPLE_PAD_T = 8
ROWS = 128
N_PAD_ROWS = 16
NEG = -1e30
MOE_TILES = (128, 144, 160)
MOE_BLOCK = 1024
PROJ_TILE = 1024
PROJ_SUB = 256
GRP_LANE = MOE_EXPERTS
VMEM_LIMIT = 56 * 1024 * 1024

ZB_QK, ZB_V, ZB_OG, ZB_U, ZB_GV, ZB_GA, ZB_GB = range(7)
N_ZB = 7
N_ZB_A = 3
GATE_OFF = 2 * ML_HEADS * ML_DQK + 2 * ML_HEADS * ML_DV

f32 = jnp.float32
bf16 = jnp.bfloat16


def _sigmoid(x):
    return 0.5 * jnp.tanh(0.5 * x) + 0.5


def _log_sigmoid(x):
    return jnp.minimum(x, 0.0) - jnp.log1p(jnp.exp(-jnp.abs(x)))


def _gelu_tanh(x):
    return 0.5 * x * (1.0 + jnp.tanh(np.sqrt(2.0 / np.pi) * (x + 0.044715 * (x * x * x))))


def _rms(x, g):
    return x * lax.rsqrt(jnp.mean(x * x, axis=-1, keepdims=True) + EPS) * g


def _dot(a, b):
    return jnp.dot(a, b, preferred_element_type=f32)


def _dot_nt(a, b):
    return lax.dot_general(a, b, (((1,), (1,)), ((), ())), preferred_element_type=f32)


def _dot_tn(a, b):
    return lax.dot_general(a, b, (((0,), (0,)), ((), ())), preferred_element_type=f32)


def _in_proj_kernel(x_ref, g_ref, wa_ref, wb_ref, wg_ref, gb_ref, gmg_ref, z_ref, zg_ref, zgt_ref):
    tm = x_ref.shape[0]
    subs = [slice(r, r + tm // 2) for r in (0, tm // 2)]
    xn = [None, None]

    def prepare(s):
        rows = subs[s]
        xn[s] = _rms(x_ref[rows, :], g_ref[...]).astype(bf16)
        zg = _dot(xn[s], wg_ref[...]) + gb_ref[...]
        lane = lax.broadcasted_iota(jnp.int32, zg.shape, 1)
        zg = jnp.where((lane >= ML_HEADS) & (lane < 2 * ML_HEADS), _log_sigmoid(zg), zg)
        zg_ref[rows, :] = zg
        zgt_ref[:, rows] = zg.T[:2 * ML_HEADS, :]

    def gelu_group_rms(a, g):
        return _rms(_gelu_tanh(a), gmg_ref[:, g * GM_DG:(g + 1) * GM_DG])

    act = {ZB_QK: None, ZB_V: None, ZB_OG: _sigmoid, ZB_GA: _sigmoid, ZB_GB: _sigmoid,
           ZB_U: _gelu_tanh, ZB_GV: gelu_group_rms}
    per_blk = D_MODEL // GM_DG

    def emit(blk, s):
        for g in range(per_blk):
            cols = slice(blk * D_MODEL + g * GM_DG, blk * D_MODEL + (g + 1) * GM_DG)
            if blk < N_ZB_A:
                a = _dot(xn[s], wa_ref[:, cols])
            else:
                a = _dot(xn[s], wb_ref[:, cols.start - N_ZB_A * D_MODEL:cols.stop - N_ZB_A * D_MODEL])
            if blk == ZB_GV:
                a = gelu_group_rms(a, g)
            elif act[blk] is not None:
                a = act[blk](a)
            z_ref[subs[s], cols] = a.astype(z_ref.dtype)

    prepare(0)
    emit(0, 0)
    prepare(1)
    emit(0, 1)
    for blk in range(1, N_ZB):
        emit(blk, 0)
        emit(blk, 1)


def _in_proj(x, g, w_a, w_b, w_gate, gate_bias, gm_g, tm, z_dtype):
    n = x.shape[0]
    const = lambda shape: pl.BlockSpec(shape, lambda i: (0, 0), pipeline_mode=pl.Buffered(1))
    return pl.pallas_call(
        _in_proj_kernel,
        grid=(n // tm,),
        in_specs=[
            pl.BlockSpec((tm, D_MODEL), lambda i: (i, 0)),
            const((1, D_MODEL)),
            const((D_MODEL, N_ZB_A * D_MODEL)),
            const((D_MODEL, (N_ZB - N_ZB_A) * D_MODEL)),
            const((D_MODEL, LANES)),
            const((1, LANES)),
            const((1, D_MODEL)),
        ],
        out_specs=[
            pl.BlockSpec((tm, N_ZB * D_MODEL), lambda i: (i, 0)),
            pl.BlockSpec((tm, LANES), lambda i: (i, 0)),
            pl.BlockSpec((2 * ML_HEADS, tm), lambda i: (0, i)),
        ],
        out_shape=[
            jax.ShapeDtypeStruct((n, N_ZB * D_MODEL), z_dtype),
            jax.ShapeDtypeStruct((n, LANES), f32),
            jax.ShapeDtypeStruct((2 * ML_HEADS, n), f32),
        ],
        compiler_params=pltpu.CompilerParams(
            dimension_semantics=("parallel",), vmem_limit_bytes=VMEM_LIMIT),
        name="in_proj",
    )(x, g, w_a, w_b, w_gate, gate_bias, gm_g)


def _mixer_core_kernel(zqk_ref, zv_ref, zog_ref, zu_ref, zgv_ref, zg_ref, zgt_ref,
                       c0_ref, n0_ref, m0_ref, mlg_ref, wsbd_ref, bcol_ref, eye_ref,
                       hml_ref, hgm_ref, c_ref, n_ref, m_ref, *, nblk, nb, L, valid, single_chunk):
    if single_chunk:
        cs_ref, ns_ref, ms_ref = c0_ref, n0_ref, m0_ref
    else:
        cs_ref, ns_ref, ms_ref = c_ref, n_ref, m_ref

        @pl.when(pl.program_id(1) == 0)
        def _():
            c_ref[...] = c0_ref[...]
            n_ref[...] = n0_ref[...]
            m_ref[...] = m0_ref[...]

    shift = L.bit_length() - 1
    pp = lax.broadcasted_iota(jnp.int32, (ROWS, ROWS), 0)
    qq = lax.broadcasted_iota(jnp.int32, (ROWS, ROWS), 1)
    same = (pp >> shift) == (qq >> shift)
    p_col = lax.broadcasted_iota(jnp.int32, (ROWS, 1), 0)
    p_row = lax.broadcasted_iota(jnp.int32, (1, ROWS), 1)
    col_ok = (p_col & (L - 1)) < valid
    row_ok = (p_row & (L - 1)) < valid
    causal = same & (qq <= pp) & row_ok
    upper = same & (pp <= qq)
    scale = ML_DQK ** -0.5

    def per_seq(fn):
        return jnp.concatenate([fn(b) for b in range(nb)], axis=0) if nb > 1 else fn(0)

    stores = []
    chains = [(r, h) for r in range(nblk) for h in range(ML_HEADS)]
    each = lambda fn: [fn(i, r, h) for i, (r, h) in enumerate(chains)]

    q = each(lambda i, r, h: zqk_ref[r, :, h * ML_DQK:(h + 1) * ML_DQK])
    k = each(lambda i, r, h: zqk_ref[r, :, (ML_HEADS + h) * ML_DQK:(ML_HEADS + h + 1) * ML_DQK])
    v = each(lambda i, r, h: zv_ref[r, :, h * ML_DV:(h + 1) * ML_DV])
    qc = [x.astype(bf16) for x in q]
    kc = [x.astype(bf16) for x in k]
    c0 = each(lambda i, r, h: cs_ref[r * nb:(r + 1) * nb, h])
    one_seq = nb == 1
    if one_seq:
        m0 = each(lambda i, r, h: ms_ref[r, h:h + 1, 0:1])
    else:
        m0 = each(lambda i, r, h: per_seq(lambda b: jnp.broadcast_to(ms_ref[r * nb + b, h:h + 1, 0:1], (L, 1))))
    n0 = each(lambda i, r, h: per_seq(lambda b: jnp.broadcast_to(ns_ref[r * nb + b, h:h + 1, :], (L, ML_DQK))))
    logi_col = each(lambda i, r, h: zg_ref[r, :, h:h + 1])
    logi_row = each(lambda i, r, h: zgt_ref[r, h:h + 1, :])
    logf_col = each(lambda i, r, h: jnp.where(col_ok, zg_ref[r, :, ML_HEADS + h:ML_HEADS + h + 1], 0.0))
    logf_row = each(lambda i, r, h: jnp.where(row_ok, zgt_ref[r, ML_HEADS + h:ML_HEADS + h + 1, :], 0.0))

    if one_seq:
        n_rows = each(lambda i, r, h: jnp.broadcast_to(ns_ref[r, h:h + 1, :], (N_PAD_ROWS, ML_DQK)).astype(bf16))
        qkc = each(lambda i, r, h: _dot_nt(
            qc[i], jnp.concatenate([kc[i], c0[i].reshape(ML_DV, ML_DQK).astype(bf16), n_rows[i]], axis=0)))
    else:
        qkc = each(lambda i, r, h: _dot_nt(
            qc[i], jnp.concatenate([kc[i], c0[i].reshape(nb * ML_DV, ML_DQK).astype(bf16)], axis=0)))
    b_col = each(lambda i, r, h: jnp.sum(jnp.where(causal, logf_row[i], 0.0), axis=1, keepdims=True))
    b_row = each(lambda i, r, h: jnp.sum(jnp.where(upper, logf_col[i], 0.0), axis=0, keepdims=True))
    if one_seq:
        b_last_col = each(lambda i, r, h: jnp.sum(logf_row[i], axis=1, keepdims=True))
        b_last_row = b_last_col
    else:
        b_last_col = each(lambda i, r, h: jnp.sum(jnp.where(same, logf_row[i], 0.0), axis=1, keepdims=True))
        b_last_row = each(lambda i, r, h: jnp.sum(jnp.where(same, logf_col[i], 0.0), axis=0, keepdims=True))
    d = each(lambda i, r, h: jnp.where(causal, b_col[i] - b_row[i] + logi_row[i], NEG))
    inter = each(lambda i, r, h: b_col[i] + m0[i])
    m_col = each(lambda i, r, h: jnp.maximum(inter[i], jnp.max(d[i], axis=1, keepdims=True)))
    w_intra = each(lambda i, r, h: jnp.exp(d[i] - m_col[i]) * scale)
    w_inter = each(lambda i, r, h: jnp.exp(inter[i] - m_col[i]) * scale)
    s = each(lambda i, r, h: qkc[i][:, :ROWS] * w_intra[i])
    q_mem = each(lambda i, r, h: per_seq(
        lambda b: qkc[i][b * L:(b + 1) * L, ROWS + b * ML_DV:ROWS + (b + 1) * ML_DV]))
    num = each(lambda i, r, h: _dot(s[i].astype(bf16), v[i].astype(bf16)) + w_inter[i] * q_mem[i])
    if one_seq:
        q_n = each(lambda i, r, h: qkc[i][:, ROWS + ML_DV:ROWS + ML_DV + 1])
    else:
        q_n = each(lambda i, r, h: jnp.sum(q[i].astype(f32) * n0[i], axis=1, keepdims=True))
    den = each(lambda i, r, h: jnp.sum(s[i], axis=1, keepdims=True) + w_inter[i] * q_n[i])
    hh = each(lambda i, r, h: _rms(num[i] / jnp.maximum(jnp.abs(den[i]), jnp.exp(-m_col[i])), mlg_ref[h:h + 1, :]))
    for i, (r, h) in enumerate(chains):
        sog = zog_ref[r, :, h * ML_DV:(h + 1) * ML_DV].astype(f32)
        stores.append((hml_ref, (r, slice(None), slice(h * ML_DV, (h + 1) * ML_DV)), (sog * hh[i]).astype(hml_ref.dtype)))

    wend_col = each(lambda i, r, h: jnp.where(col_ok, b_last_col[i] - b_col[i] + logi_col[i], NEG))
    wend_row = each(lambda i, r, h: jnp.where(row_ok, b_last_row[i] - b_row[i] + logi_row[i], NEG))
    if one_seq:
        m_new = each(lambda i, r, h: jnp.maximum(b_last_col[i] + m0[i], jnp.max(wend_row[i], axis=1, keepdims=True)))
    else:
        m_new = each(lambda i, r, h: jnp.maximum(
            b_last_col[i] + m0[i], jnp.max(jnp.where(same, wend_row[i], NEG), axis=1, keepdims=True)))
    decay = each(lambda i, r, h: jnp.exp(b_last_col[i] + m0[i] - m_new[i]))
    wend = each(lambda i, r, h: jnp.exp(wend_col[i] - m_new[i]))
    vw = each(lambda i, r, h: (v[i].astype(f32) * wend[i]).astype(bf16))
    kw = each(lambda i, r, h: k[i].astype(f32) * wend[i])
    if nb == 1:
        upd = each(lambda i, r, h: _dot_tn(vw[i], kc[i]))
    else:
        vw_t = each(lambda i, r, h: _dot_nt(eye_ref[...], vw[i]).astype(bf16))
        seq_of_lane = lax.broadcasted_iota(jnp.int32, (ML_DV, ROWS), 1) >> shift
        upd = each(lambda i, r, h: _dot(jnp.concatenate(
            [jnp.where(seq_of_lane == b, vw_t[i], jnp.zeros_like(vw_t[i])) for b in range(nb)], axis=0), kc[i]))
    for i, (r, h) in enumerate(chains):
        for b in range(nb):
            slot = r * nb + b
            dec = decay[i][b * L:b * L + 1, :]
            stores.append((c_ref, (slot, h), dec * c0[i][b] + upd[i][b * ML_DV:(b + 1) * ML_DV]))
            stores.append((n_ref, (slot, slice(h, h + 1), slice(None)),
                           dec * ns_ref[slot, h:h + 1, :] + jnp.sum(kw[i][b * L:(b + 1) * L], axis=0, keepdims=True)))
            stores.append((m_ref, (slot, slice(h, h + 1), slice(None)),
                           jnp.broadcast_to(m_new[i][b * L:b * L + 1, :], (1, LANES))))

    for r in range(nblk):
        gv = jnp.concatenate([zgv_ref[r, :, g * GM_DG:(g + 1) * GM_DG].astype(bf16) for g in range(GM_GROUPS)], axis=0)
        mixed = _dot(wsbd_ref[...], gv) + bcol_ref[...]
        for g in range(GM_GROUPS):
            u = zu_ref[r, :, g * GM_DG:(g + 1) * GM_DG].astype(f32)
            stores.append((hgm_ref, (r, slice(None), slice(g * GM_DG, (g + 1) * GM_DG)),
                           (u * mixed[g * ROWS:(g + 1) * ROWS]).astype(hgm_ref.dtype)))

    for ref, idx, val in stores:
        ref[idx] = val


def _mixer_core(z, zg, zgt, c0, n0, m0, ml_g, ws_bd, b_col, *, nblk, nb, L, valid, h_dtype):
    n = z.shape[0]
    nbatch = c0.shape[0]
    groups = nbatch // nb
    nc = n // groups // ROWS
    z4 = z.reshape(groups, nc, ROWS, N_ZB * D_MODEL)
    zg4 = zg.reshape(groups, nc, ROWS, LANES)
    zgt4 = zgt.reshape(2 * ML_HEADS, groups, nc, ROWS).transpose(1, 2, 0, 3)
    zspec = lambda blk: pl.BlockSpec((nblk, None, ROWS, D_MODEL), lambda b, c, blk=blk: (b, c, 0, blk))
    full = lambda shape: pl.BlockSpec(shape, lambda b, c: (0,) * len(shape))
    state_specs = [
        pl.BlockSpec((nblk * nb, ML_HEADS, ML_DV, ML_DQK), lambda b, c: (b, 0, 0, 0)),
        pl.BlockSpec((nblk * nb, ML_HEADS, ML_DQK), lambda b, c: (b, 0, 0)),
        pl.BlockSpec((nblk * nb, ML_HEADS, LANES), lambda b, c: (b, 0, 0)),
    ]
    tok_spec = zspec(0)
    eye = jnp.eye(ML_DV, dtype=bf16)
    hml, hgm, c1, n1, m1 = pl.pallas_call(
        functools.partial(_mixer_core_kernel, nblk=nblk, nb=nb, L=L, valid=valid, single_chunk=(nc == 1)),
        grid=(groups // nblk, nc),
        in_specs=[zspec(ZB_QK), zspec(ZB_V), zspec(ZB_OG), zspec(ZB_U), zspec(ZB_GV),
                  pl.BlockSpec((nblk, None, ROWS, LANES), lambda b, c: (b, c, 0, 0)),
                  pl.BlockSpec((nblk, None, 2 * ML_HEADS, ROWS), lambda b, c: (b, c, 0, 0)),
                  *state_specs,
                  full((ML_HEADS, ML_DV)), full((GM_GROUPS * ROWS, GM_GROUPS * ROWS)), full((GM_GROUPS * ROWS, 1)),
                  full((ML_DV, ML_DV))],
        out_specs=[tok_spec, tok_spec] + state_specs,
        out_shape=[
            jax.ShapeDtypeStruct((groups, nc, ROWS, D_MODEL), h_dtype),
            jax.ShapeDtypeStruct((groups, nc, ROWS, D_MODEL), h_dtype),
            jax.ShapeDtypeStruct((nbatch, ML_HEADS, ML_DV, ML_DQK), f32),
            jax.ShapeDtypeStruct((nbatch, ML_HEADS, ML_DQK), f32),
            jax.ShapeDtypeStruct((nbatch, ML_HEADS, LANES), f32),
        ],
        compiler_params=pltpu.CompilerParams(
            dimension_semantics=("parallel", "arbitrary"), vmem_limit_bytes=VMEM_LIMIT),
        name="mixer_core",
    )(z4, z4, z4, z4, z4, zg4, zgt4, c0, n0, m0, ml_g, ws_bd, b_col, eye)
    return hml.reshape(n, D_MODEL), hgm.reshape(n, D_MODEL), c1, n1, m1


def _gmlp_mixing(gm_ws, gm_bs, L):
    reps = ROWS // L
    tril = jnp.tril(jnp.ones((L, L), bool))
    blocks = [jnp.kron(jnp.eye(reps, dtype=f32), jnp.where(tril, gm_ws[g, :L, :L], 0.0)) for g in range(GM_GROUPS)]
    ws_bd = jax.scipy.linalg.block_diag(*blocks).astype(bf16)
    b_col = jnp.concatenate([jnp.tile(gm_bs[g, :L], reps) for g in range(GM_GROUPS)])[:, None]
    return ws_bd, b_col


def _proj_router_kernel(hml_ref, hgm_ref, sga_ref, sgb_ref, x_ref, pa_ref, pb_ref, wo_ref, g_ref, wr_ref, br_ref,
                        x2_ref, xn_ref, comb_ref):
    tm = x_ref.shape[0]
    subs = [slice(r, r + PROJ_SUB) for r in range(0, tm, PROJ_SUB)]
    ab, xn = {}, {}

    def stage_branches(i):
        r = subs[i]
        ab[i] = (_dot(hml_ref[r, :].astype(bf16), pa_ref[...]), _dot(hgm_ref[r, :].astype(bf16), pb_ref[...]))

    def stage_merge(i):
        r = subs[i]
        a, b = ab.pop(i)
        merged = sga_ref[r, :].astype(f32) * a + sgb_ref[r, :].astype(f32) * b
        x2 = x_ref[r, :] + _dot(merged.astype(bf16), wo_ref[...])
        x2_ref[r, :] = x2
        xn[i] = _rms(x2, g_ref[...]).astype(bf16)
        xn_ref[r, :] = xn[i]

    def stage_route(i):
        lg = _dot(xn.pop(i), wr_ref[...]) + br_ref[...]
        lane = lax.broadcasted_iota(jnp.int32, lg.shape, 1).astype(f32)
        cmask = (lane >= MOE_EXPERTS) & (lane < MOE_EXPERTS + MOE_GROUPS)
        cl = jnp.where(cmask, lg, NEG)
        cmax = jnp.max(cl, axis=1, keepdims=True)
        p_grp = 1.0 / jnp.sum(jnp.where(cmask, jnp.exp(cl - cmax), 0.0), axis=1, keepdims=True)
        grp = jnp.min(jnp.where(cl == cmax, lane, 2.0 * LANES), axis=1, keepdims=True) - MOE_EXPERTS
        fmask = (lane >= grp * MOE_PER_GROUP) & (lane < (grp + 1.0) * MOE_PER_GROUP)
        fl = jnp.where(fmask, lg, NEG)
        v1 = jnp.max(fl, axis=1, keepdims=True)
        i1 = jnp.min(jnp.where(fl == v1, lane, 2.0 * LANES), axis=1, keepdims=True)
        fl2 = jnp.where(lane == i1, NEG, fl)
        v2 = jnp.max(fl2, axis=1, keepdims=True)
        i2 = jnp.min(jnp.where(fl2 == v2, lane, 2.0 * LANES), axis=1, keepdims=True)
        e2 = jnp.exp(v2 - v1)
        g1 = p_grp / (1.0 + e2)
        g2 = p_grp * e2 / (1.0 + e2)
        comb_ref[subs[i], :] = (jnp.where(lane == i1, g1, 0.0) + jnp.where(lane == i2, g2, 0.0)
                                + jnp.where(lane == GRP_LANE, grp, 0.0))

    n = len(subs)
    for step in range(n + 2):
        if step < n:
            stage_branches(step)
        if 0 <= step - 1 < n:
            stage_merge(step - 1)
        if 0 <= step - 2 < n:
            stage_route(step - 2)


def _proj_router(hml, hgm, sga, sgb, ga_blk, gb_blk, x, pa, pb, wo, g, wr, br, tm):
    n = x.shape[0]
    row = lambda blk=0: pl.BlockSpec((tm, D_MODEL), lambda i, blk=blk: (i, blk))
    const = lambda shape: pl.BlockSpec(shape, lambda i: (0, 0), pipeline_mode=pl.Buffered(1))
    wfull = const((D_MODEL, D_MODEL))
    return pl.pallas_call(
        _proj_router_kernel,
        grid=(n // tm,),
        in_specs=[row(), row(), row(ga_blk), row(gb_blk), row(), wfull, wfull, wfull,
                  const((1, D_MODEL)), const((D_MODEL, LANES)), const((1, LANES))],
        out_specs=[row(), row(), pl.BlockSpec((tm, LANES), lambda i: (i, 0))],
        out_shape=[jax.ShapeDtypeStruct((n, D_MODEL), f32),
                   jax.ShapeDtypeStruct((n, D_MODEL), bf16),
                   jax.ShapeDtypeStruct((n, LANES), f32)],
        compiler_params=pltpu.CompilerParams(
            dimension_semantics=("parallel",), vmem_limit_bytes=VMEM_LIMIT),
        name="proj_router",
    )(hml, hgm, sga, sgb, x, pa, pb, wo, g, wr, br)


def _moe_final_kernel(xn_ref, comb_ref, x2_ref, ltri_ref, wg_ref, wu_ref, wd_ref, g_ref, y_ref,
                      chl_ref, gp_col_ref, gp_row_ref, pos_col_ref, pos_row_ref):
    grp = pl.program_id(1)
    gf = grp.astype(f32)
    tb = xn_ref.shape[0]

    @pl.when(grp == 0)
    def _():
        y_ref[...] = jnp.zeros_like(y_ref)
        comb = comb_ref[...]
        lane = lax.broadcasted_iota(jnp.int32, comb.shape, 1)
        gcol = comb[:, GRP_LANE:GRP_LANE + 1]
        onehot = jnp.where(lane.astype(f32) == gcol, 1.0, 0.0)
        before = _dot(ltri_ref[...], onehot.astype(bf16))
        pos = jnp.sum(onehot * before, axis=1, keepdims=True)
        gp = jnp.where(lane == 0, gcol, 0.0) + jnp.where(lane == 1, pos, 0.0)
        gp_col_ref[...] = gp
        gp_row_ref[...] = gp.T[:8, :]
        chi = comb.astype(bf16)
        chl_ref[:, :LANES] = chi
        chl_ref[:, LANES:] = (comb - chi.astype(f32)).astype(bf16)

    in_grp_col = gp_col_ref[:, 0:1] == gf
    pos_col_ref[...] = jnp.broadcast_to(jnp.where(in_grp_col, gp_col_ref[:, 1:2], -1.0), pos_col_ref.shape)
    pos_row_ref[...] = jnp.broadcast_to(jnp.where(gp_row_ref[0:1, :] == gf, gp_row_ref[1:2, :], -1.0), (8, tb))
    cnt = jnp.sum(jnp.where(in_grp_col, 1.0, 0.0)).astype(jnp.int32)

    def run_tiles(rows):
        def tile(t, carry):
            base = (t * rows).astype(f32)
            r_iota = lax.broadcasted_iota(jnp.int32, (rows, tb), 0).astype(f32)
            gather = jnp.where(pos_row_ref[0:1, :] - base == r_iota, 1.0, 0.0).astype(bf16)
            x = _dot(gather, xn_ref[...]).astype(bf16)
            c2 = _dot(gather, chl_ref[...])
            c = c2[:, :LANES] + c2[:, LANES:]
            lane = lax.broadcasted_iota(jnp.int32, c.shape, 1)
            hid = []
            for e in range(MOE_PER_GROUP):
                ce = jnp.sum(jnp.where(lane == grp * MOE_PER_GROUP + e, c, 0.0), axis=1, keepdims=True)
                a = _dot(x, wg_ref[e])
                u = _dot(x, wu_ref[e])
                hid.append((a * _sigmoid(a) * u * ce).astype(bf16))
            y = _dot(jnp.concatenate(hid, axis=1), wd_ref[...]).astype(bf16)
            l_iota = lax.broadcasted_iota(jnp.int32, (tb, rows), 1).astype(f32)
            scatter = jnp.where(pos_col_ref[:, :rows] - base == l_iota, 1.0, 0.0).astype(bf16)
            y_ref[...] += _dot(scatter, y)
            return carry

        lax.fori_loop(0, (cnt + rows - 1) // rows, tile, 0)

    n_tiles = [(cnt + rows - 1) // rows for rows in MOE_TILES]
    for i, rows in enumerate(MOE_TILES):
        pick = n_tiles[i] == n_tiles[-1]
        for j in range(i):
            pick = pick & (n_tiles[j] != n_tiles[-1])
        pl.when(pick)(functools.partial(run_tiles, rows))

    @pl.when(grp == MOE_GROUPS - 1)
    def _():
        y_ref[...] = _rms(x2_ref[...] + y_ref[...], g_ref[...])


def _moe_final(xn, comb, x2, wg, wu, wd, g, tb):
    n = x2.shape[0]
    ltri = jnp.tril(jnp.ones((tb, tb), bf16), -1)
    return pl.pallas_call(
        _moe_final_kernel,
        grid=(n // tb, MOE_GROUPS),
        in_specs=[pl.BlockSpec((tb, D_MODEL), lambda i, j: (i, 0)),
                  pl.BlockSpec((tb, LANES), lambda i, j: (i, 0)),
                  pl.BlockSpec((tb, D_MODEL), lambda i, j: (i, 0)),
                  pl.BlockSpec((tb, tb), lambda i, j: (0, 0)),
                  pl.BlockSpec((MOE_PER_GROUP, D_MODEL, MOE_HIDDEN), lambda i, j: (j, 0, 0)),
                  pl.BlockSpec((MOE_PER_GROUP, D_MODEL, MOE_HIDDEN), lambda i, j: (j, 0, 0)),
                  pl.BlockSpec((None, MOE_PER_GROUP * MOE_HIDDEN, D_MODEL), lambda i, j: (j, 0, 0)),
                  pl.BlockSpec((1, D_MODEL), lambda i, j: (0, 0))],
        out_specs=pl.BlockSpec((tb, D_MODEL), lambda i, j: (i, 0)),
        out_shape=jax.ShapeDtypeStruct((n, D_MODEL), f32),
        scratch_shapes=[pltpu.VMEM((tb, 2 * LANES), bf16),
                        pltpu.VMEM((tb, LANES), f32), pltpu.VMEM((8, tb), f32),
                        pltpu.VMEM((tb, 2 * LANES), f32), pltpu.VMEM((8, tb), f32)],
        compiler_params=pltpu.CompilerParams(
            dimension_semantics=("parallel", "arbitrary"), vmem_limit_bytes=VMEM_LIMIT),
        name="moe_final",
    )(xn, comb, x2, ltri, wg, wu, wd.reshape(MOE_GROUPS, MOE_PER_GROUP * MOE_HIDDEN, D_MODEL), g)


def _layer(x3, state, lw, *, prompt):
    nbatch, t, _ = x3.shape
    n = nbatch * t
    x = x3.reshape(n, D_MODEL)
    if prompt:
        L, valid, z_dtype, tm, nblk = ML_CHUNK, ML_CHUNK, bf16, 512, 4
    else:
        L, valid, z_dtype, tm, nblk = t, lw["valid"], f32, 256, 1
    nb = ROWS // L

    z, zg, zgt = _in_proj(x, lw["norm_mix_g"], lw["w_a"], lw["w_b"], lw["w_gate"], lw["gate_bias"],
                          lw["gm_norm_g"], tm, z_dtype)
    c0, n0, m0 = state
    m0 = jnp.broadcast_to(m0[:, :, None], (nbatch, ML_HEADS, LANES))
    ws_bd, b_col = _gmlp_mixing(lw["gm_ws"], lw["gm_bs"], L)
    hml, hgm, c1, n1, m1 = _mixer_core(z, zg, zgt, c0, n0, m0, lw["ml_norm_g"],
                                       ws_bd, b_col, nblk=nblk, nb=nb, L=L, valid=valid, h_dtype=z_dtype)
    keep = t if prompt else valid
    if keep == t:
        sga, sgb, ga_blk, gb_blk = z, z, ZB_GA, ZB_GB
    else:
        rows = lambda a: a.reshape(nbatch, t, a.shape[-1])[:, :keep].reshape(nbatch * keep, a.shape[-1])
        hml, hgm, x = rows(hml), rows(hgm), rows(x)
        sga = rows(z[:, ZB_GA * D_MODEL:(ZB_GA + 1) * D_MODEL])
        sgb = rows(z[:, ZB_GB * D_MODEL:(ZB_GB + 1) * D_MODEL])
        ga_blk = gb_blk = 0
    x2, xn, comb = _proj_router(hml, hgm, sga, sgb, ga_blk, gb_blk, x, lw["p_a"], lw["p_b"], lw["w_out"],
                                lw["norm_ffn_g"], lw["w_router"], lw["b_router"], min(x.shape[0], PROJ_TILE))
    y = _moe_final(xn, comb, x2, lw["e_wg"], lw["e_wu"], lw["e_wd"], lw["out_g"], min(x.shape[0], MOE_BLOCK))
    gv = z[:, ZB_GV * D_MODEL:(ZB_GV + 1) * D_MODEL].reshape(nbatch, t, D_MODEL)
    return y.reshape(nbatch, keep, D_MODEL), (c1, n1, m1[:, :, 0]), gv


def kernel(x_prompt, x_sample, state_mlstm_C, state_mlstm_n, state_mlstm_m, norm_mix_g, w_in, ml_b_i, ml_b_f, ml_norm_g, gm_norm_g, gm_ws, gm_bs, p_a, p_b, w_out, norm_ffn_g, rc_w, rc_b, rf_w, rf_b, e_wg, e_wu, e_wd, final_norm_g):
    depth = w_in.shape[0]
    assert depth == 1, "the final norm is fused into the last layer's MoE kernel; only depth 1 is wired up"
    nbp = x_prompt.shape[0]
    nbs, ts, _ = x_sample.shape

    def layer_weights(l):
        w = w_in[l]
        gates = w[:, GATE_OFF:GATE_OFF + 2 * ML_HEADS]
        return dict(
            norm_mix_g=norm_mix_g[l][None, :],
            w_a=w[:, :GATE_OFF].astype(bf16), w_b=w[:, GATE_OFF + 2 * ML_HEADS:].astype(bf16),
            w_gate=jnp.pad(gates, ((0, 0), (0, LANES - 2 * ML_HEADS))).astype(bf16),
            gate_bias=jnp.pad(jnp.concatenate([ml_b_i[l], ml_b_f[l]]), (0, LANES - 2 * ML_HEADS))[None, :],
            ml_norm_g=ml_norm_g[l],
            gm_norm_g=gm_norm_g[l].reshape(1, GM_GROUPS * GM_DG),
            gm_ws=gm_ws[l], gm_bs=gm_bs[l],
            p_a=p_a[l].astype(bf16), p_b=p_b[l].astype(bf16), w_out=w_out[l].astype(bf16),
            norm_ffn_g=norm_ffn_g[l][None, :],
            w_router=jnp.pad(
                jnp.concatenate([rf_w[l].transpose(1, 0, 2).reshape(D_MODEL, MOE_EXPERTS), rc_w[l]], axis=1),
                ((0, 0), (0, LANES - MOE_EXPERTS - MOE_GROUPS))).astype(bf16),
            b_router=jnp.pad(jnp.concatenate([rf_b[l].reshape(-1), rc_b[l]]),
                             (0, LANES - MOE_EXPERTS - MOE_GROUPS))[None, :],
            e_wg=e_wg[l].astype(bf16), e_wu=e_wu[l].astype(bf16), e_wd=e_wd[l].astype(bf16),
            out_g=final_norm_g[None, :],
        )

    lw = layer_weights(0)

    zero_state = (jnp.zeros((nbp, ML_HEADS, ML_DV, ML_DQK), f32), jnp.zeros((nbp, ML_HEADS, ML_DQK), f32),
                  jnp.zeros((nbp, ML_HEADS), f32))
    y_p, (c_p, n_p, m_p), _ = _layer(x_prompt, zero_state, lw, prompt=True)

    xs = jnp.pad(x_sample, ((0, 0), (0, SAMPLE_PAD_T - ts), (0, 0)))
    y_s, (c_s, n_s, m_s), v_s = _layer(xs, (state_mlstm_C[0], state_mlstm_n[0], state_mlstm_m[0]),
                                       dict(lw, valid=ts), prompt=False)
    v_s = v_s[:, :ts].reshape(nbs, ts, GM_GROUPS, GM_DG)
    return (y_p, y_s, c_p[None], n_p[None], m_p[None], c_s[None], n_s[None], m_s[None], v_s[None])
```

```python
import functools

import numpy as np
import jax
import jax.numpy as jnp
from jax import lax
from jax.experimental import pallas as pl
from jax.experimental.pallas import tpu as pltpu

D_MODEL = 1024
ML_HEADS = 4
ML_DQK = 128
ML_DV = 256
ML_CHUNK = 128
GM_GROUPS = 4
GM_DG = 256
MOE_GROUPS = 4
MOE_PER_GROUP = 8
MOE_EXPERTS = MOE_GROUPS * MOE_PER_GROUP
MOE_HIDDEN = 256
EPS = 1e-6

LANES = 128
SAMPLE_PAD_T = 8
ROWS = 128
N_PAD_ROWS = 16
NEG = -1e30
MOE_TILES = (128, 144, 160)
MOE_BLOCK = 1024
PROJ_TILE = 1024
PROJ_SUB = 256
GRP_LANE = MOE_EXPERTS
VMEM_LIMIT = 56 * 1024 * 1024

ZB_QK, ZB_V, ZB_OG, ZB_U, ZB_GV, ZB_GA, ZB_GB = range(7)
N_ZB = 7
N_ZB_A = 3
GATE_OFF = 2 * ML_HEADS * ML_DQK + 2 * ML_HEADS * ML_DV

f32 = jnp.float32
bf16 = jnp.bfloat16


def _sigmoid(x):
    return 0.5 * jnp.tanh(0.5 * x) + 0.5


def _log_sigmoid(x):
    return jnp.minimum(x, 0.0) - jnp.log1p(jnp.exp(-jnp.abs(x)))


def _gelu_tanh(x):
    return 0.5 * x * (1.0 + jnp.tanh(np.sqrt(2.0 / np.pi) * (x + 0.044715 * (x * x * x))))


def _rms(x, g):
    return x * lax.rsqrt(jnp.mean(x * x, axis=-1, keepdims=True) + EPS) * g


def _dot(a, b):
    return jnp.dot(a, b, preferred_element_type=f32)


def _dot_nt(a, b):
    return lax.dot_general(a, b, (((1,), (1,)), ((), ())), preferred_element_type=f32)


def _dot_tn(a, b):
    return lax.dot_general(a, b, (((0,), (0,)), ((), ())), preferred_element_type=f32)


def _in_proj_kernel(x_ref, g_ref, wa_ref, wb_ref, wg_ref, gb_ref, gmg_ref, z_ref, zg_ref, zgt_ref):
    tm = x_ref.shape[0]
    subs = [slice(r, r + tm // 2) for r in (0, tm // 2)]
    xn = [None, None]

    def prepare(s):
        rows = subs[s]
        xn[s] = _rms(x_ref[rows, :], g_ref[...]).astype(bf16)
        zg = _dot(xn[s], wg_ref[...]) + gb_ref[...]
        lane = lax.broadcasted_iota(jnp.int32, zg.shape, 1)
        zg = jnp.where((lane >= ML_HEADS) & (lane < 2 * ML_HEADS), _log_sigmoid(zg), zg)
        zg_ref[rows, :] = zg
        zgt_ref[:, rows] = zg.T[:2 * ML_HEADS, :]

    def gelu_group_rms(a, g):
        return _rms(_gelu_tanh(a), gmg_ref[:, g * GM_DG:(g + 1) * GM_DG])

    act = {ZB_QK: None, ZB_V: None, ZB_OG: _sigmoid, ZB_GA: _sigmoid, ZB_GB: _sigmoid,
           ZB_U: _gelu_tanh, ZB_GV: gelu_group_rms}
    per_blk = D_MODEL // GM_DG

    def emit(blk, s):
        for g in range(per_blk):
            cols = slice(blk * D_MODEL + g * GM_DG, blk * D_MODEL + (g + 1) * GM_DG)
            if blk < N_ZB_A:
                a = _dot(xn[s], wa_ref[:, cols])
            else:
                a = _dot(xn[s], wb_ref[:, cols.start - N_ZB_A * D_MODEL:cols.stop - N_ZB_A * D_MODEL])
            if blk == ZB_GV:
                a = gelu_group_rms(a, g)
            elif act[blk] is not None:
                a = act[blk](a)
            z_ref[subs[s], cols] = a.astype(z_ref.dtype)

    prepare(0)
    emit(0, 0)
    prepare(1)
    emit(0, 1)
    for blk in range(1, N_ZB):
        emit(blk, 0)
        emit(blk, 1)


def _in_proj(x, g, w_a, w_b, w_gate, gate_bias, gm_g, tm, z_dtype):
    n = x.shape[0]
    const = lambda shape: pl.BlockSpec(shape, lambda i: (0, 0), pipeline_mode=pl.Buffered(1))
    return pl.pallas_call(
        _in_proj_kernel,
        grid=(n // tm,),
        in_specs=[
            pl.BlockSpec((tm, D_MODEL), lambda i: (i, 0)),
            const((1, D_MODEL)),
            const((D_MODEL, N_ZB_A * D_MODEL)),
            const((D_MODEL, (N_ZB - N_ZB_A) * D_MODEL)),
            const((D_MODEL, LANES)),
            const((1, LANES)),
            const((1, D_MODEL)),
        ],
        out_specs=[
            pl.BlockSpec((tm, N_ZB * D_MODEL), lambda i: (i, 0)),
            pl.BlockSpec((tm, LANES), lambda i: (i, 0)),
            pl.BlockSpec((2 * ML_HEADS, tm), lambda i: (0, i)),
        ],
        out_shape=[
            jax.ShapeDtypeStruct((n, N_ZB * D_MODEL), z_dtype),
            jax.ShapeDtypeStruct((n, LANES), f32),
            jax.ShapeDtypeStruct((2 * ML_HEADS, n), f32),
        ],
        compiler_params=pltpu.CompilerParams(
            dimension_semantics=("parallel",), vmem_limit_bytes=VMEM_LIMIT),
        name="in_proj",
    )(x, g, w_a, w_b, w_gate, gate_bias, gm_g)


def _mixer_core_kernel(zqk_ref, zv_ref, zog_ref, zu_ref, zgv_ref, zg_ref, zgt_ref,
                       c0_ref, n0_ref, m0_ref, mlg_ref, wsbd_ref, bcol_ref, eye_ref,
                       hml_ref, hgm_ref, c_ref, n_ref, m_ref, *, nblk, nb, L, valid, single_chunk):
    if single_chunk:
        cs_ref, ns_ref, ms_ref = c0_ref, n0_ref, m0_ref
    else:
        cs_ref, ns_ref, ms_ref = c_ref, n_ref, m_ref

        @pl.when(pl.program_id(1) == 0)
        def _():
            c_ref[...] = c0_ref[...]
            n_ref[...] = n0_ref[...]
            m_ref[...] = m0_ref[...]

    shift = L.bit_length() - 1
    pp = lax.broadcasted_iota(jnp.int32, (ROWS, ROWS), 0)
    qq = lax.broadcasted_iota(jnp.int32, (ROWS, ROWS), 1)
    same = (pp >> shift) == (qq >> shift)
    p_col = lax.broadcasted_iota(jnp.int32, (ROWS, 1), 0)
    p_row = lax.broadcasted_iota(jnp.int32, (1, ROWS), 1)
    col_ok = (p_col & (L - 1)) < valid
    row_ok = (p_row & (L - 1)) < valid
    causal = same & (qq <= pp) & row_ok
    upper = same & (pp <= qq)
    scale = ML_DQK ** -0.5

    def per_seq(fn):
        return jnp.concatenate([fn(b) for b in range(nb)], axis=0) if nb > 1 else fn(0)

    stores = []
    chains = [(r, h) for r in range(nblk) for h in range(ML_HEADS)]
    each = lambda fn: [fn(i, r, h) for i, (r, h) in enumerate(chains)]

    q = each(lambda i, r, h: zqk_ref[r, :, h * ML_DQK:(h + 1) * ML_DQK])
    k = each(lambda i, r, h: zqk_ref[r, :, (ML_HEADS + h) * ML_DQK:(ML_HEADS + h + 1) * ML_DQK])
    v = each(lambda i, r, h: zv_ref[r, :, h * ML_DV:(h + 1) * ML_DV])
    qc = [x.astype(bf16) for x in q]
    kc = [x.astype(bf16) for x in k]
    c0 = each(lambda i, r, h: cs_ref[r * nb:(r + 1) * nb, h])
    one_seq = nb == 1
    if one_seq:
        m0 = each(lambda i, r, h: ms_ref[r, h:h + 1, 0:1])
    else:
        m0 = each(lambda i, r, h: per_seq(lambda b: jnp.broadcast_to(ms_ref[r * nb + b, h:h + 1, 0:1], (L, 1))))
    n0 = each(lambda i, r, h: per_seq(lambda b: jnp.broadcast_to(ns_ref[r * nb + b, h:h + 1, :], (L, ML_DQK))))
    logi_col = each(lambda i, r, h: zg_ref[r, :, h:h + 1])
    logi_row = each(lambda i, r, h: zgt_ref[r, h:h + 1, :])
    logf_col = each(lambda i, r, h: jnp.where(col_ok, zg_ref[r, :, ML_HEADS + h:ML_HEADS + h + 1], 0.0))
    logf_row = each(lambda i, r, h: jnp.where(row_ok, zgt_ref[r, ML_HEADS + h:ML_HEADS + h + 1, :], 0.0))

    if one_seq:
        n_rows = each(lambda i, r, h: jnp.broadcast_to(ns_ref[r, h:h + 1, :], (N_PAD_ROWS, ML_DQK)).astype(bf16))
        qkc = each(lambda i, r, h: _dot_nt(
            qc[i], jnp.concatenate([kc[i], c0[i].reshape(ML_DV, ML_DQK).astype(bf16), n_rows[i]], axis=0)))
    else:
        qkc = each(lambda i, r, h: _dot_nt(
            qc[i], jnp.concatenate([kc[i], c0[i].reshape(nb * ML_DV, ML_DQK).astype(bf16)], axis=0)))
    b_col = each(lambda i, r, h: jnp.sum(jnp.where(causal, logf_row[i], 0.0), axis=1, keepdims=True))
    b_row = each(lambda i, r, h: jnp.sum(jnp.where(upper, logf_col[i], 0.0), axis=0, keepdims=True))
    if one_seq:
        b_last_col = each(lambda i, r, h: jnp.sum(logf_row[i], axis=1, keepdims=True))
        b_last_row = b_last_col
    else:
        b_last_col = each(lambda i, r, h: jnp.sum(jnp.where(same, logf_row[i], 0.0), axis=1, keepdims=True))
        b_last_row = each(lambda i, r, h: jnp.sum(jnp.where(same, logf_col[i], 0.0), axis=0, keepdims=True))
    d = each(lambda i, r, h: jnp.where(causal, b_col[i] - b_row[i] + logi_row[i], NEG))
    inter = each(lambda i, r, h: b_col[i] + m0[i])
    m_col = each(lambda i, r, h: jnp.maximum(inter[i], jnp.max(d[i], axis=1, keepdims=True)))
    w_intra = each(lambda i, r, h: jnp.exp(d[i] - m_col[i]) * scale)
    w_inter = each(lambda i, r, h: jnp.exp(inter[i] - m_col[i]) * scale)
    s = each(lambda i, r, h: qkc[i][:, :ROWS] * w_intra[i])
    q_mem = each(lambda i, r, h: per_seq(
        lambda b: qkc[i][b * L:(b + 1) * L, ROWS + b * ML_DV:ROWS + (b + 1) * ML_DV]))
    if one_seq:
        ones_blk = jnp.ones((ROWS, LANES), bf16)
        sv = each(lambda i, r, h: _dot(s[i].astype(bf16), jnp.concatenate([v[i].astype(bf16), ones_blk], axis=1)))
        num = each(lambda i, r, h: sv[i][:, :ML_DV] + w_inter[i] * q_mem[i])
        s_sum = each(lambda i, r, h: sv[i][:, ML_DV:ML_DV + 1])
        q_n = each(lambda i, r, h: qkc[i][:, ROWS + ML_DV:ROWS + ML_DV + 1])
    else:
        num = each(lambda i, r, h: _dot(s[i].astype(bf16), v[i].astype(bf16)) + w_inter[i] * q_mem[i])
        s_sum = each(lambda i, r, h: jnp.sum(s[i], axis=1, keepdims=True))
        q_n = each(lambda i, r, h: jnp.sum(q[i].astype(f32) * n0[i], axis=1, keepdims=True))
    den = each(lambda i, r, h: s_sum[i] + w_inter[i] * q_n[i])
    ones = jnp.ones((ML_DV, LANES), bf16)

    def head_norm(x, g):
        if not one_seq:
            return _rms(x, g)
        ssq = _dot((x * x).astype(bf16), ones)
        scale = lax.rsqrt(ssq * (1.0 / ML_DV) + EPS)
        return x * jnp.concatenate([scale] * (ML_DV // LANES), axis=1) * g

    hh = each(lambda i, r, h: head_norm(num[i] / jnp.maximum(jnp.abs(den[i]), jnp.exp(-m_col[i])), mlg_ref[h:h + 1, :]))
    for i, (r, h) in enumerate(chains):
        sog = zog_ref[r, :, h * ML_DV:(h + 1) * ML_DV].astype(f32)
        stores.append((hml_ref, (r, slice(None), slice(h * ML_DV, (h + 1) * ML_DV)), (sog * hh[i]).astype(hml_ref.dtype)))

    wend_col = each(lambda i, r, h: jnp.where(col_ok, b_last_col[i] - b_col[i] + logi_col[i], NEG))
    wend_row = each(lambda i, r, h: jnp.where(row_ok, b_last_row[i] - b_row[i] + logi_row[i], NEG))
    if one_seq:
        m_new = each(lambda i, r, h: jnp.maximum(b_last_col[i] + m0[i], jnp.max(wend_row[i], axis=1, keepdims=True)))
    else:
        m_new = each(lambda i, r, h: jnp.maximum(
            b_last_col[i] + m0[i], jnp.max(jnp.where(same, wend_row[i], NEG), axis=1, keepdims=True)))
    decay = each(lambda i, r, h: jnp.exp(b_last_col[i] + m0[i] - m_new[i]))
    wend = each(lambda i, r, h: jnp.exp(wend_col[i] - m_new[i]))
    vw = each(lambda i, r, h: (v[i].astype(f32) * wend[i]).astype(bf16))
    kw = each(lambda i, r, h: k[i].astype(f32) * wend[i])
    if nb == 1:
        upd = each(lambda i, r, h: _dot_tn(vw[i], kc[i]))
    else:
        vw_t = each(lambda i, r, h: _dot_nt(eye_ref[...], vw[i]).astype(bf16))
        seq_of_lane = lax.broadcasted_iota(jnp.int32, (ML_DV, ROWS), 1) >> shift
        upd = each(lambda i, r, h: _dot(jnp.concatenate(
            [jnp.where(seq_of_lane == b, vw_t[i], jnp.zeros_like(vw_t[i])) for b in range(nb)], axis=0), kc[i]))
    for i, (r, h) in enumerate(chains):
        for b in range(nb):
            slot = r * nb + b
            dec = decay[i][b * L:b * L + 1, :]
            stores.append((c_ref, (slot, h), dec * c0[i][b] + upd[i][b * ML_DV:(b + 1) * ML_DV]))
            stores.append((n_ref, (slot, slice(h, h + 1), slice(None)),
                           dec * ns_ref[slot, h:h + 1, :] + jnp.sum(kw[i][b * L:(b + 1) * L], axis=0, keepdims=True)))
            stores.append((m_ref, (slot, slice(h, h + 1), slice(None)),
                           jnp.broadcast_to(m_new[i][b * L:b * L + 1, :], (1, LANES))))

    for r in range(nblk):
        gv = jnp.concatenate([zgv_ref[r, :, g * GM_DG:(g + 1) * GM_DG].astype(bf16) for g in range(GM_GROUPS)], axis=0)
        mixed = _dot(wsbd_ref[...], gv) + bcol_ref[...]
        for g in range(GM_GROUPS):
            u = zu_ref[r, :, g * GM_DG:(g + 1) * GM_DG].astype(f32)
            stores.append((hgm_ref, (r, slice(None), slice(g * GM_DG, (g + 1) * GM_DG)),
                           (u * mixed[g * ROWS:(g + 1) * ROWS]).astype(hgm_ref.dtype)))

    for ref, idx, val in stores:
        ref[idx] = val


def _mixer_core(z, zg, zgt, c0, n0, m0, ml_g, ws_bd, b_col, *, nblk, nb, L, valid, h_dtype):
    n = z.shape[0]
    nbatch = c0.shape[0]
    groups = nbatch // nb
    nc = n // groups // ROWS
    z4 = z.reshape(groups, nc, ROWS, N_ZB * D_MODEL)
    zg4 = zg.reshape(groups, nc, ROWS, LANES)
    zgt4 = zgt.reshape(2 * ML_HEADS, groups, nc, ROWS).transpose(1, 2, 0, 3)
    zspec = lambda blk: pl.BlockSpec((nblk, None, ROWS, D_MODEL), lambda b, c, blk=blk: (b, c, 0, blk))
    full = lambda shape: pl.BlockSpec(shape, lambda b, c: (0,) * len(shape))
    state_specs = [
        pl.BlockSpec((nblk * nb, ML_HEADS, ML_DV, ML_DQK), lambda b, c: (b, 0, 0, 0)),
        pl.BlockSpec((nblk * nb, ML_HEADS, ML_DQK), lambda b, c: (b, 0, 0)),
        pl.BlockSpec((nblk * nb, ML_HEADS, LANES), lambda b, c: (b, 0, 0)),
    ]
    tok_spec = zspec(0)
    eye = jnp.eye(ML_DV, dtype=bf16)
    hml, hgm, c1, n1, m1 = pl.pallas_call(
        functools.partial(_mixer_core_kernel, nblk=nblk, nb=nb, L=L, valid=valid, single_chunk=(nc == 1)),
        grid=(groups // nblk, nc),
        in_specs=[zspec(ZB_QK), zspec(ZB_V), zspec(ZB_OG), zspec(ZB_U), zspec(ZB_GV),
                  pl.BlockSpec((nblk, None, ROWS, LANES), lambda b, c: (b, c, 0, 0)),
                  pl.BlockSpec((nblk, None, 2 * ML_HEADS, ROWS), lambda b, c: (b, c, 0, 0)),
                  *state_specs,
                  full((ML_HEADS, ML_DV)), full((GM_GROUPS * ROWS, GM_GROUPS * ROWS)), full((GM_GROUPS * ROWS, 1)),
                  full((ML_DV, ML_DV))],
        out_specs=[tok_spec, tok_spec] + state_specs,
        out_shape=[
            jax.ShapeDtypeStruct((groups, nc, ROWS, D_MODEL), h_dtype),
            jax.ShapeDtypeStruct((groups, nc, ROWS, D_MODEL), h_dtype),
            jax.ShapeDtypeStruct((nbatch, ML_HEADS, ML_DV, ML_DQK), f32),
            jax.ShapeDtypeStruct((nbatch, ML_HEADS, ML_DQK), f32),
            jax.ShapeDtypeStruct((nbatch, ML_HEADS, LANES), f32),
        ],
        compiler_params=pltpu.CompilerParams(
            dimension_semantics=("parallel", "arbitrary"), vmem_limit_bytes=VMEM_LIMIT),
        name="mixer_core",
    )(z4, z4, z4, z4, z4, zg4, zgt4, c0, n0, m0, ml_g, ws_bd, b_col, eye)
    return hml.reshape(n, D_MODEL), hgm.reshape(n, D_MODEL), c1, n1, m1


def _gmlp_mixing(gm_ws, gm_bs, L):
    reps = ROWS // L
    tril = jnp.tril(jnp.ones((L, L), bool))
    blocks = [jnp.kron(jnp.eye(reps, dtype=f32), jnp.where(tril, gm_ws[g, :L, :L], 0.0)) for g in range(GM_GROUPS)]
    ws_bd = jax.scipy.linalg.block_diag(*blocks).astype(bf16)
    b_col = jnp.concatenate([jnp.tile(gm_bs[g, :L], reps) for g in range(GM_GROUPS)])[:, None]
    return ws_bd, b_col


def _proj_router_kernel(hml_ref, hgm_ref, sga_ref, sgb_ref, x_ref, pa_ref, pb_ref, wo_ref, g_ref, wr_ref, br_ref,
                        x2_ref, xn_ref, comb_ref):
    tm = x_ref.shape[0]
    subs = [slice(r, r + PROJ_SUB) for r in range(0, tm, PROJ_SUB)]
    ab, xn = {}, {}

    def stage_branches(i):
        r = subs[i]
        ab[i] = (_dot(hml_ref[r, :].astype(bf16), pa_ref[...]), _dot(hgm_ref[r, :].astype(bf16), pb_ref[...]))

    def stage_merge(i):
        r = subs[i]
        a, b = ab.pop(i)
        merged = sga_ref[r, :].astype(f32) * a + sgb_ref[r, :].astype(f32) * b
        x2 = x_ref[r, :] + _dot(merged.astype(bf16), wo_ref[...])
        x2_ref[r, :] = x2
        xn[i] = _rms(x2, g_ref[...]).astype(bf16)
        xn_ref[r, :] = xn[i]

    def stage_route(i):
        lg = _dot(xn.pop(i), wr_ref[...]) + br_ref[...]
        lane = lax.broadcasted_iota(jnp.int32, lg.shape, 1).astype(f32)
        cmask = (lane >= MOE_EXPERTS) & (lane < MOE_EXPERTS + MOE_GROUPS)
        cl = jnp.where(cmask, lg, NEG)
        cmax = jnp.max(cl, axis=1, keepdims=True)
        p_grp = 1.0 / jnp.sum(jnp.where(cmask, jnp.exp(cl - cmax), 0.0), axis=1, keepdims=True)
        grp = jnp.min(jnp.where(cl == cmax, lane, 2.0 * LANES), axis=1, keepdims=True) - MOE_EXPERTS
        fmask = (lane >= grp * MOE_PER_GROUP) & (lane < (grp + 1.0) * MOE_PER_GROUP)
        fl = jnp.where(fmask, lg, NEG)
        v1 = jnp.max(fl, axis=1, keepdims=True)
        i1 = jnp.min(jnp.where(fl == v1, lane, 2.0 * LANES), axis=1, keepdims=True)
        fl2 = jnp.where(lane == i1, NEG, fl)
        v2 = jnp.max(fl2, axis=1, keepdims=True)
        i2 = jnp.min(jnp.where(fl2 == v2, lane, 2.0 * LANES), axis=1, keepdims=True)
        e2 = jnp.exp(v2 - v1)
        g1 = p_grp / (1.0 + e2)
        g2 = p_grp * e2 / (1.0 + e2)
        comb_ref[subs[i], :] = (jnp.where(lane == i1, g1, 0.0) + jnp.where(lane == i2, g2, 0.0)
                                + jnp.where(lane == GRP_LANE, grp, 0.0))

    n = len(subs)
    for step in range(n + 2):
        if step < n:
            stage_branches(step)
        if 0 <= step - 1 < n:
            stage_merge(step - 1)
        if 0 <= step - 2 < n:
            stage_route(step - 2)


def _proj_router(hml, hgm, sga, sgb, ga_blk, gb_blk, x, pa, pb, wo, g, wr, br, tm):
    n = x.shape[0]
    row = lambda blk=0: pl.BlockSpec((tm, D_MODEL), lambda i, blk=blk: (i, blk))
    const = lambda shape: pl.BlockSpec(shape, lambda i: (0, 0), pipeline_mode=pl.Buffered(1))
    wfull = const((D_MODEL, D_MODEL))
    return pl.pallas_call(
        _proj_router_kernel,
        grid=(n // tm,),
        in_specs=[row(), row(), row(ga_blk), row(gb_blk), row(), wfull, wfull, wfull,
                  const((1, D_MODEL)), const((D_MODEL, LANES)), const((1, LANES))],
        out_specs=[row(), row(), pl.BlockSpec((tm, LANES), lambda i: (i, 0))],
        out_shape=[jax.ShapeDtypeStruct((n, D_MODEL), f32),
                   jax.ShapeDtypeStruct((n, D_MODEL), bf16),
                   jax.ShapeDtypeStruct((n, LANES), f32)],
        compiler_params=pltpu.CompilerParams(
            dimension_semantics=("parallel",), vmem_limit_bytes=VMEM_LIMIT),
        name="proj_router",
    )(hml, hgm, sga, sgb, x, pa, pb, wo, g, wr, br)


def _moe_final_kernel(xn_ref, comb_ref, x2_ref, ltri_ref, wg_ref, wu_ref, wd_ref, g_ref, y_ref,
                      chl_ref, gp_col_ref, gp_row_ref, pos_col_ref, pos_row_ref):
    grp = pl.program_id(1)
    gf = grp.astype(f32)
    tb = xn_ref.shape[0]

    @pl.when(grp == 0)
    def _():
        y_ref[...] = jnp.zeros_like(y_ref)
        comb = comb_ref[...]
        lane = lax.broadcasted_iota(jnp.int32, comb.shape, 1)
        gcol = comb[:, GRP_LANE:GRP_LANE + 1]
        onehot = jnp.where(lane.astype(f32) == gcol, 1.0, 0.0)
        before = _dot(ltri_ref[...], onehot.astype(bf16))
        pos = jnp.sum(onehot * before, axis=1, keepdims=True)
        gp = jnp.where(lane == 0, gcol, 0.0) + jnp.where(lane == 1, pos, 0.0)
        gp_col_ref[...] = gp
        gp_row_ref[...] = gp.T[:8, :]
        chi = comb.astype(bf16)
        chl_ref[:, :LANES] = chi
        chl_ref[:, LANES:] = (comb - chi.astype(f32)).astype(bf16)

    in_grp_col = gp_col_ref[:, 0:1] == gf
    pos_col_ref[...] = jnp.broadcast_to(jnp.where(in_grp_col, gp_col_ref[:, 1:2], -1.0), pos_col_ref.shape)
    pos_row_ref[...] = jnp.broadcast_to(jnp.where(gp_row_ref[0:1, :] == gf, gp_row_ref[1:2, :], -1.0), (8, tb))
    cnt = jnp.sum(jnp.where(in_grp_col, 1.0, 0.0)).astype(jnp.int32)

    def run_tiles(rows):
        def tile(t, carry):
            base = (t * rows).astype(f32)
            r_iota = lax.broadcasted_iota(jnp.int32, (rows, tb), 0).astype(f32)
            gather = jnp.where(pos_row_ref[0:1, :] - base == r_iota, 1.0, 0.0).astype(bf16)
            x = _dot(gather, xn_ref[...]).astype(bf16)
            c2 = _dot(gather, chl_ref[...])
            c = c2[:, :LANES] + c2[:, LANES:]
            lane = lax.broadcasted_iota(jnp.int32, c.shape, 1)
            hid = []
            for e in range(MOE_PER_GROUP):
                ce = jnp.sum(jnp.where(lane == grp * MOE_PER_GROUP + e, c, 0.0), axis=1, keepdims=True)
                a = _dot(x, wg_ref[e])
                u = _dot(x, wu_ref[e])
                hid.append((a * _sigmoid(a) * u * ce).astype(bf16))
            y = _dot(jnp.concatenate(hid, axis=1), wd_ref[...]).astype(bf16)
            l_iota = lax.broadcasted_iota(jnp.int32, (tb, rows), 1).astype(f32)
            scatter = jnp.where(pos_col_ref[:, :rows] - base == l_iota, 1.0, 0.0).astype(bf16)
            y_ref[...] += _dot(scatter, y)
            return carry

        lax.fori_loop(0, (cnt + rows - 1) // rows, tile, 0)

    n_tiles = [(cnt + rows - 1) // rows for rows in MOE_TILES]
    for i, rows in enumerate(MOE_TILES):
        pick = n_tiles[i] == n_tiles[-1]
        for j in range(i):
            pick = pick & (n_tiles[j] != n_tiles[-1])
        pl.when(pick)(functools.partial(run_tiles, rows))

    @pl.when(grp == MOE_GROUPS - 1)
    def _():
        y_ref[...] = _rms(x2_ref[...] + y_ref[...], g_ref[...])


def _moe_final(xn, comb, x2, wg, wu, wd, g, tb):
    n = x2.shape[0]
    ltri = jnp.tril(jnp.ones((tb, tb), bf16), -1)
    return pl.pallas_call(
        _moe_final_kernel,
        grid=(n // tb, MOE_GROUPS),
        in_specs=[pl.BlockSpec((tb, D_MODEL), lambda i, j: (i, 0)),
                  pl.BlockSpec((tb, LANES), lambda i, j: (i, 0)),
                  pl.BlockSpec((tb, D_MODEL), lambda i, j: (i, 0)),
                  pl.BlockSpec((tb, tb), lambda i, j: (0, 0)),
                  pl.BlockSpec((MOE_PER_GROUP, D_MODEL, MOE_HIDDEN), lambda i, j: (j, 0, 0)),
                  pl.BlockSpec((MOE_PER_GROUP, D_MODEL, MOE_HIDDEN), lambda i, j: (j, 0, 0)),
                  pl.BlockSpec((None, MOE_PER_GROUP * MOE_HIDDEN, D_MODEL), lambda i, j: (j, 0, 0)),
                  pl.BlockSpec((1, D_MODEL), lambda i, j: (0, 0))],
        out_specs=pl.BlockSpec((tb, D_MODEL), lambda i, j: (i, 0)),
        out_shape=jax.ShapeDtypeStruct((n, D_MODEL), f32),
        scratch_shapes=[pltpu.VMEM((tb, 2 * LANES), bf16),
                        pltpu.VMEM((tb, LANES), f32), pltpu.VMEM((8, tb), f32),
                        pltpu.VMEM((tb, 2 * LANES), f32), pltpu.VMEM((8, tb), f32)],
        compiler_params=pltpu.CompilerParams(
            dimension_semantics=("parallel", "arbitrary"), vmem_limit_bytes=VMEM_LIMIT),
        name="moe_final",
    )(xn, comb, x2, ltri, wg, wu, wd.reshape(MOE_GROUPS, MOE_PER_GROUP * MOE_HIDDEN, D_MODEL), g)


def _layer(x3, state, lw, *, prompt):
    nbatch, t, _ = x3.shape
    n = nbatch * t
    x = x3.reshape(n, D_MODEL)
    if prompt:
        L, valid, z_dtype, tm, nblk = ML_CHUNK, ML_CHUNK, bf16, 512, 4
    else:
        L, valid, z_dtype, tm, nblk = t, lw["valid"], f32, 256, 1
    nb = ROWS // L

    z, zg, zgt = _in_proj(x, lw["norm_mix_g"], lw["w_a"], lw["w_b"], lw["w_gate"], lw["gate_bias"],
                          lw["gm_norm_g"], tm, z_dtype)
    c0, n0, m0 = state
    m0 = jnp.broadcast_to(m0[:, :, None], (nbatch, ML_HEADS, LANES))
    ws_bd, b_col = _gmlp_mixing(lw["gm_ws"], lw["gm_bs"], L)
    hml, hgm, c1, n1, m1 = _mixer_core(z, zg, zgt, c0, n0, m0, lw["ml_norm_g"],
                                       ws_bd, b_col, nblk=nblk, nb=nb, L=L, valid=valid, h_dtype=z_dtype)
    keep = t if prompt else valid
    if keep == t:
        sga, sgb, ga_blk, gb_blk = z, z, ZB_GA, ZB_GB
    else:
        rows = lambda a: a.reshape(nbatch, t, a.shape[-1])[:, :keep].reshape(nbatch * keep, a.shape[-1])
        hml, hgm, x = rows(hml), rows(hgm), rows(x)
        sga = rows(z[:, ZB_GA * D_MODEL:(ZB_GA + 1) * D_MODEL])
        sgb = rows(z[:, ZB_GB * D_MODEL:(ZB_GB + 1) * D_MODEL])
        ga_blk = gb_blk = 0
    x2, xn, comb = _proj_router(hml, hgm, sga, sgb, ga_blk, gb_blk, x, lw["p_a"], lw["p_b"], lw["w_out"],
                                lw["norm_ffn_g"], lw["w_router"], lw["b_router"], min(x.shape[0], PROJ_TILE))
    y = _moe_final(xn, comb, x2, lw["e_wg"], lw["e_wu"], lw["e_wd"], lw["out_g"], min(x.shape[0], MOE_BLOCK))
    gv = z[:, ZB_GV * D_MODEL:(ZB_GV + 1) * D_MODEL].reshape(nbatch, t, D_MODEL)
    return y.reshape(nbatch, keep, D_MODEL), (c1, n1, m1[:, :, 0]), gv


def kernel(x_prompt, x_sample, state_mlstm_C, state_mlstm_n, state_mlstm_m, norm_mix_g, w_in, ml_b_i, ml_b_f, ml_norm_g, gm_norm_g, gm_ws, gm_bs, p_a, p_b, w_out, norm_ffn_g, rc_w, rc_b, rf_w, rf_b, e_wg, e_wu, e_wd, final_norm_g):
    depth = w_in.shape[0]
    assert depth == 1, "the final norm is fused into the last layer's MoE kernel; only depth 1 is wired up"
    nbp = x_prompt.shape[0]
    nbs, ts, _ = x_sample.shape

    def layer_weights(l):
        w = w_in[l]
        gates = w[:, GATE_OFF:GATE_OFF + 2 * ML_HEADS]
        return dict(
            norm_mix_g=norm_mix_g[l][None, :],
            w_a=w[:, :GATE_OFF].astype(bf16), w_b=w[:, GATE_OFF + 2 * ML_HEADS:].astype(bf16),
            w_gate=jnp.pad(gates, ((0, 0), (0, LANES - 2 * ML_HEADS))).astype(bf16),
            gate_bias=jnp.pad(jnp.concatenate([ml_b_i[l], ml_b_f[l]]), (0, LANES - 2 * ML_HEADS))[None, :],
            ml_norm_g=ml_norm_g[l],
            gm_norm_g=gm_norm_g[l].reshape(1, GM_GROUPS * GM_DG),
            gm_ws=gm_ws[l], gm_bs=gm_bs[l],
            p_a=p_a[l].astype(bf16), p_b=p_b[l].astype(bf16), w_out=w_out[l].astype(bf16),
            norm_ffn_g=norm_ffn_g[l][None, :],
            w_router=jnp.pad(
                jnp.concatenate([rf_w[l].transpose(1, 0, 2).reshape(D_MODEL, MOE_EXPERTS), rc_w[l]], axis=1),
                ((0, 0), (0, LANES - MOE_EXPERTS - MOE_GROUPS))).astype(bf16),
            b_router=jnp.pad(jnp.concatenate([rf_b[l].reshape(-1), rc_b[l]]),
                             (0, LANES - MOE_EXPERTS - MOE_GROUPS))[None, :],
            e_wg=e_wg[l].astype(bf16), e_wu=e_wu[l].astype(bf16), e_wd=e_wd[l].astype(bf16),
            out_g=final_norm_g[None, :],
        )

    lw = layer_weights(0)

    zero_state = (jnp.zeros((nbp, ML_HEADS, ML_DV, ML_DQK), f32), jnp.zeros((nbp, ML_HEADS, ML_DQK), f32),
                  jnp.zeros((nbp, ML_HEADS), f32))
    y_p, (c_p, n_p, m_p), _ = _layer(x_prompt, zero_state, lw, prompt=True)

    xs = jnp.pad(x_sample, ((0, 0), (0, SAMPLE_PAD_T - ts), (0, 0)))
    y_s, (c_s, n_s, m_s), v_s = _layer(xs, (state_mlstm_C[0], state_mlstm_n[0], state_mlstm_m[0]),
                                       dict(lw, valid=ts), prompt=False)
    v_s = v_s[:, :ts].reshape(nbs, ts, GM_GROUPS, GM_DG)
    return (y_p, y_s, c_p[None], n_p[None], m_p[None], c_s[None], n_s[None], m_s[None], v_s[None])
```

```python
import functools

import numpy as np
import jax
import jax.numpy as jnp
from jax import lax
from jax.experimental import pallas as pl
from jax.experimental.pallas import tpu as pltpu

D_MODEL = 1024
ML_HEADS = 4
ML_DQK = 128
ML_DV = 256
ML_CHUNK = 128
GM_GROUPS = 4
GM_DG = 256
MOE_GROUPS = 4
MOE_PER_GROUP = 8
MOE_EXPERTS = MOE_GROUPS * MOE_PER_GROUP
MOE_HIDDEN = 256
EPS = 1e-6

LANES = 128
SAMPLE_PAD_T = 8
ROWS = 128
N_PAD_ROWS = 16
NEG = -1e30
MOE_TILES = (128, 144, 160)
MOE_BLOCK = 1024
IN_SUB = 256
PROJ_TILE = 1024
PROJ_SUB = 256
GRP_LANE = MOE_EXPERTS
VMEM_LIMIT = 56 * 1024 * 1024

ZB_QK, ZB_V, ZB_OG, ZB_U, ZB_GV, ZB_GA, ZB_GB = range(7)
N_ZB = 7
N_ZB_A = 3
GATE_OFF = 2 * ML_HEADS * ML_DQK + 2 * ML_HEADS * ML_DV

f32 = jnp.float32
bf16 = jnp.bfloat16


def _sigmoid(x):
    return 0.5 * jnp.tanh(0.5 * x) + 0.5


def _log_sigmoid(x):
    return jnp.minimum(x, 0.0) - jnp.log1p(jnp.exp(-jnp.abs(x)))


def _gelu_tanh(x):
    return 0.5 * x * (1.0 + jnp.tanh(np.sqrt(2.0 / np.pi) * (x + 0.044715 * (x * x * x))))


def _rms(x, g):
    return x * lax.rsqrt(jnp.mean(x * x, axis=-1, keepdims=True) + EPS) * g


def _dot(a, b):
    return jnp.dot(a, b, preferred_element_type=f32)


def _dot_nt(a, b):
    return lax.dot_general(a, b, (((1,), (1,)), ((), ())), preferred_element_type=f32)


def _dot_tn(a, b):
    return lax.dot_general(a, b, (((0,), (0,)), ((), ())), preferred_element_type=f32)


def _split_w_in_kernel(w_ref, wa_ref, wb_ref, wg_ref):
    wa_ref[...] = w_ref[:, :GATE_OFF].astype(bf16)
    wb_ref[...] = w_ref[:, GATE_OFF + 2 * ML_HEADS:].astype(bf16)
    gates = w_ref[:, GATE_OFF:GATE_OFF + LANES]
    lane = lax.broadcasted_iota(jnp.int32, gates.shape, 1)
    wg_ref[...] = jnp.where(lane < 2 * ML_HEADS, gates, 0.0).astype(bf16)


def _split_w_in(w):
    d, width = w.shape
    rows = 128
    n_b = width - GATE_OFF - 2 * ML_HEADS
    return pl.pallas_call(
        _split_w_in_kernel,
        grid=(d // rows,),
        in_specs=[pl.BlockSpec((rows, width), lambda i: (i, 0))],
        out_specs=[pl.BlockSpec((rows, GATE_OFF), lambda i: (i, 0)),
                   pl.BlockSpec((rows, n_b), lambda i: (i, 0)),
                   pl.BlockSpec((rows, LANES), lambda i: (i, 0))],
        out_shape=[jax.ShapeDtypeStruct((d, GATE_OFF), bf16),
                   jax.ShapeDtypeStruct((d, n_b), bf16),
                   jax.ShapeDtypeStruct((d, LANES), bf16)],
        compiler_params=pltpu.CompilerParams(dimension_semantics=("parallel",)),
        name="split_w_in",
    )(w)


def _in_proj_kernel(x_ref, g_ref, wa_ref, wb_ref, wg_ref, gb_ref, gmg_ref, z_ref, zg_ref, zgt_ref):
    tm = x_ref.shape[0]
    subs = [slice(r, r + IN_SUB) for r in range(0, tm, IN_SUB)]
    xn = [None] * len(subs)

    def prepare(s):
        rows = subs[s]
        xn[s] = _rms(x_ref[rows, :], g_ref[...]).astype(bf16)
        zg = _dot(xn[s], wg_ref[...]) + gb_ref[...]
        lane = lax.broadcasted_iota(jnp.int32, zg.shape, 1)
        zg = jnp.where((lane >= ML_HEADS) & (lane < 2 * ML_HEADS), _log_sigmoid(zg), zg)
        zg_ref[rows, :] = zg
        zgt_ref[:, rows] = zg.T[:2 * ML_HEADS, :]

    def gelu_group_rms(a, g):
        return _rms(_gelu_tanh(a), gmg_ref[:, g * GM_DG:(g + 1) * GM_DG])

    act = {ZB_QK: None, ZB_V: None, ZB_OG: _sigmoid, ZB_GA: _sigmoid, ZB_GB: _sigmoid,
           ZB_U: _gelu_tanh, ZB_GV: gelu_group_rms}
    per_blk = D_MODEL // GM_DG

    def emit(blk, s):
        for g in range(per_blk):
            cols = slice(blk * D_MODEL + g * GM_DG, blk * D_MODEL + (g + 1) * GM_DG)
            if blk < N_ZB_A:
                a = _dot(xn[s], wa_ref[:, cols])
            else:
                a = _dot(xn[s], wb_ref[:, cols.start - N_ZB_A * D_MODEL:cols.stop - N_ZB_A * D_MODEL])
            if blk == ZB_GV:
                a = gelu_group_rms(a, g)
            elif act[blk] is not None:
                a = act[blk](a)
            z_ref[subs[s], cols] = a.astype(z_ref.dtype)

    for s in range(len(subs)):
        prepare(s)
        emit(0, s)
    for blk in range(1, N_ZB):
        for s in range(len(subs)):
            emit(blk, s)


def _in_proj(x, g, w_a, w_b, w_gate, gate_bias, gm_g, tm, z_dtype):
    n = x.shape[0]
    const = lambda shape: pl.BlockSpec(shape, lambda i: (0, 0), pipeline_mode=pl.Buffered(1))
    return pl.pallas_call(
        _in_proj_kernel,
        grid=(n // tm,),
        in_specs=[
            pl.BlockSpec((tm, D_MODEL), lambda i: (i, 0)),
            const((1, D_MODEL)),
            const((D_MODEL, N_ZB_A * D_MODEL)),
            const((D_MODEL, (N_ZB - N_ZB_A) * D_MODEL)),
            const((D_MODEL, LANES)),
            const((1, LANES)),
            const((1, D_MODEL)),
        ],
        out_specs=[
            pl.BlockSpec((tm, N_ZB * D_MODEL), lambda i: (i, 0)),
            pl.BlockSpec((tm, LANES), lambda i: (i, 0)),
            pl.BlockSpec((2 * ML_HEADS, tm), lambda i: (0, i)),
        ],
        out_shape=[
            jax.ShapeDtypeStruct((n, N_ZB * D_MODEL), z_dtype),
            jax.ShapeDtypeStruct((n, LANES), f32),
            jax.ShapeDtypeStruct((2 * ML_HEADS, n), f32),
        ],
        compiler_params=pltpu.CompilerParams(
            dimension_semantics=("parallel",), vmem_limit_bytes=VMEM_LIMIT),
        name="in_proj",
    )(x, g, w_a, w_b, w_gate, gate_bias, gm_g)


def _mixer_core_kernel(zqk_ref, zv_ref, zog_ref, zu_ref, zgv_ref, zg_ref, zgt_ref,
                       c0_ref, n0_ref, m0_ref, mlg_ref, wsbd_ref, bcol_ref, eye_ref,
                       hml_ref, hgm_ref, c_ref, n_ref, m_ref, *, nblk, nb, L, valid, single_chunk):
    if single_chunk:
        cs_ref, ns_ref, ms_ref = c0_ref, n0_ref, m0_ref
    else:
        cs_ref, ns_ref, ms_ref = c_ref, n_ref, m_ref

        @pl.when(pl.program_id(1) == 0)
        def _():
            c_ref[...] = c0_ref[...]
            n_ref[...] = n0_ref[...]
            m_ref[...] = m0_ref[...]

    shift = L.bit_length() - 1
    pp = lax.broadcasted_iota(jnp.int32, (ROWS, ROWS), 0)
    qq = lax.broadcasted_iota(jnp.int32, (ROWS, ROWS), 1)
    same = (pp >> shift) == (qq >> shift)
    p_col = lax.broadcasted_iota(jnp.int32, (ROWS, 1), 0)
    p_row = lax.broadcasted_iota(jnp.int32, (1, ROWS), 1)
    col_ok = (p_col & (L - 1)) < valid
    row_ok = (p_row & (L - 1)) < valid
    causal = same & (qq <= pp) & row_ok
    upper = same & (pp <= qq)
    scale = ML_DQK ** -0.5

    def per_seq(fn):
        return jnp.concatenate([fn(b) for b in range(nb)], axis=0) if nb > 1 else fn(0)

    stores = []
    chains = [(r, h) for r in range(nblk) for h in range(ML_HEADS)]
    each = lambda fn: [fn(i, r, h) for i, (r, h) in enumerate(chains)]

    q = each(lambda i, r, h: zqk_ref[r, :, h * ML_DQK:(h + 1) * ML_DQK])
    k = each(lambda i, r, h: zqk_ref[r, :, (ML_HEADS + h) * ML_DQK:(ML_HEADS + h + 1) * ML_DQK])
    v = each(lambda i, r, h: zv_ref[r, :, h * ML_DV:(h + 1) * ML_DV])
    qc = [x.astype(bf16) for x in q]
    kc = [x.astype(bf16) for x in k]
    c0 = each(lambda i, r, h: cs_ref[r * nb:(r + 1) * nb, h])
    one_seq = nb == 1
    if one_seq:
        m0 = each(lambda i, r, h: ms_ref[r, h:h + 1, 0:1])
    else:
        m0 = each(lambda i, r, h: per_seq(lambda b: jnp.broadcast_to(ms_ref[r * nb + b, h:h + 1, 0:1], (L, 1))))
    n0 = each(lambda i, r, h: per_seq(lambda b: jnp.broadcast_to(ns_ref[r * nb + b, h:h + 1, :], (L, ML_DQK))))
    logi_col = each(lambda i, r, h: zg_ref[r, :, h:h + 1])
    logi_row = each(lambda i, r, h: zgt_ref[r, h:h + 1, :])
    logf_col = each(lambda i, r, h: jnp.where(col_ok, zg_ref[r, :, ML_HEADS + h:ML_HEADS + h + 1], 0.0))
    logf_row = each(lambda i, r, h: jnp.where(row_ok, zgt_ref[r, ML_HEADS + h:ML_HEADS + h + 1, :], 0.0))

    if one_seq:
        n_rows = each(lambda i, r, h: jnp.broadcast_to(ns_ref[r, h:h + 1, :], (N_PAD_ROWS, ML_DQK)).astype(bf16))
        qkc = each(lambda i, r, h: _dot_nt(
            qc[i], jnp.concatenate([kc[i], c0[i].reshape(ML_DV, ML_DQK).astype(bf16), n_rows[i]], axis=0)))
    else:
        qkc = each(lambda i, r, h: _dot_nt(
            qc[i], jnp.concatenate([kc[i], c0[i].reshape(nb * ML_DV, ML_DQK).astype(bf16)], axis=0)))
    b_col = each(lambda i, r, h: jnp.sum(jnp.where(causal, logf_row[i], 0.0), axis=1, keepdims=True))
    b_row = each(lambda i, r, h: jnp.sum(jnp.where(upper, logf_col[i], 0.0), axis=0, keepdims=True))
    if one_seq:
        b_last_col = each(lambda i, r, h: jnp.sum(logf_row[i], axis=1, keepdims=True))
        b_last_row = b_last_col
    else:
        b_last_col = each(lambda i, r, h: jnp.sum(jnp.where(same, logf_row[i], 0.0), axis=1, keepdims=True))
        b_last_row = each(lambda i, r, h: jnp.sum(jnp.where(same, logf_col[i], 0.0), axis=0, keepdims=True))
    d = each(lambda i, r, h: jnp.where(causal, b_col[i] - b_row[i] + logi_row[i], NEG))
    inter = each(lambda i, r, h: b_col[i] + m0[i])
    m_col = each(lambda i, r, h: jnp.maximum(inter[i], jnp.max(d[i], axis=1, keepdims=True)))
    w_intra = each(lambda i, r, h: jnp.exp(d[i] - m_col[i]) * scale)
    w_inter = each(lambda i, r, h: jnp.exp(inter[i] - m_col[i]) * scale)
    s = each(lambda i, r, h: qkc[i][:, :ROWS] * w_intra[i])
    q_mem = each(lambda i, r, h: per_seq(
        lambda b: qkc[i][b * L:(b + 1) * L, ROWS + b * ML_DV:ROWS + (b + 1) * ML_DV]))
    if one_seq:
        ones_blk = jnp.ones((ROWS, LANES), bf16)
        sv = each(lambda i, r, h: _dot(s[i].astype(bf16), jnp.concatenate([v[i].astype(bf16), ones_blk], axis=1)))
        num = each(lambda i, r, h: sv[i][:, :ML_DV] + w_inter[i] * q_mem[i])
        s_sum = each(lambda i, r, h: sv[i][:, ML_DV:ML_DV + 1])
        q_n = each(lambda i, r, h: qkc[i][:, ROWS + ML_DV:ROWS + ML_DV + 1])
    else:
        num = each(lambda i, r, h: _dot(s[i].astype(bf16), v[i].astype(bf16)) + w_inter[i] * q_mem[i])
        s_sum = each(lambda i, r, h: jnp.sum(s[i], axis=1, keepdims=True))
        q_n = each(lambda i, r, h: jnp.sum(q[i].astype(f32) * n0[i], axis=1, keepdims=True))
    den = each(lambda i, r, h: s_sum[i] + w_inter[i] * q_n[i])
    ones = jnp.ones((ML_DV, LANES), bf16)

    def head_norm(x, g):
        if not one_seq:
            return _rms(x, g)
        ssq = _dot((x * x).astype(bf16), ones)
        scale = lax.rsqrt(ssq * (1.0 / ML_DV) + EPS)
        return x * jnp.concatenate([scale] * (ML_DV // LANES), axis=1) * g

    hh = each(lambda i, r, h: head_norm(num[i] / jnp.maximum(jnp.abs(den[i]), jnp.exp(-m_col[i])), mlg_ref[h:h + 1, :]))
    for i, (r, h) in enumerate(chains):
        sog = zog_ref[r, :, h * ML_DV:(h + 1) * ML_DV].astype(f32)
        stores.append((hml_ref, (r, slice(None), slice(h * ML_DV, (h + 1) * ML_DV)), (sog * hh[i]).astype(hml_ref.dtype)))

    wend_col = each(lambda i, r, h: jnp.where(col_ok, b_last_col[i] - b_col[i] + logi_col[i], NEG))
    wend_row = each(lambda i, r, h: jnp.where(row_ok, b_last_row[i] - b_row[i] + logi_row[i], NEG))
    if one_seq:
        m_new = each(lambda i, r, h: jnp.maximum(b_last_col[i] + m0[i], jnp.max(wend_row[i], axis=1, keepdims=True)))
    else:
        m_new = each(lambda i, r, h: jnp.maximum(
            b_last_col[i] + m0[i], jnp.max(jnp.where(same, wend_row[i], NEG), axis=1, keepdims=True)))
    decay = each(lambda i, r, h: jnp.exp(b_last_col[i] + m0[i] - m_new[i]))
    wend = each(lambda i, r, h: jnp.exp(wend_col[i] - m_new[i]))
    vw = each(lambda i, r, h: (v[i].astype(f32) * wend[i]).astype(bf16))
    kw = each(lambda i, r, h: k[i].astype(f32) * wend[i])
    if nb == 1:
        upd = each(lambda i, r, h: _dot_tn(vw[i], kc[i]))
    else:
        vw_t = each(lambda i, r, h: _dot_nt(eye_ref[...], vw[i]).astype(bf16))
        seq_of_lane = lax.broadcasted_iota(jnp.int32, (ML_DV, ROWS), 1) >> shift
        upd = each(lambda i, r, h: _dot(jnp.concatenate(
            [jnp.where(seq_of_lane == b, vw_t[i], jnp.zeros_like(vw_t[i])) for b in range(nb)], axis=0), kc[i]))
    for i, (r, h) in enumerate(chains):
        for b in range(nb):
            slot = r * nb + b
            dec = decay[i][b * L:b * L + 1, :]
            stores.append((c_ref, (slot, h), dec * c0[i][b] + upd[i][b * ML_DV:(b + 1) * ML_DV]))
            stores.append((n_ref, (slot, slice(h, h + 1), slice(None)),
                           dec * ns_ref[slot, h:h + 1, :] + jnp.sum(kw[i][b * L:(b + 1) * L], axis=0, keepdims=True)))
            stores.append((m_ref, (slot, slice(h, h + 1), slice(None)),
                           jnp.broadcast_to(m_new[i][b * L:b * L + 1, :], (1, LANES))))

    for r in range(nblk):
        gv = jnp.concatenate([zgv_ref[r, :, g * GM_DG:(g + 1) * GM_DG].astype(bf16) for g in range(GM_GROUPS)], axis=0)
        mixed = _dot(wsbd_ref[...], gv) + bcol_ref[...]
        for g in range(GM_GROUPS):
            u = zu_ref[r, :, g * GM_DG:(g + 1) * GM_DG].astype(f32)
            stores.append((hgm_ref, (r, slice(None), slice(g * GM_DG, (g + 1) * GM_DG)),
                           (u * mixed[g * ROWS:(g + 1) * ROWS]).astype(hgm_ref.dtype)))

    for ref, idx, val in stores:
        ref[idx] = val


def _mixer_core(z, zg, zgt, c0, n0, m0, ml_g, ws_bd, b_col, *, nblk, nb, L, valid, h_dtype):
    n = z.shape[0]
    nbatch = c0.shape[0]
    groups = nbatch // nb
    nc = n // groups // ROWS
    z4 = z.reshape(groups, nc, ROWS, N_ZB * D_MODEL)
    zg4 = zg.reshape(groups, nc, ROWS, LANES)
    zgt4 = zgt.reshape(2 * ML_HEADS, groups, nc, ROWS).transpose(1, 2, 0, 3)
    zspec = lambda blk: pl.BlockSpec((nblk, None, ROWS, D_MODEL), lambda b, c, blk=blk: (b, c, 0, blk))
    full = lambda shape: pl.BlockSpec(shape, lambda b, c: (0,) * len(shape))
    state_specs = [
        pl.BlockSpec((nblk * nb, ML_HEADS, ML_DV, ML_DQK), lambda b, c: (b, 0, 0, 0)),
        pl.BlockSpec((nblk * nb, ML_HEADS, ML_DQK), lambda b, c: (b, 0, 0)),
        pl.BlockSpec((nblk * nb, ML_HEADS, LANES), lambda b, c: (b, 0, 0)),
    ]
    tok_spec = zspec(0)
    eye = jnp.eye(ML_DV, dtype=bf16)
    hml, hgm, c1, n1, m1 = pl.pallas_call(
        functools.partial(_mixer_core_kernel, nblk=nblk, nb=nb, L=L, valid=valid, single_chunk=(nc == 1)),
        grid=(groups // nblk, nc),
        in_specs=[zspec(ZB_QK), zspec(ZB_V), zspec(ZB_OG), zspec(ZB_U), zspec(ZB_GV),
                  pl.BlockSpec((nblk, None, ROWS, LANES), lambda b, c: (b, c, 0, 0)),
                  pl.BlockSpec((nblk, None, 2 * ML_HEADS, ROWS), lambda b, c: (b, c, 0, 0)),
                  *state_specs,
                  full((ML_HEADS, ML_DV)), full((GM_GROUPS * ROWS, GM_GROUPS * ROWS)), full((GM_GROUPS * ROWS, 1)),
                  full((ML_DV, ML_DV))],
        out_specs=[tok_spec, tok_spec] + state_specs,
        out_shape=[
            jax.ShapeDtypeStruct((groups, nc, ROWS, D_MODEL), h_dtype),
            jax.ShapeDtypeStruct((groups, nc, ROWS, D_MODEL), h_dtype),
            jax.ShapeDtypeStruct((nbatch, ML_HEADS, ML_DV, ML_DQK), f32),
            jax.ShapeDtypeStruct((nbatch, ML_HEADS, ML_DQK), f32),
            jax.ShapeDtypeStruct((nbatch, ML_HEADS, LANES), f32),
        ],
        compiler_params=pltpu.CompilerParams(
            dimension_semantics=("parallel", "arbitrary"), vmem_limit_bytes=VMEM_LIMIT),
        name="mixer_core",
    )(z4, z4, z4, z4, z4, zg4, zgt4, c0, n0, m0, ml_g, ws_bd, b_col, eye)
    return hml.reshape(n, D_MODEL), hgm.reshape(n, D_MODEL), c1, n1, m1


def _gmlp_mixing(gm_ws, gm_bs, L):
    reps = ROWS // L
    tril = jnp.tril(jnp.ones((L, L), bool))
    blocks = [jnp.kron(jnp.eye(reps, dtype=f32), jnp.where(tril, gm_ws[g, :L, :L], 0.0)) for g in range(GM_GROUPS)]
    ws_bd = jax.scipy.linalg.block_diag(*blocks).astype(bf16)
    b_col = jnp.concatenate([jnp.tile(gm_bs[g, :L], reps) for g in range(GM_GROUPS)])[:, None]
    return ws_bd, b_col


def _proj_router_kernel(hml_ref, hgm_ref, sga_ref, sgb_ref, x_ref, pa_ref, pb_ref, wo_ref, g_ref, wr_ref, br_ref,
                        x2_ref, xn_ref, comb_ref):
    tm = x_ref.shape[0]
    subs = [slice(r, r + PROJ_SUB) for r in range(0, tm, PROJ_SUB)]
    ab, xn = {}, {}

    def stage_branches(i):
        r = subs[i]
        ab[i] = (_dot(hml_ref[r, :].astype(bf16), pa_ref[...]), _dot(hgm_ref[r, :].astype(bf16), pb_ref[...]))

    def stage_merge(i):
        r = subs[i]
        a, b = ab.pop(i)
        merged = sga_ref[r, :].astype(f32) * a + sgb_ref[r, :].astype(f32) * b
        x2 = x_ref[r, :] + _dot(merged.astype(bf16), wo_ref[...])
        x2_ref[r, :] = x2
        xn[i] = _rms(x2, g_ref[...]).astype(bf16)
        xn_ref[r, :] = xn[i]

    def stage_route(i):
        lg = _dot(xn.pop(i), wr_ref[...]) + br_ref[...]
        lane = lax.broadcasted_iota(jnp.int32, lg.shape, 1).astype(f32)
        cmask = (lane >= MOE_EXPERTS) & (lane < MOE_EXPERTS + MOE_GROUPS)
        cl = jnp.where(cmask, lg, NEG)
        cmax = jnp.max(cl, axis=1, keepdims=True)
        p_grp = 1.0 / jnp.sum(jnp.where(cmask, jnp.exp(cl - cmax), 0.0), axis=1, keepdims=True)
        grp = jnp.min(jnp.where(cl == cmax, lane, 2.0 * LANES), axis=1, keepdims=True) - MOE_EXPERTS
        fmask = (lane >= grp * MOE_PER_GROUP) & (lane < (grp + 1.0) * MOE_PER_GROUP)
        fl = jnp.where(fmask, lg, NEG)
        v1 = jnp.max(fl, axis=1, keepdims=True)
        i1 = jnp.min(jnp.where(fl == v1, lane, 2.0 * LANES), axis=1, keepdims=True)
        fl2 = jnp.where(lane == i1, NEG, fl)
        v2 = jnp.max(fl2, axis=1, keepdims=True)
        i2 = jnp.min(jnp.where(fl2 == v2, lane, 2.0 * LANES), axis=1, keepdims=True)
        e2 = jnp.exp(v2 - v1)
        g1 = p_grp / (1.0 + e2)
        g2 = p_grp * e2 / (1.0 + e2)
        comb_ref[subs[i], :] = (jnp.where(lane == i1, g1, 0.0) + jnp.where(lane == i2, g2, 0.0)
                                + jnp.where(lane == GRP_LANE, grp, 0.0))

    n = len(subs)
    for step in range(n + 2):
        if step < n:
            stage_branches(step)
        if 0 <= step - 1 < n:
            stage_merge(step - 1)
        if 0 <= step - 2 < n:
            stage_route(step - 2)


def _proj_router(hml, hgm, sga, sgb, ga_blk, gb_blk, x, pa, pb, wo, g, wr, br, tm):
    n = x.shape[0]
    row = lambda blk=0: pl.BlockSpec((tm, D_MODEL), lambda i, blk=blk: (i, blk))
    const = lambda shape: pl.BlockSpec(shape, lambda i: (0, 0), pipeline_mode=pl.Buffered(1))
    wfull = const((D_MODEL, D_MODEL))
    return pl.pallas_call(
        _proj_router_kernel,
        grid=(n // tm,),
        in_specs=[row(), row(), row(ga_blk), row(gb_blk), row(), wfull, wfull, wfull,
                  const((1, D_MODEL)), const((D_MODEL, LANES)), const((1, LANES))],
        out_specs=[row(), row(), pl.BlockSpec((tm, LANES), lambda i: (i, 0))],
        out_shape=[jax.ShapeDtypeStruct((n, D_MODEL), f32),
                   jax.ShapeDtypeStruct((n, D_MODEL), bf16),
                   jax.ShapeDtypeStruct((n, LANES), f32)],
        compiler_params=pltpu.CompilerParams(
            dimension_semantics=("parallel",), vmem_limit_bytes=VMEM_LIMIT),
        name="proj_router",
    )(hml, hgm, sga, sgb, x, pa, pb, wo, g, wr, br)


def _moe_final_kernel(xn_ref, comb_ref, x2_ref, ltri_ref, wg_ref, wu_ref, wd_ref, g_ref, y_ref,
                      chl_ref, gp_col_ref, gp_row_ref, pos_col_ref, pos_row_ref):
    grp = pl.program_id(1)
    gf = grp.astype(f32)
    tb = xn_ref.shape[0]

    @pl.when(grp == 0)
    def _():
        y_ref[...] = jnp.zeros_like(y_ref)
        comb = comb_ref[...]
        lane = lax.broadcasted_iota(jnp.int32, comb.shape, 1)
        gcol = comb[:, GRP_LANE:GRP_LANE + 1]
        onehot = jnp.where(lane.astype(f32) == gcol, 1.0, 0.0)
        before = _dot(ltri_ref[...], onehot.astype(bf16))
        pos = jnp.sum(onehot * before, axis=1, keepdims=True)
        gp = jnp.where(lane == 0, gcol, 0.0) + jnp.where(lane == 1, pos, 0.0)
        gp_col_ref[...] = gp
        gp_row_ref[...] = gp.T[:8, :]
        chi = comb.astype(bf16)
        chl_ref[:, :LANES] = chi
        chl_ref[:, LANES:] = (comb - chi.astype(f32)).astype(bf16)

    in_grp_col = gp_col_ref[:, 0:1] == gf
    pos_col_ref[...] = jnp.broadcast_to(jnp.where(in_grp_col, gp_col_ref[:, 1:2], -1.0), pos_col_ref.shape)
    pos_row_ref[...] = jnp.broadcast_to(jnp.where(gp_row_ref[0:1, :] == gf, gp_row_ref[1:2, :], -1.0), (8, tb))
    cnt = jnp.sum(jnp.where(in_grp_col, 1.0, 0.0)).astype(jnp.int32)

    def run_tiles(rows):
        def tile(t, carry):
            base = (t * rows).astype(f32)
            r_iota = lax.broadcasted_iota(jnp.int32, (rows, tb), 0).astype(f32)
            gather = jnp.where(pos_row_ref[0:1, :] - base == r_iota, 1.0, 0.0).astype(bf16)
            x = _dot(gather, xn_ref[...]).astype(bf16)
            c2 = _dot(gather, chl_ref[...])
            c = c2[:, :LANES] + c2[:, LANES:]
            lane = lax.broadcasted_iota(jnp.int32, c.shape, 1)
            hid = []
            for e in range(MOE_PER_GROUP):
                ce = jnp.sum(jnp.where(lane == grp * MOE_PER_GROUP + e, c, 0.0), axis=1, keepdims=True)
                a = _dot(x, wg_ref[e])
                u = _dot(x, wu_ref[e])
                hid.append((a * _sigmoid(a) * u * ce).astype(bf16))
            y = _dot(jnp.concatenate(hid, axis=1), wd_ref[...]).astype(bf16)
            l_iota = lax.broadcasted_iota(jnp.int32, (tb, rows), 1).astype(f32)
            scatter = jnp.where(pos_col_ref[:, :rows] - base == l_iota, 1.0, 0.0).astype(bf16)
            y_ref[...] += _dot(scatter, y)
            return carry

        lax.fori_loop(0, (cnt + rows - 1) // rows, tile, 0)

    n_tiles = [(cnt + rows - 1) // rows for rows in MOE_TILES]
    for i, rows in enumerate(MOE_TILES):
        pick = n_tiles[i] == n_tiles[-1]
        for j in range(i):
            pick = pick & (n_tiles[j] != n_tiles[-1])
        pl.when(pick)(functools.partial(run_tiles, rows))

    @pl.when(grp == MOE_GROUPS - 1)
    def _():
        y_ref[...] = _rms(x2_ref[...] + y_ref[...], g_ref[...])


def _moe_final(xn, comb, x2, wg, wu, wd, g, tb):
    n = x2.shape[0]
    ltri = jnp.tril(jnp.ones((tb, tb), bf16), -1)
    return pl.pallas_call(
        _moe_final_kernel,
        grid=(n // tb, MOE_GROUPS),
        in_specs=[pl.BlockSpec((tb, D_MODEL), lambda i, j: (i, 0)),
                  pl.BlockSpec((tb, LANES), lambda i, j: (i, 0)),
                  pl.BlockSpec((tb, D_MODEL), lambda i, j: (i, 0)),
                  pl.BlockSpec((tb, tb), lambda i, j: (0, 0)),
                  pl.BlockSpec((MOE_PER_GROUP, D_MODEL, MOE_HIDDEN), lambda i, j: (j, 0, 0)),
                  pl.BlockSpec((MOE_PER_GROUP, D_MODEL, MOE_HIDDEN), lambda i, j: (j, 0, 0)),
                  pl.BlockSpec((None, MOE_PER_GROUP * MOE_HIDDEN, D_MODEL), lambda i, j: (j, 0, 0)),
                  pl.BlockSpec((1, D_MODEL), lambda i, j: (0, 0))],
        out_specs=pl.BlockSpec((tb, D_MODEL), lambda i, j: (i, 0)),
        out_shape=jax.ShapeDtypeStruct((n, D_MODEL), f32),
        scratch_shapes=[pltpu.VMEM((tb, 2 * LANES), bf16),
                        pltpu.VMEM((tb, LANES), f32), pltpu.VMEM((8, tb), f32),
                        pltpu.VMEM((tb, 2 * LANES), f32), pltpu.VMEM((8, tb), f32)],
        compiler_params=pltpu.CompilerParams(
            dimension_semantics=("parallel", "arbitrary"), vmem_limit_bytes=VMEM_LIMIT),
        name="moe_final",
    )(xn, comb, x2, ltri, wg, wu, wd.reshape(MOE_GROUPS, MOE_PER_GROUP * MOE_HIDDEN, D_MODEL), g)


def _layer(x3, state, lw, *, prompt):
    nbatch, t, _ = x3.shape
    n = nbatch * t
    x = x3.reshape(n, D_MODEL)
    if prompt:
        L, valid, z_dtype, tm, nblk = ML_CHUNK, ML_CHUNK, bf16, 1024, 4
    else:
        L, valid, z_dtype, tm, nblk = t, lw["valid"], f32, 256, 1
    nb = ROWS // L

    z, zg, zgt = _in_proj(x, lw["norm_mix_g"], lw["w_a"], lw["w_b"], lw["w_gate"], lw["gate_bias"],
                          lw["gm_norm_g"], tm, z_dtype)
    c0, n0, m0 = state
    m0 = jnp.broadcast_to(m0[:, :, None], (nbatch, ML_HEADS, LANES))
    ws_bd, b_col = _gmlp_mixing(lw["gm_ws"], lw["gm_bs"], L)
    hml, hgm, c1, n1, m1 = _mixer_core(z, zg, zgt, c0, n0, m0, lw["ml_norm_g"],
                                       ws_bd, b_col, nblk=nblk, nb=nb, L=L, valid=valid, h_dtype=z_dtype)
    keep = t if prompt else valid
    if keep == t:
        sga, sgb, ga_blk, gb_blk = z, z, ZB_GA, ZB_GB
    else:
        rows = lambda a: a.reshape(nbatch, t, a.shape[-1])[:, :keep].reshape(nbatch * keep, a.shape[-1])
        hml, hgm, x = rows(hml), rows(hgm), rows(x)
        sga = rows(z[:, ZB_GA * D_MODEL:(ZB_GA + 1) * D_MODEL])
        sgb = rows(z[:, ZB_GB * D_MODEL:(ZB_GB + 1) * D_MODEL])
        ga_blk = gb_blk = 0
    x2, xn, comb = _proj_router(hml, hgm, sga, sgb, ga_blk, gb_blk, x, lw["p_a"], lw["p_b"], lw["w_out"],
                                lw["norm_ffn_g"], lw["w_router"], lw["b_router"], min(x.shape[0], PROJ_TILE))
    y = _moe_final(xn, comb, x2, lw["e_wg"], lw["e_wu"], lw["e_wd"], lw["out_g"], min(x.shape[0], MOE_BLOCK))
    gv = z[:, ZB_GV * D_MODEL:(ZB_GV + 1) * D_MODEL].reshape(nbatch, t, D_MODEL)
    return y.reshape(nbatch, keep, D_MODEL), (c1, n1, m1[:, :, 0]), gv


def kernel(x_prompt, x_sample, state_mlstm_C, state_mlstm_n, state_mlstm_m, norm_mix_g, w_in, ml_b_i, ml_b_f, ml_norm_g, gm_norm_g, gm_ws, gm_bs, p_a, p_b, w_out, norm_ffn_g, rc_w, rc_b, rf_w, rf_b, e_wg, e_wu, e_wd, final_norm_g):
    depth = w_in.shape[0]
    assert depth == 1, "the final norm is fused into the last layer's MoE kernel; only depth 1 is wired up"
    nbp = x_prompt.shape[0]
    nbs, ts, _ = x_sample.shape

    def layer_weights(l):
        w_a, w_b, w_gate = _split_w_in(w_in[l])
        return dict(
            norm_mix_g=norm_mix_g[l][None, :],
            w_a=w_a, w_b=w_b, w_gate=w_gate,
            gate_bias=jnp.pad(jnp.concatenate([ml_b_i[l], ml_b_f[l]]), (0, LANES - 2 * ML_HEADS))[None, :],
            ml_norm_g=ml_norm_g[l],
            gm_norm_g=gm_norm_g[l].reshape(1, GM_GROUPS * GM_DG),
            gm_ws=gm_ws[l], gm_bs=gm_bs[l],
            p_a=p_a[l].astype(bf16), p_b=p_b[l].astype(bf16), w_out=w_out[l].astype(bf16),
            norm_ffn_g=norm_ffn_g[l][None, :],
            w_router=jnp.pad(
                jnp.concatenate([rf_w[l].transpose(1, 0, 2).reshape(D_MODEL, MOE_EXPERTS), rc_w[l]], axis=1),
                ((0, 0), (0, LANES - MOE_EXPERTS - MOE_GROUPS))).astype(bf16),
            b_router=jnp.pad(jnp.concatenate([rf_b[l].reshape(-1), rc_b[l]]),
                             (0, LANES - MOE_EXPERTS - MOE_GROUPS))[None, :],
            e_wg=e_wg[l].astype(bf16), e_wu=e_wu[l].astype(bf16), e_wd=e_wd[l].astype(bf16),
            out_g=final_norm_g[None, :],
        )

    lw = layer_weights(0)

    zero_state = (jnp.zeros((nbp, ML_HEADS, ML_DV, ML_DQK), f32), jnp.zeros((nbp, ML_HEADS, ML_DQK), f32),
                  jnp.zeros((nbp, ML_HEADS), f32))
    y_p, (c_p, n_p, m_p), _ = _layer(x_prompt, zero_state, lw, prompt=True)

    xs = jnp.pad(x_sample, ((0, 0), (0, SAMPLE_PAD_T - ts), (0, 0)))
    y_s, (c_s, n_s, m_s), v_s = _layer(xs, (state_mlstm_C[0], state_mlstm_n[0], state_mlstm_m[0]),
                                       dict(lw, valid=ts), prompt=False)
    v_s = v_s[:, :ts].reshape(nbs, ts, GM_GROUPS, GM_DG)
    return (y_p, y_s, c_p[None], n_p[None], m_p[None], c_s[None], n_s[None], m_s[None], v_s[None])
```

```python
import functools

import numpy as np
import jax
import jax.numpy as jnp
from jax import lax
from jax.experimental import pallas as pl
from jax.experimental.pallas import tpu as pltpu

D_MODEL = 1024
ML_HEADS = 4
ML_DQK = 128
ML_DV = 256
ML_CHUNK = 128
GM_GROUPS = 4
GM_DG = 256
MOE_GROUPS = 4
MOE_PER_GROUP = 8
MOE_EXPERTS = MOE_GROUPS * MOE_PER_GROUP
MOE_HIDDEN = 256
EPS = 1e-6

LANES = 128
SAMPLE_PAD_T = 8
ROWS = 128
N_PAD_ROWS = 16
NEG = -1e30
MOE_TILES = (128, 144, 160)
MOE_BLOCK = 1024
IN_SUB = 256
PROJ_TILE = 1024
PROJ_SUB = 256
GRP_LANE = MOE_EXPERTS
VMEM_LIMIT = 56 * 1024 * 1024

ZB_QK, ZB_V, ZB_OG, ZB_U, ZB_GV, ZB_GA, ZB_GB = range(7)
N_ZB = 7
N_ZB_A = 3
GATE_OFF = 2 * ML_HEADS * ML_DQK + 2 * ML_HEADS * ML_DV

f32 = jnp.float32
bf16 = jnp.bfloat16


def _sigmoid(x):
    return 0.5 * jnp.tanh(0.5 * x) + 0.5


def _log_sigmoid(x):
    return jnp.minimum(x, 0.0) - jnp.log1p(jnp.exp(-jnp.abs(x)))


def _gelu_tanh(x):
    return 0.5 * x * (1.0 + jnp.tanh(np.sqrt(2.0 / np.pi) * (x + 0.044715 * (x * x * x))))


def _rms(x, g):
    return x * lax.rsqrt(jnp.mean(x * x, axis=-1, keepdims=True) + EPS) * g


def _dot(a, b):
    return jnp.dot(a, b, preferred_element_type=f32)


def _dot_nt(a, b):
    return lax.dot_general(a, b, (((1,), (1,)), ((), ())), preferred_element_type=f32)


def _dot_tn(a, b):
    return lax.dot_general(a, b, (((0,), (0,)), ((), ())), preferred_element_type=f32)


def _in_proj_kernel(x_ref, g_ref, wa_ref, wb_ref, wg_ref, gb_ref, gmg_ref, z_ref, zg_ref, zgt_ref):
    tm = x_ref.shape[0]
    subs = [slice(r, r + IN_SUB) for r in range(0, tm, IN_SUB)]
    xn = [None] * len(subs)

    def prepare(s):
        rows = subs[s]
        xn[s] = _rms(x_ref[rows, :], g_ref[...]).astype(bf16)
        zg = _dot(xn[s], wg_ref[...]) + gb_ref[...]
        lane = lax.broadcasted_iota(jnp.int32, zg.shape, 1)
        zg = jnp.where((lane >= ML_HEADS) & (lane < 2 * ML_HEADS), _log_sigmoid(zg), zg)
        zg_ref[rows, :] = zg
        zgt_ref[:, rows] = zg.T[:2 * ML_HEADS, :]

    def gelu_group_rms(a, g):
        return _rms(_gelu_tanh(a), gmg_ref[:, g * GM_DG:(g + 1) * GM_DG])

    act = {ZB_QK: None, ZB_V: None, ZB_OG: _sigmoid, ZB_GA: _sigmoid, ZB_GB: _sigmoid,
           ZB_U: _gelu_tanh, ZB_GV: gelu_group_rms}
    per_blk = D_MODEL // GM_DG

    def emit(blk, s):
        for g in range(per_blk):
            cols = slice(blk * D_MODEL + g * GM_DG, blk * D_MODEL + (g + 1) * GM_DG)
            if blk < N_ZB_A:
                a = _dot(xn[s], wa_ref[:, cols])
            else:
                a = _dot(xn[s], wb_ref[:, cols.start - N_ZB_A * D_MODEL:cols.stop - N_ZB_A * D_MODEL])
            if blk == ZB_GV:
                a = gelu_group_rms(a, g)
            elif act[blk] is not None:
                a = act[blk](a)
            z_ref[subs[s], cols] = a.astype(z_ref.dtype)

    for s in range(len(subs)):
        prepare(s)
        emit(0, s)
    for blk in range(1, N_ZB):
        for s in range(len(subs)):
            emit(blk, s)


def _in_proj(x, g, w_a, w_b, w_gate, gate_bias, gm_g, tm, z_dtype):
    n = x.shape[0]
    const = lambda shape: pl.BlockSpec(shape, lambda i: (0, 0), pipeline_mode=pl.Buffered(1))
    return pl.pallas_call(
        _in_proj_kernel,
        grid=(n // tm,),
        in_specs=[
            pl.BlockSpec((tm, D_MODEL), lambda i: (i, 0)),
            const((1, D_MODEL)),
            const((D_MODEL, N_ZB_A * D_MODEL)),
            const((D_MODEL, (N_ZB - N_ZB_A) * D_MODEL)),
            const((D_MODEL, LANES)),
            const((1, LANES)),
            const((1, D_MODEL)),
        ],
        out_specs=[
            pl.BlockSpec((tm, N_ZB * D_MODEL), lambda i: (i, 0)),
            pl.BlockSpec((tm, LANES), lambda i: (i, 0)),
            pl.BlockSpec((2 * ML_HEADS, tm), lambda i: (0, i)),
        ],
        out_shape=[
            jax.ShapeDtypeStruct((n, N_ZB * D_MODEL), z_dtype),
            jax.ShapeDtypeStruct((n, LANES), f32),
            jax.ShapeDtypeStruct((2 * ML_HEADS, n), f32),
        ],
        compiler_params=pltpu.CompilerParams(
            dimension_semantics=("parallel",), vmem_limit_bytes=VMEM_LIMIT),
        name="in_proj",
    )(x, g, w_a, w_b, w_gate, gate_bias, gm_g)


def _mixer_core_kernel(zqk_ref, zv_ref, zog_ref, zu_ref, zgv_ref, zg_ref, zgt_ref,
                       c0_ref, n0_ref, m0_ref, mlg_ref, wsbd_ref, bcol_ref, eye_ref,
                       hml_ref, hgm_ref, c_ref, n_ref, m_ref, *, nblk, nb, L, valid, single_chunk):
    if single_chunk:
        cs_ref, ns_ref, ms_ref = c0_ref, n0_ref, m0_ref
    else:
        cs_ref, ns_ref, ms_ref = c_ref, n_ref, m_ref

        @pl.when(pl.program_id(1) == 0)
        def _():
            c_ref[...] = c0_ref[...]
            n_ref[...] = n0_ref[...]
            m_ref[...] = m0_ref[...]

    seq_bits = nb.bit_length() - 1
    seq_of = lambda p: p & (nb - 1)
    pos_of = lambda p: p >> seq_bits
    pp = lax.broadcasted_iota(jnp.int32, (ROWS, ROWS), 0)
    qq = lax.broadcasted_iota(jnp.int32, (ROWS, ROWS), 1)
    same = seq_of(pp) == seq_of(qq)
    p_col = lax.broadcasted_iota(jnp.int32, (ROWS, 1), 0)
    p_row = lax.broadcasted_iota(jnp.int32, (1, ROWS), 1)
    col_ok = pos_of(p_col) < valid
    row_ok = pos_of(p_row) < valid
    causal = same & (qq <= pp) & row_ok
    upper = same & (pp <= qq)
    scale = ML_DQK ** -0.5
    one_seq = nb == 1

    def rows_of(ref, r, cols):
        if one_seq:
            return ref[r, :, cols]
        return ref[:, :, cols].reshape(ROWS, cols.stop - cols.start)

    def per_row(vec):
        return jnp.concatenate([vec] * L, axis=0)

    stores = []
    chains = [(r, h) for r in range(nblk) for h in range(ML_HEADS)]
    each = lambda fn: [fn(i, r, h) for i, (r, h) in enumerate(chains)]

    q = each(lambda i, r, h: rows_of(zqk_ref, r, slice(h * ML_DQK, (h + 1) * ML_DQK)))
    k = each(lambda i, r, h: rows_of(zqk_ref, r, slice((ML_HEADS + h) * ML_DQK, (ML_HEADS + h + 1) * ML_DQK)))
    v = each(lambda i, r, h: rows_of(zv_ref, r, slice(h * ML_DV, (h + 1) * ML_DV)))
    qc = [x.astype(bf16) for x in q]
    kc = [x.astype(bf16) for x in k]
    c0 = each(lambda i, r, h: cs_ref[r * nb:(r + 1) * nb, h])
    if one_seq:
        m0 = each(lambda i, r, h: ms_ref[r, h:h + 1, 0:1])
    else:
        m0 = each(lambda i, r, h: per_row(ms_ref[:, h, 0:1]))
        n0 = each(lambda i, r, h: per_row(ns_ref[:, h, :]))
    logi_col = each(lambda i, r, h: rows_of(zg_ref, r, slice(h, h + 1)))
    logi_row = each(lambda i, r, h: zgt_ref[r, h:h + 1, :])
    logf_col = each(lambda i, r, h: jnp.where(col_ok, rows_of(zg_ref, r, slice(ML_HEADS + h, ML_HEADS + h + 1)), 0.0))
    logf_row = each(lambda i, r, h: jnp.where(row_ok, zgt_ref[r, ML_HEADS + h:ML_HEADS + h + 1, :], 0.0))

    if one_seq:
        n_rows = each(lambda i, r, h: jnp.broadcast_to(ns_ref[r, h:h + 1, :], (N_PAD_ROWS, ML_DQK)).astype(bf16))
        qkc = each(lambda i, r, h: _dot_nt(
            qc[i], jnp.concatenate([kc[i], c0[i].reshape(ML_DV, ML_DQK).astype(bf16), n_rows[i]], axis=0)))
    else:
        qkc = each(lambda i, r, h: _dot_nt(
            qc[i], jnp.concatenate([kc[i], c0[i].reshape(nb * ML_DV, ML_DQK).astype(bf16)], axis=0)))
    b_col = each(lambda i, r, h: jnp.sum(jnp.where(causal, logf_row[i], 0.0), axis=1, keepdims=True))
    b_row = each(lambda i, r, h: jnp.sum(jnp.where(upper, logf_col[i], 0.0), axis=0, keepdims=True))
    if one_seq:
        b_last_col = each(lambda i, r, h: jnp.sum(logf_row[i], axis=1, keepdims=True))
        b_last_row = b_last_col
    else:
        b_last_col = each(lambda i, r, h: jnp.sum(jnp.where(same, logf_row[i], 0.0), axis=1, keepdims=True))
        b_last_row = each(lambda i, r, h: jnp.sum(jnp.where(same, logf_col[i], 0.0), axis=0, keepdims=True))
    d = each(lambda i, r, h: jnp.where(causal, b_col[i] - b_row[i] + logi_row[i], NEG))
    inter = each(lambda i, r, h: b_col[i] + m0[i])
    m_col = each(lambda i, r, h: jnp.maximum(inter[i], jnp.max(d[i], axis=1, keepdims=True)))
    w_intra = each(lambda i, r, h: jnp.exp(d[i] - m_col[i]) * scale)
    w_inter = each(lambda i, r, h: jnp.exp(inter[i] - m_col[i]) * scale)
    s = each(lambda i, r, h: qkc[i][:, :ROWS] * w_intra[i])
    if one_seq:
        q_mem = each(lambda i, r, h: qkc[i][:, ROWS:ROWS + ML_DV])
    else:
        seq_col = seq_of(p_col)
        q_mem = each(lambda i, r, h: functools.reduce(jnp.add, [
            jnp.where(seq_col == b, qkc[i][:, ROWS + b * ML_DV:ROWS + (b + 1) * ML_DV], 0.0) for b in range(nb)]))
    if one_seq:
        ones_blk = jnp.ones((ROWS, LANES), bf16)
        sv = each(lambda i, r, h: _dot(s[i].astype(bf16), jnp.concatenate([v[i].astype(bf16), ones_blk], axis=1)))
        num = each(lambda i, r, h: sv[i][:, :ML_DV] + w_inter[i] * q_mem[i])
        s_sum = each(lambda i, r, h: sv[i][:, ML_DV:ML_DV + 1])
        q_n = each(lambda i, r, h: qkc[i][:, ROWS + ML_DV:ROWS + ML_DV + 1])
    else:
        num = each(lambda i, r, h: _dot(s[i].astype(bf16), v[i].astype(bf16)) + w_inter[i] * q_mem[i])
        s_sum = each(lambda i, r, h: jnp.sum(s[i], axis=1, keepdims=True))
        q_n = each(lambda i, r, h: jnp.sum(q[i].astype(f32) * n0[i], axis=1, keepdims=True))
    den = each(lambda i, r, h: s_sum[i] + w_inter[i] * q_n[i])
    ones = jnp.ones((ML_DV, LANES), bf16)

    def head_norm(x, g):
        if not one_seq:
            return _rms(x, g)
        ssq = _dot((x * x).astype(bf16), ones)
        scale = lax.rsqrt(ssq * (1.0 / ML_DV) + EPS)
        return x * jnp.concatenate([scale] * (ML_DV // LANES), axis=1) * g

    hh = each(lambda i, r, h: head_norm(num[i] / jnp.maximum(jnp.abs(den[i]), jnp.exp(-m_col[i])), mlg_ref[h:h + 1, :]))
    def token_store(ref, r, cols, val):
        if one_seq:
            stores.append((ref, (r, slice(None), cols), val.astype(ref.dtype)))
        else:
            stores.append((ref, (slice(None), slice(None), cols),
                           val.astype(ref.dtype).reshape(L, nb, cols.stop - cols.start)))

    for i, (r, h) in enumerate(chains):
        cols = slice(h * ML_DV, (h + 1) * ML_DV)
        token_store(hml_ref, r, cols, rows_of(zog_ref, r, cols).astype(f32) * hh[i])

    wend_col = each(lambda i, r, h: jnp.where(col_ok, b_last_col[i] - b_col[i] + logi_col[i], NEG))
    wend_row = each(lambda i, r, h: jnp.where(row_ok, b_last_row[i] - b_row[i] + logi_row[i], NEG))
    if one_seq:
        m_new = each(lambda i, r, h: jnp.maximum(b_last_col[i] + m0[i], jnp.max(wend_row[i], axis=1, keepdims=True)))
    else:
        m_new = each(lambda i, r, h: jnp.maximum(
            b_last_col[i] + m0[i], jnp.max(jnp.where(same, wend_row[i], NEG), axis=1, keepdims=True)))
    decay = each(lambda i, r, h: jnp.exp(b_last_col[i] + m0[i] - m_new[i]))
    wend = each(lambda i, r, h: jnp.exp(wend_col[i] - m_new[i]))
    vw = each(lambda i, r, h: (v[i].astype(f32) * wend[i]).astype(bf16))
    kw = each(lambda i, r, h: k[i].astype(f32) * wend[i])
    if nb == 1:
        upd = each(lambda i, r, h: _dot_tn(vw[i], kc[i]))
    else:
        vw_t = each(lambda i, r, h: _dot_nt(eye_ref[...], vw[i]).astype(bf16))
        seq_of_lane = seq_of(lax.broadcasted_iota(jnp.int32, (ML_DV, ROWS), 1))
        upd = each(lambda i, r, h: _dot(jnp.concatenate(
            [jnp.where(seq_of_lane == b, vw_t[i], jnp.zeros_like(vw_t[i])) for b in range(nb)], axis=0), kc[i]))
    if one_seq:
        kw_sum = each(lambda i, r, h: jnp.sum(kw[i], axis=0, keepdims=True))
    else:
        kw_sum = each(lambda i, r, h: jnp.sum(kw[i].reshape(L, nb, ML_DQK), axis=0))
    for i, (r, h) in enumerate(chains):
        for b in range(nb):
            slot = r * nb + b
            dec = decay[i][b:b + 1, :]
            stores.append((c_ref, (slot, h), dec * c0[i][b] + upd[i][b * ML_DV:(b + 1) * ML_DV]))
            stores.append((n_ref, (slot, slice(h, h + 1), slice(None)),
                           dec * ns_ref[slot, h:h + 1, :] + kw_sum[i][b:b + 1, :]))
            stores.append((m_ref, (slot, slice(h, h + 1), slice(None)),
                           jnp.broadcast_to(m_new[i][b:b + 1, :], (1, LANES))))

    for r in range(nblk):
        gcols = [slice(g * GM_DG, (g + 1) * GM_DG) for g in range(GM_GROUPS)]
        gv = jnp.concatenate([rows_of(zgv_ref, r, c).astype(bf16) for c in gcols], axis=0)
        mixed = _dot(wsbd_ref[...], gv) + bcol_ref[...]
        for g, c in enumerate(gcols):
            token_store(hgm_ref, r, c, rows_of(zu_ref, r, c).astype(f32) * mixed[g * ROWS:(g + 1) * ROWS])

    for ref, idx, val in stores:
        ref[idx] = val


def _mixer_core(z, zg, zgt, c0, n0, m0, ml_g, ws_bd, b_col, *, nblk, nb, L, valid, h_dtype):
    n = z.shape[0]
    nbatch = c0.shape[0]
    groups = nbatch // nb
    nc = n // groups // ROWS
    if nb == 1:
        tok_shape = (groups, nc, ROWS)
        zgt4 = zgt.reshape(2 * ML_HEADS, groups, nc, ROWS).transpose(1, 2, 0, 3)
        tspec = lambda width, blk: pl.BlockSpec((nblk, None, ROWS, width), lambda b, c: (b, c, 0, blk))
    else:
        assert nc == 1 and nblk == 1
        tok_shape = (L, groups, nb)
        zgt4 = zgt.reshape(2 * ML_HEADS, L, groups, nb).transpose(2, 0, 1, 3).reshape(groups, 1, 2 * ML_HEADS, ROWS)
        tspec = lambda width, blk: pl.BlockSpec((L, None, nb, width), lambda b, c: (0, b, 0, blk))
    z4 = z.reshape(*tok_shape, N_ZB * D_MODEL)
    zg4 = zg.reshape(*tok_shape, LANES)
    zspec = lambda blk: tspec(D_MODEL, blk)
    full = lambda shape: pl.BlockSpec(shape, lambda b, c: (0,) * len(shape))
    state_specs = [
        pl.BlockSpec((nblk * nb, ML_HEADS, ML_DV, ML_DQK), lambda b, c: (b, 0, 0, 0)),
        pl.BlockSpec((nblk * nb, ML_HEADS, ML_DQK), lambda b, c: (b, 0, 0)),
        pl.BlockSpec((nblk * nb, ML_HEADS, LANES), lambda b, c: (b, 0, 0)),
    ]
    tok_spec = zspec(0)
    eye = jnp.eye(ML_DV, dtype=bf16)
    hml, hgm, c1, n1, m1 = pl.pallas_call(
        functools.partial(_mixer_core_kernel, nblk=nblk, nb=nb, L=L, valid=valid, single_chunk=(nc == 1)),
        grid=(groups // nblk, nc),
        in_specs=[zspec(ZB_QK), zspec(ZB_V), zspec(ZB_OG), zspec(ZB_U), zspec(ZB_GV),
                  tspec(LANES, 0),
                  pl.BlockSpec((nblk, None, 2 * ML_HEADS, ROWS), lambda b, c: (b, c, 0, 0)),
                  *state_specs,
                  full((ML_HEADS, ML_DV)), full((GM_GROUPS * ROWS, GM_GROUPS * ROWS)), full((GM_GROUPS * ROWS, 1)),
                  full((ML_DV, ML_DV))],
        out_specs=[tok_spec, tok_spec] + state_specs,
        out_shape=[
            jax.ShapeDtypeStruct((*tok_shape, D_MODEL), h_dtype),
            jax.ShapeDtypeStruct((*tok_shape, D_MODEL), h_dtype),
            jax.ShapeDtypeStruct((nbatch, ML_HEADS, ML_DV, ML_DQK), f32),
            jax.ShapeDtypeStruct((nbatch, ML_HEADS, ML_DQK), f32),
            jax.ShapeDtypeStruct((nbatch, ML_HEADS, LANES), f32),
        ],
        compiler_params=pltpu.CompilerParams(
            dimension_semantics=("parallel", "arbitrary"), vmem_limit_bytes=VMEM_LIMIT),
        name="mixer_core",
    )(z4, z4, z4, z4, z4, zg4, zgt4, c0, n0, m0, ml_g, ws_bd, b_col, eye)
    return hml.reshape(n, D_MODEL), hgm.reshape(n, D_MODEL), c1, n1, m1


def _gmlp_mixing(gm_ws, gm_bs, L):
    reps = ROWS // L
    tril = jnp.tril(jnp.ones((L, L), bool))
    blocks = [jnp.kron(jnp.where(tril, gm_ws[g, :L, :L], 0.0), jnp.eye(reps, dtype=f32)) for g in range(GM_GROUPS)]
    ws_bd = jax.scipy.linalg.block_diag(*blocks).astype(bf16)
    b_col = jnp.concatenate([jnp.repeat(gm_bs[g, :L], reps) for g in range(GM_GROUPS)])[:, None]
    return ws_bd, b_col


def _proj_router_kernel(hml_ref, hgm_ref, sga_ref, sgb_ref, x_ref, pa_ref, pb_ref, wo_ref, g_ref, wr_ref, br_ref,
                        x2_ref, xn_ref, comb_ref):
    tm = x_ref.shape[0]
    subs = [slice(r, r + PROJ_SUB) for r in range(0, tm, PROJ_SUB)]
    ab, xn = {}, {}

    def stage_branches(i):
        r = subs[i]
        ab[i] = (_dot(hml_ref[r, :].astype(bf16), pa_ref[...]), _dot(hgm_ref[r, :].astype(bf16), pb_ref[...]))

    def stage_merge(i):
        r = subs[i]
        a, b = ab.pop(i)
        merged = sga_ref[r, :].astype(f32) * a + sgb_ref[r, :].astype(f32) * b
        x2 = x_ref[r, :] + _dot(merged.astype(bf16), wo_ref[...])
        x2_ref[r, :] = x2
        xn[i] = _rms(x2, g_ref[...]).astype(bf16)
        xn_ref[r, :] = xn[i]

    def stage_route(i):
        lg = _dot(xn.pop(i), wr_ref[...]) + br_ref[...]
        lane = lax.broadcasted_iota(jnp.int32, lg.shape, 1).astype(f32)
        cmask = (lane >= MOE_EXPERTS) & (lane < MOE_EXPERTS + MOE_GROUPS)
        cl = jnp.where(cmask, lg, NEG)
        cmax = jnp.max(cl, axis=1, keepdims=True)
        p_grp = 1.0 / jnp.sum(jnp.where(cmask, jnp.exp(cl - cmax), 0.0), axis=1, keepdims=True)
        grp = jnp.min(jnp.where(cl == cmax, lane, 2.0 * LANES), axis=1, keepdims=True) - MOE_EXPERTS
        fmask = (lane >= grp * MOE_PER_GROUP) & (lane < (grp + 1.0) * MOE_PER_GROUP)
        fl = jnp.where(fmask, lg, NEG)
        v1 = jnp.max(fl, axis=1, keepdims=True)
        i1 = jnp.min(jnp.where(fl == v1, lane, 2.0 * LANES), axis=1, keepdims=True)
        fl2 = jnp.where(lane == i1, NEG, fl)
        v2 = jnp.max(fl2, axis=1, keepdims=True)
        i2 = jnp.min(jnp.where(fl2 == v2, lane, 2.0 * LANES), axis=1, keepdims=True)
        e2 = jnp.exp(v2 - v1)
        g1 = p_grp / (1.0 + e2)
        g2 = p_grp * e2 / (1.0 + e2)
        comb_ref[subs[i], :] = (jnp.where(lane == i1, g1, 0.0) + jnp.where(lane == i2, g2, 0.0)
                                + jnp.where(lane == GRP_LANE, grp, 0.0))

    n = len(subs)
    for step in range(n + 2):
        if step < n:
            stage_branches(step)
        if 0 <= step - 1 < n:
            stage_merge(step - 1)
        if 0 <= step - 2 < n:
            stage_route(step - 2)


def _proj_router(hml, hgm, z, x, n, pa, pb, wo, g, wr, br, tm):
    row = lambda blk=0: pl.BlockSpec((tm, D_MODEL), lambda i, blk=blk: (i, blk))
    const = lambda shape: pl.BlockSpec(shape, lambda i: (0, 0), pipeline_mode=pl.Buffered(1))
    wfull = const((D_MODEL, D_MODEL))
    return pl.pallas_call(
        _proj_router_kernel,
        grid=(n // tm,),
        in_specs=[row(), row(), row(ZB_GA), row(ZB_GB), row(), wfull, wfull, wfull,
                  const((1, D_MODEL)), const((D_MODEL, LANES)), const((1, LANES))],
        out_specs=[row(), row(), pl.BlockSpec((tm, LANES), lambda i: (i, 0))],
        out_shape=[jax.ShapeDtypeStruct((n, D_MODEL), f32),
                   jax.ShapeDtypeStruct((n, D_MODEL), bf16),
                   jax.ShapeDtypeStruct((n, LANES), f32)],
        compiler_params=pltpu.CompilerParams(
            dimension_semantics=("parallel",), vmem_limit_bytes=VMEM_LIMIT),
        name="proj_router",
    )(hml, hgm, z, z, x, pa, pb, wo, g, wr, br)


def _moe_final_kernel(xn_ref, comb_ref, x2_ref, ltri_ref, wg_ref, wu_ref, wd_ref, g_ref, y_ref,
                      chl_ref, gp_col_ref, gp_row_ref, pos_col_ref, pos_row_ref):
    grp = pl.program_id(1)
    gf = grp.astype(f32)
    tb = xn_ref.shape[0]

    @pl.when(grp == 0)
    def _():
        y_ref[...] = jnp.zeros_like(y_ref)
        comb = comb_ref[...]
        lane = lax.broadcasted_iota(jnp.int32, comb.shape, 1)
        gcol = comb[:, GRP_LANE:GRP_LANE + 1]
        onehot = jnp.where(lane.astype(f32) == gcol, 1.0, 0.0)
        before = _dot(ltri_ref[...], onehot.astype(bf16))
        pos = jnp.sum(onehot * before, axis=1, keepdims=True)
        gp = jnp.where(lane == 0, gcol, 0.0) + jnp.where(lane == 1, pos, 0.0)
        gp_col_ref[...] = gp
        gp_row_ref[...] = gp.T[:8, :]
        chi = comb.astype(bf16)
        chl_ref[:, :LANES] = chi
        chl_ref[:, LANES:] = (comb - chi.astype(f32)).astype(bf16)

    in_grp_col = gp_col_ref[:, 0:1] == gf
    pos_col_ref[...] = jnp.broadcast_to(jnp.where(in_grp_col, gp_col_ref[:, 1:2], -1.0), pos_col_ref.shape)
    pos_row_ref[...] = jnp.broadcast_to(jnp.where(gp_row_ref[0:1, :] == gf, gp_row_ref[1:2, :], -1.0), (8, tb))
    cnt = jnp.sum(jnp.where(in_grp_col, 1.0, 0.0)).astype(jnp.int32)

    def run_tiles(rows):
        def tile(t, carry):
            base = (t * rows).astype(f32)
            r_iota = lax.broadcasted_iota(jnp.int32, (rows, tb), 0).astype(f32)
            gather = jnp.where(pos_row_ref[0:1, :] - base == r_iota, 1.0, 0.0).astype(bf16)
            x = _dot(gather, xn_ref[...]).astype(bf16)
            c2 = _dot(gather, chl_ref[...])
            c = c2[:, :LANES] + c2[:, LANES:]
            lane = lax.broadcasted_iota(jnp.int32, c.shape, 1)
            hid = []
            for e in range(MOE_PER_GROUP):
                ce = jnp.sum(jnp.where(lane == grp * MOE_PER_GROUP + e, c, 0.0), axis=1, keepdims=True)
                a = _dot(x, wg_ref[e])
                u = _dot(x, wu_ref[e])
                hid.append((a * _sigmoid(a) * u * ce).astype(bf16))
            y = _dot(jnp.concatenate(hid, axis=1), wd_ref[...]).astype(bf16)
            l_iota = lax.broadcasted_iota(jnp.int32, (tb, rows), 1).astype(f32)
            scatter = jnp.where(pos_col_ref[:, :rows] - base == l_iota, 1.0, 0.0).astype(bf16)
            y_ref[...] += _dot(scatter, y)
            return carry

        lax.fori_loop(0, (cnt + rows - 1) // rows, tile, 0)

    n_tiles = [(cnt + rows - 1) // rows for rows in MOE_TILES]
    for i, rows in enumerate(MOE_TILES):
        pick = n_tiles[i] == n_tiles[-1]
        for j in range(i):
            pick = pick & (n_tiles[j] != n_tiles[-1])
        pl.when(pick)(functools.partial(run_tiles, rows))

    @pl.when(grp == MOE_GROUPS - 1)
    def _():
        y_ref[...] = _rms(x2_ref[...] + y_ref[...], g_ref[...])


def _moe_final(xn, comb, x2, wg, wu, wd, g, tb):
    n = x2.shape[0]
    ltri = jnp.tril(jnp.ones((tb, tb), bf16), -1)
    return pl.pallas_call(
        _moe_final_kernel,
        grid=(n // tb, MOE_GROUPS),
        in_specs=[pl.BlockSpec((tb, D_MODEL), lambda i, j: (i, 0)),
                  pl.BlockSpec((tb, LANES), lambda i, j: (i, 0)),
                  pl.BlockSpec((tb, D_MODEL), lambda i, j: (i, 0)),
                  pl.BlockSpec((tb, tb), lambda i, j: (0, 0)),
                  pl.BlockSpec((MOE_PER_GROUP, D_MODEL, MOE_HIDDEN), lambda i, j: (j, 0, 0)),
                  pl.BlockSpec((MOE_PER_GROUP, D_MODEL, MOE_HIDDEN), lambda i, j: (j, 0, 0)),
                  pl.BlockSpec((None, MOE_PER_GROUP * MOE_HIDDEN, D_MODEL), lambda i, j: (j, 0, 0)),
                  pl.BlockSpec((1, D_MODEL), lambda i, j: (0, 0))],
        out_specs=pl.BlockSpec((tb, D_MODEL), lambda i, j: (i, 0)),
        out_shape=jax.ShapeDtypeStruct((n, D_MODEL), f32),
        scratch_shapes=[pltpu.VMEM((tb, 2 * LANES), bf16),
                        pltpu.VMEM((tb, LANES), f32), pltpu.VMEM((8, tb), f32),
                        pltpu.VMEM((tb, 2 * LANES), f32), pltpu.VMEM((8, tb), f32)],
        compiler_params=pltpu.CompilerParams(
            dimension_semantics=("parallel", "arbitrary"), vmem_limit_bytes=VMEM_LIMIT),
        name="moe_final",
    )(xn, comb, x2, ltri, wg, wu, wd.reshape(MOE_GROUPS, MOE_PER_GROUP * MOE_HIDDEN, D_MODEL), g)


def _layer(x3, state, lw, *, prompt):
    nbatch, t_real, _ = x3.shape
    if prompt:
        t = t_real
        L, valid, z_dtype, tm, nblk = ML_CHUNK, ML_CHUNK, bf16, 512, 4
        x = x3.reshape(nbatch * t, D_MODEL)
    else:
        t = SAMPLE_PAD_T
        L, valid, z_dtype, tm, nblk = t, t_real, f32, 256, 1
        x = jnp.pad(x3.transpose(1, 0, 2), ((0, t - t_real), (0, 0), (0, 0))).reshape(t * nbatch, D_MODEL)
    n = nbatch * t
    nb = ROWS // L

    z, zg, zgt = _in_proj(x, lw["norm_mix_g"], lw["w_a"], lw["w_b"], lw["w_gate"], lw["gate_bias"],
                          lw["gm_norm_g"], tm, z_dtype)
    c0, n0, m0 = state
    m0 = jnp.broadcast_to(m0[:, :, None], (nbatch, ML_HEADS, LANES))
    ws_bd, b_col = _gmlp_mixing(lw["gm_ws"], lw["gm_bs"], L)
    hml, hgm, c1, n1, m1 = _mixer_core(z, zg, zgt, c0, n0, m0, lw["ml_norm_g"],
                                       ws_bd, b_col, nblk=nblk, nb=nb, L=L, valid=valid, h_dtype=z_dtype)
    n_real = nbatch * t_real
    x2, xn, comb = _proj_router(hml, hgm, z, x, n_real, lw["p_a"], lw["p_b"], lw["w_out"],
                                lw["norm_ffn_g"], lw["w_router"], lw["b_router"], min(n_real, PROJ_TILE))
    y = _moe_final(xn, comb, x2, lw["e_wg"], lw["e_wu"], lw["e_wd"], lw["out_g"], min(n_real, MOE_BLOCK))
    gv = z[:n_real, ZB_GV * D_MODEL:(ZB_GV + 1) * D_MODEL]
    if prompt:
        return y.reshape(nbatch, t_real, D_MODEL), (c1, n1, m1[:, :, 0]), None
    unrows = lambda a: a.reshape(t_real, nbatch, D_MODEL).transpose(1, 0, 2)
    return unrows(y), (c1, n1, m1[:, :, 0]), unrows(gv)


def kernel(x_prompt, x_sample, state_mlstm_C, state_mlstm_n, state_mlstm_m, norm_mix_g, w_in, ml_b_i, ml_b_f, ml_norm_g, gm_norm_g, gm_ws, gm_bs, p_a, p_b, w_out, norm_ffn_g, rc_w, rc_b, rf_w, rf_b, e_wg, e_wu, e_wd, final_norm_g):
    depth = w_in.shape[0]
    assert depth == 1, "the final norm is fused into the last layer's MoE kernel; only depth 1 is wired up"
    nbp = x_prompt.shape[0]
    nbs, ts, _ = x_sample.shape

    def layer_weights(l):
        w = w_in[l]
        gates = w[:, GATE_OFF:GATE_OFF + 2 * ML_HEADS]
        return dict(
            norm_mix_g=norm_mix_g[l][None, :],
            w_a=w[:, :GATE_OFF].astype(bf16), w_b=w[:, GATE_OFF + 2 * ML_HEADS:].astype(bf16),
            w_gate=jnp.pad(gates, ((0, 0), (0, LANES - 2 * ML_HEADS))).astype(bf16),
            gate_bias=jnp.pad(jnp.concatenate([ml_b_i[l], ml_b_f[l]]), (0, LANES - 2 * ML_HEADS))[None, :],
            ml_norm_g=ml_norm_g[l],
            gm_norm_g=gm_norm_g[l].reshape(1, GM_GROUPS * GM_DG),
            gm_ws=gm_ws[l], gm_bs=gm_bs[l],
            p_a=p_a[l].astype(bf16), p_b=p_b[l].astype(bf16), w_out=w_out[l].astype(bf16),
            norm_ffn_g=norm_ffn_g[l][None, :],
            w_router=jnp.pad(
                jnp.concatenate([rf_w[l].transpose(1, 0, 2).reshape(D_MODEL, MOE_EXPERTS), rc_w[l]], axis=1),
                ((0, 0), (0, LANES - MOE_EXPERTS - MOE_GROUPS))).astype(bf16),
            b_router=jnp.pad(jnp.concatenate([rf_b[l].reshape(-1), rc_b[l]]),
                             (0, LANES - MOE_EXPERTS - MOE_GROUPS))[None, :],
            e_wg=e_wg[l].astype(bf16), e_wu=e_wu[l].astype(bf16), e_wd=e_wd[l].astype(bf16),
            out_g=final_norm_g[None, :],
        )

    lw = layer_weights(0)

    zero_state = (jnp.zeros((nbp, ML_HEADS, ML_DV, ML_DQK), f32), jnp.zeros((nbp, ML_HEADS, ML_DQK), f32),
                  jnp.zeros((nbp, ML_HEADS), f32))
    y_p, (c_p, n_p, m_p), _ = _layer(x_prompt, zero_state, lw, prompt=True)

    y_s, (c_s, n_s, m_s), v_s = _layer(x_sample, (state_mlstm_C[0], state_mlstm_n[0], state_mlstm_m[0]),
                                       lw, prompt=False)
    v_s = v_s.reshape(nbs, ts, GM_GROUPS, GM_DG)
    return (y_p, y_s, c_p[None], n_p[None], m_p[None], c_s[None], n_s[None], m_s[None], v_s[None])
```

```python
import functools

import numpy as np
import jax
import jax.numpy as jnp
from jax import lax
from jax.experimental import pallas as pl
from jax.experimental.pallas import tpu as pltpu

D_MODEL = 1024
ML_HEADS = 4
ML_DQK = 128
ML_DV = 256
ML_CHUNK = 128
GM_GROUPS = 4
GM_DG = 256
MOE_GROUPS = 4
MOE_PER_GROUP = 8
MOE_EXPERTS = MOE_GROUPS * MOE_PER_GROUP
MOE_HIDDEN = 256
EPS = 1e-6

LANES = 128
SAMPLE_PAD_T = 8
ROWS = 128
N_PAD_ROWS = 16
NEG = -1e30
MOE_TILES = (128, 144, 160)
MOE_BLOCK = 1024
IN_TILE = 512
IN_SUB = 256
MIXER_BLOCKS = 4
PROJ_TILE = 1024
PROJ_SUB = 256
GRP_LANE = MOE_EXPERTS
VMEM_LIMIT = 56 * 1024 * 1024

ZB_QK, ZB_V, ZB_OG, ZB_U, ZB_GV, ZB_GA, ZB_GB = range(7)
N_ZB = 7
N_ZB_A = 3
GATE_OFF = 2 * ML_HEADS * ML_DQK + 2 * ML_HEADS * ML_DV

f32 = jnp.float32
bf16 = jnp.bfloat16


def _sigmoid(x):
    return 0.5 * jnp.tanh(0.5 * x) + 0.5


def _log_sigmoid(x):
    return jnp.minimum(x, 0.0) - jnp.log1p(jnp.exp(-jnp.abs(x)))


def _gelu_tanh(x):
    return 0.5 * x * (1.0 + jnp.tanh(np.sqrt(2.0 / np.pi) * (x + 0.044715 * (x * x * x))))


def _rms(x, g):
    return x * lax.rsqrt(jnp.mean(x * x, axis=-1, keepdims=True) + EPS) * g


def _dot(a, b):
    return jnp.dot(a, b, preferred_element_type=f32)


def _dot_nt(a, b):
    return lax.dot_general(a, b, (((1,), (1,)), ((), ())), preferred_element_type=f32)


def _dot_tn(a, b):
    return lax.dot_general(a, b, (((0,), (0,)), ((), ())), preferred_element_type=f32)


def _in_proj_kernel(x_ref, g_ref, wa_ref, wb_ref, wg_ref, gb_ref, gmg_ref, z_ref, zg_ref, zgt_ref):
    tm = x_ref.shape[0]
    subs = [slice(r, r + IN_SUB) for r in range(0, tm, IN_SUB)]
    xn = [None] * len(subs)

    def prepare(s):
        rows = subs[s]
        xn[s] = _rms(x_ref[rows, :], g_ref[...]).astype(bf16)
        zg = _dot(xn[s], wg_ref[...]) + gb_ref[...]
        lane = lax.broadcasted_iota(jnp.int32, zg.shape, 1)
        zg = jnp.where((lane >= ML_HEADS) & (lane < 2 * ML_HEADS), _log_sigmoid(zg), zg)
        zg_ref[rows, :] = zg
        zgt_ref[:, rows] = zg.T[:2 * ML_HEADS, :]

    def gelu_group_rms(a, g):
        return _rms(_gelu_tanh(a), gmg_ref[:, g * GM_DG:(g + 1) * GM_DG])

    act = {ZB_QK: None, ZB_V: None, ZB_OG: _sigmoid, ZB_GA: _sigmoid, ZB_GB: _sigmoid,
           ZB_U: _gelu_tanh, ZB_GV: gelu_group_rms}
    per_blk = D_MODEL // GM_DG

    def emit(blk, s):
        for g in range(per_blk):
            cols = slice(blk * D_MODEL + g * GM_DG, blk * D_MODEL + (g + 1) * GM_DG)
            if blk < N_ZB_A:
                a = _dot(xn[s], wa_ref[:, cols])
            else:
                a = _dot(xn[s], wb_ref[:, cols.start - N_ZB_A * D_MODEL:cols.stop - N_ZB_A * D_MODEL])
            if blk == ZB_GV:
                a = gelu_group_rms(a, g)
            elif act[blk] is not None:
                a = act[blk](a)
            z_ref[subs[s], cols] = a.astype(z_ref.dtype)

    for s in range(len(subs)):
        prepare(s)
        emit(0, s)
    for blk in range(1, N_ZB):
        for s in range(len(subs)):
            emit(blk, s)


def _in_proj(x, g, w_a, w_b, w_gate, gate_bias, gm_g, tm, z_dtype):
    n = x.shape[0]
    const = lambda shape: pl.BlockSpec(shape, lambda i: (0, 0), pipeline_mode=pl.Buffered(1))
    return pl.pallas_call(
        _in_proj_kernel,
        grid=(n // tm,),
        in_specs=[
            pl.BlockSpec((tm, D_MODEL), lambda i: (i, 0)),
            const((1, D_MODEL)),
            const((D_MODEL, N_ZB_A * D_MODEL)),
            const((D_MODEL, (N_ZB - N_ZB_A) * D_MODEL)),
            const((D_MODEL, LANES)),
            const((1, LANES)),
            const((1, D_MODEL)),
        ],
        out_specs=[
            pl.BlockSpec((tm, N_ZB * D_MODEL), lambda i: (i, 0)),
            pl.BlockSpec((tm, LANES), lambda i: (i, 0)),
            pl.BlockSpec((2 * ML_HEADS, tm), lambda i: (0, i)),
        ],
        out_shape=[
            jax.ShapeDtypeStruct((n, N_ZB * D_MODEL), z_dtype),
            jax.ShapeDtypeStruct((n, LANES), f32),
            jax.ShapeDtypeStruct((2 * ML_HEADS, n), f32),
        ],
        compiler_params=pltpu.CompilerParams(
            dimension_semantics=("parallel",), vmem_limit_bytes=VMEM_LIMIT),
        name="in_proj",
    )(x, g, w_a, w_b, w_gate, gate_bias, gm_g)


def _mixer_core_kernel(zqk_ref, zv_ref, zog_ref, zu_ref, zgv_ref, zg_ref, zgt_ref,
                       c0_ref, n0_ref, m0_ref, mlg_ref, wsbd_ref, bcol_ref, eye_ref,
                       hml_ref, hgm_ref, c_ref, n_ref, m_ref, *, nblk, nb, L, valid, single_chunk):
    if single_chunk:
        cs_ref, ns_ref, ms_ref = c0_ref, n0_ref, m0_ref
    else:
        cs_ref, ns_ref, ms_ref = c_ref, n_ref, m_ref

        @pl.when(pl.program_id(1) == 0)
        def _():
            c_ref[...] = c0_ref[...]
            n_ref[...] = n0_ref[...]
            m_ref[...] = m0_ref[...]

    seq_bits = nb.bit_length() - 1
    seq_of = lambda p: p & (nb - 1)
    pos_of = lambda p: p >> seq_bits
    pp = lax.broadcasted_iota(jnp.int32, (ROWS, ROWS), 0)
    qq = lax.broadcasted_iota(jnp.int32, (ROWS, ROWS), 1)
    same = seq_of(pp) == seq_of(qq)
    p_col = lax.broadcasted_iota(jnp.int32, (ROWS, 1), 0)
    p_row = lax.broadcasted_iota(jnp.int32, (1, ROWS), 1)
    col_ok = pos_of(p_col) < valid
    row_ok = pos_of(p_row) < valid
    causal = same & (qq <= pp) & row_ok
    upper = same & (pp <= qq)
    scale = ML_DQK ** -0.5
    one_seq = nb == 1

    def rows_of(ref, r, cols):
        if one_seq:
            return ref[r, :, cols]
        return ref[:, :, cols].reshape(ROWS, cols.stop - cols.start)

    def per_row(vec):
        return jnp.concatenate([vec] * L, axis=0)

    stores = []
    chains = [(r, h) for r in range(nblk) for h in range(ML_HEADS)]
    each = lambda fn: [fn(i, r, h) for i, (r, h) in enumerate(chains)]

    q = each(lambda i, r, h: rows_of(zqk_ref, r, slice(h * ML_DQK, (h + 1) * ML_DQK)))
    k = each(lambda i, r, h: rows_of(zqk_ref, r, slice((ML_HEADS + h) * ML_DQK, (ML_HEADS + h + 1) * ML_DQK)))
    v = each(lambda i, r, h: rows_of(zv_ref, r, slice(h * ML_DV, (h + 1) * ML_DV)))
    qc = [x.astype(bf16) for x in q]
    kc = [x.astype(bf16) for x in k]
    c0 = each(lambda i, r, h: cs_ref[r * nb:(r + 1) * nb, h])
    if one_seq:
        m0 = each(lambda i, r, h: ms_ref[r, h:h + 1, 0:1])
    else:
        m0 = each(lambda i, r, h: per_row(ms_ref[:, h, 0:1]))
        n0 = each(lambda i, r, h: per_row(ns_ref[:, h, :]))
    logi_col = each(lambda i, r, h: rows_of(zg_ref, r, slice(h, h + 1)))
    logi_row = each(lambda i, r, h: zgt_ref[r, h:h + 1, :])
    logf_col = each(lambda i, r, h: jnp.where(col_ok, rows_of(zg_ref, r, slice(ML_HEADS + h, ML_HEADS + h + 1)), 0.0))
    logf_row = each(lambda i, r, h: jnp.where(row_ok, zgt_ref[r, ML_HEADS + h:ML_HEADS + h + 1, :], 0.0))

    if one_seq:
        n_rows = each(lambda i, r, h: jnp.broadcast_to(ns_ref[r, h:h + 1, :], (N_PAD_ROWS, ML_DQK)).astype(bf16))
        qkc = each(lambda i, r, h: _dot_nt(
            qc[i], jnp.concatenate([kc[i], c0[i].reshape(ML_DV, ML_DQK).astype(bf16), n_rows[i]], axis=0)))
    else:
        qkc = each(lambda i, r, h: _dot_nt(
            qc[i], jnp.concatenate([kc[i], c0[i].reshape(nb * ML_DV, ML_DQK).astype(bf16)], axis=0)))
    b_col = each(lambda i, r, h: jnp.sum(jnp.where(causal, logf_row[i], 0.0), axis=1, keepdims=True))
    b_row = each(lambda i, r, h: jnp.sum(jnp.where(upper, logf_col[i], 0.0), axis=0, keepdims=True))
    if one_seq:
        b_last_col = each(lambda i, r, h: jnp.sum(logf_row[i], axis=1, keepdims=True))
        b_last_row = b_last_col
    else:
        b_last_col = each(lambda i, r, h: jnp.sum(jnp.where(same, logf_row[i], 0.0), axis=1, keepdims=True))
        b_last_row = each(lambda i, r, h: jnp.sum(jnp.where(same, logf_col[i], 0.0), axis=0, keepdims=True))
    d = each(lambda i, r, h: jnp.where(causal, b_col[i] - b_row[i] + logi_row[i], NEG))
    inter = each(lambda i, r, h: b_col[i] + m0[i])
    m_col = each(lambda i, r, h: jnp.maximum(inter[i], jnp.max(d[i], axis=1, keepdims=True)))
    w_intra = each(lambda i, r, h: jnp.exp(d[i] - m_col[i]) * scale)
    w_inter = each(lambda i, r, h: jnp.exp(inter[i] - m_col[i]) * scale)
    s = each(lambda i, r, h: qkc[i][:, :ROWS] * w_intra[i])
    if one_seq:
        q_mem = each(lambda i, r, h: qkc[i][:, ROWS:ROWS + ML_DV])
    else:
        seq_col = seq_of(p_col)
        q_mem = each(lambda i, r, h: functools.reduce(jnp.add, [
            jnp.where(seq_col == b, qkc[i][:, ROWS + b * ML_DV:ROWS + (b + 1) * ML_DV], 0.0) for b in range(nb)]))
    if one_seq:
        ones_blk = jnp.ones((ROWS, LANES), bf16)
        sv = each(lambda i, r, h: _dot(s[i].astype(bf16), jnp.concatenate([v[i].astype(bf16), ones_blk], axis=1)))
        num = each(lambda i, r, h: sv[i][:, :ML_DV] + w_inter[i] * q_mem[i])
        s_sum = each(lambda i, r, h: sv[i][:, ML_DV:ML_DV + 1])
        q_n = each(lambda i, r, h: qkc[i][:, ROWS + ML_DV:ROWS + ML_DV + 1])
    else:
        num = each(lambda i, r, h: _dot(s[i].astype(bf16), v[i].astype(bf16)) + w_inter[i] * q_mem[i])
        s_sum = each(lambda i, r, h: jnp.sum(s[i], axis=1, keepdims=True))
        q_n = each(lambda i, r, h: jnp.sum(q[i].astype(f32) * n0[i], axis=1, keepdims=True))
    den = each(lambda i, r, h: s_sum[i] + w_inter[i] * q_n[i])
    ones = jnp.ones((ML_DV, LANES), bf16)

    def head_norm(x, g):
        if not one_seq:
            return _rms(x, g)
        ssq = _dot((x * x).astype(bf16), ones)
        scale = lax.rsqrt(ssq * (1.0 / ML_DV) + EPS)
        return x * jnp.concatenate([scale] * (ML_DV // LANES), axis=1) * g

    hh = each(lambda i, r, h: head_norm(num[i] / jnp.maximum(jnp.abs(den[i]), jnp.exp(-m_col[i])), mlg_ref[h:h + 1, :]))
    def token_store(ref, r, cols, val):
        if one_seq:
            stores.append((ref, (r, slice(None), cols), val.astype(ref.dtype)))
        else:
            stores.append((ref, (slice(None), slice(None), cols),
                           val.astype(ref.dtype).reshape(L, nb, cols.stop - cols.start)))

    for i, (r, h) in enumerate(chains):
        cols = slice(h * ML_DV, (h + 1) * ML_DV)
        token_store(hml_ref, r, cols, rows_of(zog_ref, r, cols).astype(f32) * hh[i])

    wend_col = each(lambda i, r, h: jnp.where(col_ok, b_last_col[i] - b_col[i] + logi_col[i], NEG))
    wend_row = each(lambda i, r, h: jnp.where(row_ok, b_last_row[i] - b_row[i] + logi_row[i], NEG))
    if one_seq:
        m_new = each(lambda i, r, h: jnp.maximum(b_last_col[i] + m0[i], jnp.max(wend_row[i], axis=1, keepdims=True)))
    else:
        m_new = each(lambda i, r, h: jnp.maximum(
            b_last_col[i] + m0[i], jnp.max(jnp.where(same, wend_row[i], NEG), axis=1, keepdims=True)))
    decay = each(lambda i, r, h: jnp.exp(b_last_col[i] + m0[i] - m_new[i]))
    wend = each(lambda i, r, h: jnp.exp(wend_col[i] - m_new[i]))
    vw = each(lambda i, r, h: (v[i].astype(f32) * wend[i]).astype(bf16))
    kw = each(lambda i, r, h: k[i].astype(f32) * wend[i])
    if nb == 1:
        upd = each(lambda i, r, h: _dot_tn(vw[i], kc[i]))
    else:
        vw_t = each(lambda i, r, h: _dot_nt(eye_ref[...], vw[i]).astype(bf16))
        seq_of_lane = seq_of(lax.broadcasted_iota(jnp.int32, (ML_DV, ROWS), 1))
        upd = each(lambda i, r, h: _dot(jnp.concatenate(
            [jnp.where(seq_of_lane == b, vw_t[i], jnp.zeros_like(vw_t[i])) for b in range(nb)], axis=0), kc[i]))
    if one_seq:
        kw_sum = each(lambda i, r, h: jnp.sum(kw[i], axis=0, keepdims=True))
    else:
        kw_sum = each(lambda i, r, h: jnp.sum(kw[i].reshape(L, nb, ML_DQK), axis=0))
    for i, (r, h) in enumerate(chains):
        for b in range(nb):
            slot = r * nb + b
            dec = decay[i][b:b + 1, :]
            stores.append((c_ref, (slot, h), dec * c0[i][b] + upd[i][b * ML_DV:(b + 1) * ML_DV]))
            stores.append((n_ref, (slot, slice(h, h + 1), slice(None)),
                           dec * ns_ref[slot, h:h + 1, :] + kw_sum[i][b:b + 1, :]))
            stores.append((m_ref, (slot, slice(h, h + 1), slice(None)),
                           jnp.broadcast_to(m_new[i][b:b + 1, :], (1, LANES))))

    for r in range(nblk):
        gcols = [slice(g * GM_DG, (g + 1) * GM_DG) for g in range(GM_GROUPS)]
        gv = jnp.concatenate([rows_of(zgv_ref, r, c).astype(bf16) for c in gcols], axis=0)
        mixed = _dot(wsbd_ref[...], gv) + bcol_ref[...]
        for g, c in enumerate(gcols):
            token_store(hgm_ref, r, c, rows_of(zu_ref, r, c).astype(f32) * mixed[g * ROWS:(g + 1) * ROWS])

    for ref, idx, val in stores:
        ref[idx] = val


def _mixer_core(z, zg, zgt, c0, n0, m0, ml_g, ws_bd, b_col, *, nblk, nb, L, valid, h_dtype):
    n = z.shape[0]
    nbatch = c0.shape[0]
    groups = nbatch // nb
    nc = n // groups // ROWS
    if nb == 1:
        tok_shape = (groups, nc, ROWS)
        zgt4 = zgt.reshape(2 * ML_HEADS, groups, nc, ROWS).transpose(1, 2, 0, 3)
        tspec = lambda width, blk: pl.BlockSpec((nblk, None, ROWS, width), lambda b, c: (b, c, 0, blk))
    else:
        assert nc == 1 and nblk == 1
        tok_shape = (L, groups, nb)
        zgt4 = zgt.reshape(2 * ML_HEADS, L, groups, nb).transpose(2, 0, 1, 3).reshape(groups, 1, 2 * ML_HEADS, ROWS)
        tspec = lambda width, blk: pl.BlockSpec((L, None, nb, width), lambda b, c: (0, b, 0, blk))
    z4 = z.reshape(*tok_shape, N_ZB * D_MODEL)
    zg4 = zg.reshape(*tok_shape, LANES)
    zspec = lambda blk: tspec(D_MODEL, blk)
    full = lambda shape: pl.BlockSpec(shape, lambda b, c: (0,) * len(shape))
    state_specs = [
        pl.BlockSpec((nblk * nb, ML_HEADS, ML_DV, ML_DQK), lambda b, c: (b, 0, 0, 0)),
        pl.BlockSpec((nblk * nb, ML_HEADS, ML_DQK), lambda b, c: (b, 0, 0)),
        pl.BlockSpec((nblk * nb, ML_HEADS, LANES), lambda b, c: (b, 0, 0)),
    ]
    tok_spec = zspec(0)
    eye = jnp.eye(ML_DV, dtype=bf16)
    hml, hgm, c1, n1, m1 = pl.pallas_call(
        functools.partial(_mixer_core_kernel, nblk=nblk, nb=nb, L=L, valid=valid, single_chunk=(nc == 1)),
        grid=(groups // nblk, nc),
        in_specs=[zspec(ZB_QK), zspec(ZB_V), zspec(ZB_OG), zspec(ZB_U), zspec(ZB_GV),
                  tspec(LANES, 0),
                  pl.BlockSpec((nblk, None, 2 * ML_HEADS, ROWS), lambda b, c: (b, c, 0, 0)),
                  *state_specs,
                  full((ML_HEADS, ML_DV)), full((GM_GROUPS * ROWS, GM_GROUPS * ROWS)), full((GM_GROUPS * ROWS, 1)),
                  full((ML_DV, ML_DV))],
        out_specs=[tok_spec, tok_spec] + state_specs,
        out_shape=[
            jax.ShapeDtypeStruct((*tok_shape, D_MODEL), h_dtype),
            jax.ShapeDtypeStruct((*tok_shape, D_MODEL), h_dtype),
            jax.ShapeDtypeStruct((nbatch, ML_HEADS, ML_DV, ML_DQK), f32),
            jax.ShapeDtypeStruct((nbatch, ML_HEADS, ML_DQK), f32),
            jax.ShapeDtypeStruct((nbatch, ML_HEADS, LANES), f32),
        ],
        compiler_params=pltpu.CompilerParams(
            dimension_semantics=("parallel", "arbitrary"), vmem_limit_bytes=VMEM_LIMIT),
        name="mixer_core",
    )(z4, z4, z4, z4, z4, zg4, zgt4, c0, n0, m0, ml_g, ws_bd, b_col, eye)
    return hml.reshape(n, D_MODEL), hgm.reshape(n, D_MODEL), c1, n1, m1


def _gmlp_mixing(gm_ws, gm_bs, L):
    reps = ROWS // L
    tril = jnp.tril(jnp.ones((L, L), bool))
    blocks = [jnp.kron(jnp.where(tril, gm_ws[g, :L, :L], 0.0), jnp.eye(reps, dtype=f32)) for g in range(GM_GROUPS)]
    ws_bd = jax.scipy.linalg.block_diag(*blocks).astype(bf16)
    b_col = jnp.concatenate([jnp.repeat(gm_bs[g, :L], reps) for g in range(GM_GROUPS)])[:, None]
    return ws_bd, b_col


def _proj_router_kernel(hml_ref, hgm_ref, sga_ref, sgb_ref, x_ref, pa_ref, pb_ref, wo_ref, g_ref, wr_ref, br_ref,
                        x2_ref, xn_ref, comb_ref):
    tm = x_ref.shape[0]
    subs = [slice(r, r + PROJ_SUB) for r in range(0, tm, PROJ_SUB)]
    ab, xn = {}, {}

    def stage_branches(i):
        r = subs[i]
        ab[i] = (_dot(hml_ref[r, :].astype(bf16), pa_ref[...]), _dot(hgm_ref[r, :].astype(bf16), pb_ref[...]))

    def stage_merge(i):
        r = subs[i]
        a, b = ab.pop(i)
        merged = sga_ref[r, :].astype(f32) * a + sgb_ref[r, :].astype(f32) * b
        x2 = x_ref[r, :] + _dot(merged.astype(bf16), wo_ref[...])
        x2_ref[r, :] = x2
        xn[i] = _rms(x2, g_ref[...]).astype(bf16)
        xn_ref[r, :] = xn[i]

    def stage_route(i):
        lg = _dot(xn.pop(i), wr_ref[...]) + br_ref[...]
        lane = lax.broadcasted_iota(jnp.int32, lg.shape, 1).astype(f32)
        cmask = (lane >= MOE_EXPERTS) & (lane < MOE_EXPERTS + MOE_GROUPS)
        cl = jnp.where(cmask, lg, NEG)
        cmax = jnp.max(cl, axis=1, keepdims=True)
        p_grp = 1.0 / jnp.sum(jnp.where(cmask, jnp.exp(cl - cmax), 0.0), axis=1, keepdims=True)
        grp = jnp.min(jnp.where(cl == cmax, lane, 2.0 * LANES), axis=1, keepdims=True) - MOE_EXPERTS
        fmask = (lane >= grp * MOE_PER_GROUP) & (lane < (grp + 1.0) * MOE_PER_GROUP)
        fl = jnp.where(fmask, lg, NEG)
        v1 = jnp.max(fl, axis=1, keepdims=True)
        i1 = jnp.min(jnp.where(fl == v1, lane, 2.0 * LANES), axis=1, keepdims=True)
        fl2 = jnp.where(lane == i1, NEG, fl)
        v2 = jnp.max(fl2, axis=1, keepdims=True)
        i2 = jnp.min(jnp.where(fl2 == v2, lane, 2.0 * LANES), axis=1, keepdims=True)
        e2 = jnp.exp(v2 - v1)
        g1 = p_grp / (1.0 + e2)
        g2 = p_grp * e2 / (1.0 + e2)
        comb_ref[subs[i], :] = (jnp.where(lane == i1, g1, 0.0) + jnp.where(lane == i2, g2, 0.0)
                                + jnp.where(lane == GRP_LANE, grp, 0.0))

    n = len(subs)
    for step in range(n + 2):
        if step < n:
            stage_branches(step)
        if 0 <= step - 1 < n:
            stage_merge(step - 1)
        if 0 <= step - 2 < n:
            stage_route(step - 2)


def _proj_router(hml, hgm, z, x, n, pa, pb, wo, g, wr, br, tm):
    row = lambda blk=0: pl.BlockSpec((tm, D_MODEL), lambda i, blk=blk: (i, blk))
    const = lambda shape: pl.BlockSpec(shape, lambda i: (0, 0), pipeline_mode=pl.Buffered(1))
    wfull = const((D_MODEL, D_MODEL))
    return pl.pallas_call(
        _proj_router_kernel,
        grid=(n // tm,),
        in_specs=[row(), row(), row(ZB_GA), row(ZB_GB), row(), wfull, wfull, wfull,
                  const((1, D_MODEL)), const((D_MODEL, LANES)), const((1, LANES))],
        out_specs=[row(), row(), pl.BlockSpec((tm, LANES), lambda i: (i, 0))],
        out_shape=[jax.ShapeDtypeStruct((n, D_MODEL), f32),
                   jax.ShapeDtypeStruct((n, D_MODEL), bf16),
                   jax.ShapeDtypeStruct((n, LANES), f32)],
        compiler_params=pltpu.CompilerParams(
            dimension_semantics=("parallel",), vmem_limit_bytes=VMEM_LIMIT),
        name="proj_router",
    )(hml, hgm, z, z, x, pa, pb, wo, g, wr, br)


def _moe_final_kernel(xn_ref, comb_ref, x2_ref, ltri_ref, wg_ref, wu_ref, wd_ref, g_ref, y_ref,
                      chl_ref, gp_col_ref, gp_row_ref, pos_col_ref, pos_row_ref):
    grp = pl.program_id(1)
    gf = grp.astype(f32)
    tb = xn_ref.shape[0]

    @pl.when(grp == 0)
    def _():
        y_ref[...] = jnp.zeros_like(y_ref)
        comb = comb_ref[...]
        lane = lax.broadcasted_iota(jnp.int32, comb.shape, 1)
        gcol = comb[:, GRP_LANE:GRP_LANE + 1]
        onehot = jnp.where(lane.astype(f32) == gcol, 1.0, 0.0)
        before = _dot(ltri_ref[...], onehot.astype(bf16))
        pos = jnp.sum(onehot * before, axis=1, keepdims=True)
        gp = jnp.where(lane == 0, gcol, 0.0) + jnp.where(lane == 1, pos, 0.0)
        gp_col_ref[...] = gp
        gp_row_ref[...] = gp.T[:8, :]
        chi = comb.astype(bf16)
        chl_ref[:, :LANES] = chi
        chl_ref[:, LANES:] = (comb - chi.astype(f32)).astype(bf16)

    in_grp_col = gp_col_ref[:, 0:1] == gf
    pos_col_ref[...] = jnp.broadcast_to(jnp.where(in_grp_col, gp_col_ref[:, 1:2], -1.0), pos_col_ref.shape)
    pos_row_ref[...] = jnp.broadcast_to(jnp.where(gp_row_ref[0:1, :] == gf, gp_row_ref[1:2, :], -1.0), (8, tb))
    cnt = jnp.sum(jnp.where(in_grp_col, 1.0, 0.0)).astype(jnp.int32)

    def run_tiles(rows):
        def tile(t, carry):
            base = (t * rows).astype(f32)
            r_iota = lax.broadcasted_iota(jnp.int32, (rows, tb), 0).astype(f32)
            gather = jnp.where(pos_row_ref[0:1, :] - base == r_iota, 1.0, 0.0).astype(bf16)
            x = _dot(gather, xn_ref[...]).astype(bf16)
            c2 = _dot(gather, chl_ref[...])
            c = c2[:, :LANES] + c2[:, LANES:]
            lane = lax.broadcasted_iota(jnp.int32, c.shape, 1)
            hid = []
            for e in range(MOE_PER_GROUP):
                ce = jnp.sum(jnp.where(lane == grp * MOE_PER_GROUP + e, c, 0.0), axis=1, keepdims=True)
                a = _dot(x, wg_ref[e])
                u = _dot(x, wu_ref[e])
                hid.append((a * _sigmoid(a) * u * ce).astype(bf16))
            y = _dot(jnp.concatenate(hid, axis=1), wd_ref[...]).astype(bf16)
            l_iota = lax.broadcasted_iota(jnp.int32, (tb, rows), 1).astype(f32)
            scatter = jnp.where(pos_col_ref[:, :rows] - base == l_iota, 1.0, 0.0).astype(bf16)
            y_ref[...] += _dot(scatter, y)
            return carry

        lax.fori_loop(0, (cnt + rows - 1) // rows, tile, 0)

    n_tiles = [(cnt + rows - 1) // rows for rows in MOE_TILES]
    for i, rows in enumerate(MOE_TILES):
        pick = n_tiles[i] == n_tiles[-1]
        for j in range(i):
            pick = pick & (n_tiles[j] != n_tiles[-1])
        pl.when(pick)(functools.partial(run_tiles, rows))

    @pl.when(grp == MOE_GROUPS - 1)
    def _():
        y_ref[...] = _rms(x2_ref[...] + y_ref[...], g_ref[...])


def _moe_final(xn, comb, x2, wg, wu, wd, g, tb):
    n = x2.shape[0]
    ltri = jnp.tril(jnp.ones((tb, tb), bf16), -1)
    return pl.pallas_call(
        _moe_final_kernel,
        grid=(n // tb, MOE_GROUPS),
        in_specs=[pl.BlockSpec((tb, D_MODEL), lambda i, j: (i, 0)),
                  pl.BlockSpec((tb, LANES), lambda i, j: (i, 0)),
                  pl.BlockSpec((tb, D_MODEL), lambda i, j: (i, 0)),
                  pl.BlockSpec((tb, tb), lambda i, j: (0, 0)),
                  pl.BlockSpec((MOE_PER_GROUP, D_MODEL, MOE_HIDDEN), lambda i, j: (j, 0, 0)),
                  pl.BlockSpec((MOE_PER_GROUP, D_MODEL, MOE_HIDDEN), lambda i, j: (j, 0, 0)),
                  pl.BlockSpec((None, MOE_PER_GROUP * MOE_HIDDEN, D_MODEL), lambda i, j: (j, 0, 0)),
                  pl.BlockSpec((1, D_MODEL), lambda i, j: (0, 0))],
        out_specs=pl.BlockSpec((tb, D_MODEL), lambda i, j: (i, 0)),
        out_shape=jax.ShapeDtypeStruct((n, D_MODEL), f32),
        scratch_shapes=[pltpu.VMEM((tb, 2 * LANES), bf16),
                        pltpu.VMEM((tb, LANES), f32), pltpu.VMEM((8, tb), f32),
                        pltpu.VMEM((tb, pl.cdiv(max(MOE_TILES), LANES) * LANES), f32), pltpu.VMEM((8, tb), f32)],
        compiler_params=pltpu.CompilerParams(
            dimension_semantics=("parallel", "arbitrary"), vmem_limit_bytes=VMEM_LIMIT),
        name="moe_final",
    )(xn, comb, x2, ltri, wg, wu, wd.reshape(MOE_GROUPS, MOE_PER_GROUP * MOE_HIDDEN, D_MODEL), g)


def _layer(x3, state, lw, *, prompt):
    nbatch, t_real, _ = x3.shape
    if prompt:
        t = t_real
        L, valid, z_dtype, tm, nblk = ML_CHUNK, ML_CHUNK, bf16, IN_TILE, MIXER_BLOCKS
        x = x3.reshape(nbatch * t, D_MODEL)
    else:
        t = SAMPLE_PAD_T
        L, valid, z_dtype, tm, nblk = t, t_real, f32, IN_SUB, 1
        x = jnp.pad(x3.transpose(1, 0, 2), ((0, t - t_real), (0, 0), (0, 0))).reshape(t * nbatch, D_MODEL)
    n = nbatch * t
    nb = ROWS // L

    z, zg, zgt = _in_proj(x, lw["norm_mix_g"], lw["w_a"], lw["w_b"], lw["w_gate"], lw["gate_bias"],
                          lw["gm_norm_g"], tm, z_dtype)
    c0, n0, m0 = state
    m0 = jnp.broadcast_to(m0[:, :, None], (nbatch, ML_HEADS, LANES))
    ws_bd, b_col = _gmlp_mixing(lw["gm_ws"], lw["gm_bs"], L)
    hml, hgm, c1, n1, m1 = _mixer_core(z, zg, zgt, c0, n0, m0, lw["ml_norm_g"],
                                       ws_bd, b_col, nblk=nblk, nb=nb, L=L, valid=valid, h_dtype=z_dtype)
    n_real = nbatch * t_real
    x2, xn, comb = _proj_router(hml, hgm, z, x, n_real, lw["p_a"], lw["p_b"], lw["w_out"],
                                lw["norm_ffn_g"], lw["w_router"], lw["b_router"], min(n_real, PROJ_TILE))
    y = _moe_final(xn, comb, x2, lw["e_wg"], lw["e_wu"], lw["e_wd"], lw["out_g"], min(n_real, MOE_BLOCK))
    if prompt:
        return y.reshape(nbatch, t_real, D_MODEL), (c1, n1, m1[:, :, 0]), None
    unrows = lambda a: a.reshape(t_real, nbatch, D_MODEL).transpose(1, 0, 2)
    return unrows(y), (c1, n1, m1[:, :, 0]), unrows(z[:n_real, ZB_GV * D_MODEL:(ZB_GV + 1) * D_MODEL])


def kernel(x_prompt, x_sample, state_mlstm_C, state_mlstm_n, state_mlstm_m, norm_mix_g, w_in, ml_b_i, ml_b_f, ml_norm_g, gm_norm_g, gm_ws, gm_bs, p_a, p_b, w_out, norm_ffn_g, rc_w, rc_b, rf_w, rf_b, e_wg, e_wu, e_wd, final_norm_g):
    depth = w_in.shape[0]
    assert depth == 1, "the final norm is fused into the last layer's MoE kernel; only depth 1 is wired up"
    nbp = x_prompt.shape[0]
    nbs, ts, _ = x_sample.shape

    def layer_weights(l):
        w = w_in[l]
        gates = w[:, GATE_OFF:GATE_OFF + 2 * ML_HEADS]
        return dict(
            norm_mix_g=norm_mix_g[l][None, :],
            w_a=w[:, :GATE_OFF].astype(bf16), w_b=w[:, GATE_OFF + 2 * ML_HEADS:].astype(bf16),
            w_gate=jnp.pad(gates, ((0, 0), (0, LANES - 2 * ML_HEADS))).astype(bf16),
            gate_bias=jnp.pad(jnp.concatenate([ml_b_i[l], ml_b_f[l]]), (0, LANES - 2 * ML_HEADS))[None, :],
            ml_norm_g=ml_norm_g[l],
            gm_norm_g=gm_norm_g[l].reshape(1, GM_GROUPS * GM_DG),
            gm_ws=gm_ws[l], gm_bs=gm_bs[l],
            p_a=p_a[l].astype(bf16), p_b=p_b[l].astype(bf16), w_out=w_out[l].astype(bf16),
            norm_ffn_g=norm_ffn_g[l][None, :],
            w_router=jnp.pad(
                jnp.concatenate([rf_w[l].transpose(1, 0, 2).reshape(D_MODEL, MOE_EXPERTS), rc_w[l]], axis=1),
                ((0, 0), (0, LANES - MOE_EXPERTS - MOE_GROUPS))).astype(bf16),
            b_router=jnp.pad(jnp.concatenate([rf_b[l].reshape(-1), rc_b[l]]),
                             (0, LANES - MOE_EXPERTS - MOE_GROUPS))[None, :],
            e_wg=e_wg[l].astype(bf16), e_wu=e_wu[l].astype(bf16), e_wd=e_wd[l].astype(bf16),
            out_g=final_norm_g[None, :],
        )

    lw = layer_weights(0)

    zero_state = (jnp.zeros((nbp, ML_HEADS, ML_DV, ML_DQK), f32), jnp.zeros((nbp, ML_HEADS, ML_DQK), f32),
                  jnp.zeros((nbp, ML_HEADS), f32))
    y_p, (c_p, n_p, m_p), _ = _layer(x_prompt, zero_state, lw, prompt=True)

    y_s, (c_s, n_s, m_s), v_s = _layer(x_sample, (state_mlstm_C[0], state_mlstm_n[0], state_mlstm_m[0]),
                                       lw, prompt=False)
    v_s = v_s.reshape(nbs, ts, GM_GROUPS, GM_DG)
    return (y_p, y_s, c_p[None], n_p[None], m_p[None], c_s[None], n_s[None], m_s[None], v_s[None])
```

```python
import functools

import numpy as np
import jax
import jax.numpy as jnp
from jax import lax
from jax.experimental import pallas as pl
from jax.experimental.pallas import tpu as pltpu

D_MODEL = 1024
ML_HEADS = 4
ML_DQK = 128
ML_DV = 256
ML_CHUNK = 128
GM_GROUPS = 4
GM_DG = 256
MOE_GROUPS = 4
MOE_PER_GROUP = 8
MOE_EXPERTS = MOE_GROUPS * MOE_PER_GROUP
MOE_HIDDEN = 256
EPS = 1e-6

LANES = 128
SAMPLE_PAD_T = 8
ROWS = 128
N_PAD_ROWS = 16
NEG = -1e30
MOE_TILES = (128, 144, 160, 192, 224, 256)
MOE_TILE_FLAT_COST, MOE_TILE_BASE_COST, MOE_TILE_ROW_COST = 5050, 1310, 29
MOE_BLOCK = 1024
IN_TILE = 512
IN_SUB = 256
MIXER_BLOCKS = 4
PROJ_TILE = 1024
PROJ_SUB = 256
GRP_LANE = MOE_EXPERTS
VMEM_LIMIT = 56 * 1024 * 1024

ZB_QK, ZB_V, ZB_OG, ZB_U, ZB_GV, ZB_GA, ZB_GB = range(7)
N_ZB = 7
N_ZB_A = 3
GATE_OFF = 2 * ML_HEADS * ML_DQK + 2 * ML_HEADS * ML_DV

f32 = jnp.float32
bf16 = jnp.bfloat16


def _sigmoid(x):
    return 0.5 * jnp.tanh(0.5 * x) + 0.5


def _log_sigmoid(x):
    return jnp.minimum(x, 0.0) - jnp.log1p(jnp.exp(-jnp.abs(x)))


def _gelu_tanh(x):
    return 0.5 * x * (1.0 + jnp.tanh(np.sqrt(2.0 / np.pi) * (x + 0.044715 * (x * x * x))))


def _rms(x, g):
    return x * lax.rsqrt(jnp.mean(x * x, axis=-1, keepdims=True) + EPS) * g


def _dot(a, b):
    return jnp.dot(a, b, preferred_element_type=f32)


def _dot_nt(a, b):
    return lax.dot_general(a, b, (((1,), (1,)), ((), ())), preferred_element_type=f32)


def _dot_tn(a, b):
    return lax.dot_general(a, b, (((0,), (0,)), ((), ())), preferred_element_type=f32)


def _in_proj_kernel(x_ref, g_ref, wa_ref, wb_ref, wg_ref, gb_ref, gmg_ref, z_ref, zg_ref, zgt_ref, *, real_steps):
    if real_steps is not None:
        @pl.when(pl.program_id(0) >= real_steps)
        def _():
            z_ref[...] = jnp.zeros_like(z_ref)
            zg_ref[...] = jnp.zeros_like(zg_ref)
            zgt_ref[...] = jnp.zeros_like(zgt_ref)

        pl.when(pl.program_id(0) < real_steps)(functools.partial(
            _in_proj_rows, x_ref, g_ref, wa_ref, wb_ref, wg_ref, gb_ref, gmg_ref, z_ref, zg_ref, zgt_ref))
    else:
        _in_proj_rows(x_ref, g_ref, wa_ref, wb_ref, wg_ref, gb_ref, gmg_ref, z_ref, zg_ref, zgt_ref)


def _in_proj_rows(x_ref, g_ref, wa_ref, wb_ref, wg_ref, gb_ref, gmg_ref, z_ref, zg_ref, zgt_ref):
    tm = x_ref.shape[0]
    subs = [slice(r, r + IN_SUB) for r in range(0, tm, IN_SUB)]
    xn = [None] * len(subs)

    def prepare(s):
        rows = subs[s]
        xn[s] = _rms(x_ref[rows, :], g_ref[...]).astype(bf16)
        zg = _dot(xn[s], wg_ref[...]) + gb_ref[...]
        lane = lax.broadcasted_iota(jnp.int32, zg.shape, 1)
        zg = jnp.where((lane >= ML_HEADS) & (lane < 2 * ML_HEADS), _log_sigmoid(zg), zg)
        zg_ref[rows, :] = zg
        zgt_ref[:, rows] = zg.T[:2 * ML_HEADS, :]

    def gelu_group_rms(a, g):
        return _rms(_gelu_tanh(a), gmg_ref[:, g * GM_DG:(g + 1) * GM_DG])

    act = {ZB_QK: None, ZB_V: None, ZB_OG: _sigmoid, ZB_GA: _sigmoid, ZB_GB: _sigmoid,
           ZB_U: _gelu_tanh, ZB_GV: gelu_group_rms}
    per_blk = D_MODEL // GM_DG

    def emit(blk, s):
        for g in range(per_blk):
            cols = slice(blk * D_MODEL + g * GM_DG, blk * D_MODEL + (g + 1) * GM_DG)
            if blk < N_ZB_A:
                a = _dot(xn[s], wa_ref[:, cols])
            else:
                a = _dot(xn[s], wb_ref[:, cols.start - N_ZB_A * D_MODEL:cols.stop - N_ZB_A * D_MODEL])
            if blk == ZB_GV:
                a = gelu_group_rms(a, g)
            elif act[blk] is not None:
                a = act[blk](a)
            z_ref[subs[s], cols] = a.astype(z_ref.dtype)

    for s in range(len(subs)):
        prepare(s)
        emit(0, s)
    for blk in range(1, N_ZB):
        for s in range(len(subs)):
            emit(blk, s)


def _in_proj(x, g, w_a, w_b, w_gate, gate_bias, gm_g, tm, z_dtype, n_real):
    n = x.shape[0]
    const = lambda shape: pl.BlockSpec(shape, lambda i: (0, 0), pipeline_mode=pl.Buffered(1))
    return pl.pallas_call(
        functools.partial(_in_proj_kernel, real_steps=None if n_real == n else n_real // tm),
        grid=(n // tm,),
        in_specs=[
            pl.BlockSpec((tm, D_MODEL), lambda i: (i, 0)),
            const((1, D_MODEL)),
            const((D_MODEL, N_ZB_A * D_MODEL)),
            const((D_MODEL, (N_ZB - N_ZB_A) * D_MODEL)),
            const((D_MODEL, LANES)),
            const((1, LANES)),
            const((1, D_MODEL)),
        ],
        out_specs=[
            pl.BlockSpec((tm, N_ZB * D_MODEL), lambda i: (i, 0)),
            pl.BlockSpec((tm, LANES), lambda i: (i, 0)),
            pl.BlockSpec((2 * ML_HEADS, tm), lambda i: (0, i)),
        ],
        out_shape=[
            jax.ShapeDtypeStruct((n, N_ZB * D_MODEL), z_dtype),
            jax.ShapeDtypeStruct((n, LANES), f32),
            jax.ShapeDtypeStruct((2 * ML_HEADS, n), f32),
        ],
        compiler_params=pltpu.CompilerParams(
            dimension_semantics=("parallel",), vmem_limit_bytes=VMEM_LIMIT),
        name="in_proj",
    )(x, g, w_a, w_b, w_gate, gate_bias, gm_g)


def _mixer_core_kernel(zqk_ref, zv_ref, zog_ref, zu_ref, zgv_ref, zg_ref, zgt_ref,
                       c0_ref, n0_ref, m0_ref, mlg_ref, wsbd_ref, bcol_ref, eye_ref,
                       hml_ref, hgm_ref, c_ref, n_ref, m_ref, *, nblk, nb, L, valid, single_chunk):
    if single_chunk:
        cs_ref, ns_ref, ms_ref = c0_ref, n0_ref, m0_ref
    else:
        cs_ref, ns_ref, ms_ref = c_ref, n_ref, m_ref

        @pl.when(pl.program_id(1) == 0)
        def _():
            c_ref[...] = c0_ref[...]
            n_ref[...] = n0_ref[...]
            m_ref[...] = m0_ref[...]

    seq_bits = nb.bit_length() - 1
    seq_of = lambda p: p & (nb - 1)
    pos_of = lambda p: p >> seq_bits
    pp = lax.broadcasted_iota(jnp.int32, (ROWS, ROWS), 0)
    qq = lax.broadcasted_iota(jnp.int32, (ROWS, ROWS), 1)
    same = seq_of(pp) == seq_of(qq)
    p_col = lax.broadcasted_iota(jnp.int32, (ROWS, 1), 0)
    p_row = lax.broadcasted_iota(jnp.int32, (1, ROWS), 1)
    col_ok = pos_of(p_col) < valid
    row_ok = pos_of(p_row) < valid
    causal = same & (qq <= pp) & row_ok
    upper = same & (pp <= qq)
    scale = ML_DQK ** -0.5
    one_seq = nb == 1

    def rows_of(ref, r, cols):
        if one_seq:
            return ref[r, :, cols]
        return ref[:, :, cols].reshape(ROWS, cols.stop - cols.start)

    def per_row(vec):
        return jnp.concatenate([vec] * L, axis=0)

    stores = []
    chains = [(r, h) for r in range(nblk) for h in range(ML_HEADS)]
    each = lambda fn: [fn(i, r, h) for i, (r, h) in enumerate(chains)]

    q = each(lambda i, r, h: rows_of(zqk_ref, r, slice(h * ML_DQK, (h + 1) * ML_DQK)))
    k = each(lambda i, r, h: rows_of(zqk_ref, r, slice((ML_HEADS + h) * ML_DQK, (ML_HEADS + h + 1) * ML_DQK)))
    v = each(lambda i, r, h: rows_of(zv_ref, r, slice(h * ML_DV, (h + 1) * ML_DV)))
    qc = [x.astype(bf16) for x in q]
    kc = [x.astype(bf16) for x in k]
    c0 = each(lambda i, r, h: cs_ref[r * nb:(r + 1) * nb, h])
    if one_seq:
        m0 = each(lambda i, r, h: ms_ref[r, h:h + 1, 0:1])
    else:
        m0 = each(lambda i, r, h: per_row(ms_ref[:, h, 0:1]))
        n0 = each(lambda i, r, h: per_row(ns_ref[:, h, :]))
    logi_col = each(lambda i, r, h: rows_of(zg_ref, r, slice(h, h + 1)))
    logi_row = each(lambda i, r, h: zgt_ref[r, h:h + 1, :])
    logf_col = each(lambda i, r, h: jnp.where(col_ok, rows_of(zg_ref, r, slice(ML_HEADS + h, ML_HEADS + h + 1)), 0.0))
    logf_row = each(lambda i, r, h: jnp.where(row_ok, zgt_ref[r, ML_HEADS + h:ML_HEADS + h + 1, :], 0.0))

    if one_seq:
        n_rows = each(lambda i, r, h: jnp.broadcast_to(ns_ref[r, h:h + 1, :], (N_PAD_ROWS, ML_DQK)).astype(bf16))
        qkc = each(lambda i, r, h: _dot_nt(
            qc[i], jnp.concatenate([kc[i], c0[i].reshape(ML_DV, ML_DQK).astype(bf16), n_rows[i]], axis=0)))
    else:
        qkc = each(lambda i, r, h: _dot_nt(
            qc[i], jnp.concatenate([kc[i], c0[i].reshape(nb * ML_DV, ML_DQK).astype(bf16)], axis=0)))
    b_col = each(lambda i, r, h: jnp.sum(jnp.where(causal, logf_row[i], 0.0), axis=1, keepdims=True))
    b_row = each(lambda i, r, h: jnp.sum(jnp.where(upper, logf_col[i], 0.0), axis=0, keepdims=True))
    if one_seq:
        b_last_col = each(lambda i, r, h: jnp.sum(logf_row[i], axis=1, keepdims=True))
        b_last_row = b_last_col
    else:
        b_last_col = each(lambda i, r, h: jnp.sum(jnp.where(same, logf_row[i], 0.0), axis=1, keepdims=True))
        b_last_row = each(lambda i, r, h: jnp.sum(jnp.where(same, logf_col[i], 0.0), axis=0, keepdims=True))
    d = each(lambda i, r, h: jnp.where(causal, b_col[i] - b_row[i] + logi_row[i], NEG))
    inter = each(lambda i, r, h: b_col[i] + m0[i])
    m_col = each(lambda i, r, h: jnp.maximum(inter[i], jnp.max(d[i], axis=1, keepdims=True)))
    w_intra = each(lambda i, r, h: jnp.exp(d[i] - m_col[i]) * scale)
    w_inter = each(lambda i, r, h: jnp.exp(inter[i] - m_col[i]) * scale)
    s = each(lambda i, r, h: qkc[i][:, :ROWS] * w_intra[i])
    if one_seq:
        q_mem = each(lambda i, r, h: qkc[i][:, ROWS:ROWS + ML_DV])
    else:
        seq_col = seq_of(p_col)
        q_mem = each(lambda i, r, h: functools.reduce(jnp.add, [
            jnp.where(seq_col == b, qkc[i][:, ROWS + b * ML_DV:ROWS + (b + 1) * ML_DV], 0.0) for b in range(nb)]))
    if one_seq:
        ones_blk = jnp.ones((ROWS, LANES), bf16)
        sv = each(lambda i, r, h: _dot(s[i].astype(bf16), jnp.concatenate([v[i].astype(bf16), ones_blk], axis=1)))
        num = each(lambda i, r, h: sv[i][:, :ML_DV] + w_inter[i] * q_mem[i])
        s_sum = each(lambda i, r, h: sv[i][:, ML_DV:ML_DV + 1])
        q_n = each(lambda i, r, h: qkc[i][:, ROWS + ML_DV:ROWS + ML_DV + 1])
    else:
        num = each(lambda i, r, h: _dot(s[i].astype(bf16), v[i].astype(bf16)) + w_inter[i] * q_mem[i])
        s_sum = each(lambda i, r, h: jnp.sum(s[i], axis=1, keepdims=True))
        q_n = each(lambda i, r, h: jnp.sum(q[i].astype(f32) * n0[i], axis=1, keepdims=True))
    den = each(lambda i, r, h: s_sum[i] + w_inter[i] * q_n[i])
    ones = jnp.ones((ML_DV, LANES), bf16)

    def head_norm(x, g):
        if not one_seq:
            return _rms(x, g)
        ssq = _dot((x * x).astype(bf16), ones)
        scale = lax.rsqrt(ssq * (1.0 / ML_DV) + EPS)
        return x * jnp.concatenate([scale] * (ML_DV // LANES), axis=1) * g

    hh = each(lambda i, r, h: head_norm(num[i] / jnp.maximum(jnp.abs(den[i]), jnp.exp(-m_col[i])), mlg_ref[h:h + 1, :]))
    def token_store(ref, r, cols, val):
        if one_seq:
            stores.append((ref, (r, slice(None), cols), val.astype(ref.dtype)))
        else:
            stores.append((ref, (slice(None), slice(None), cols),
                           val.astype(ref.dtype).reshape(L, nb, cols.stop - cols.start)))

    for i, (r, h) in enumerate(chains):
        cols = slice(h * ML_DV, (h + 1) * ML_DV)
        token_store(hml_ref, r, cols, rows_of(zog_ref, r, cols).astype(f32) * hh[i])

    wend_col = each(lambda i, r, h: jnp.where(col_ok, b_last_col[i] - b_col[i] + logi_col[i], NEG))
    wend_row = each(lambda i, r, h: jnp.where(row_ok, b_last_row[i] - b_row[i] + logi_row[i], NEG))
    if one_seq:
        m_new = each(lambda i, r, h: jnp.maximum(b_last_col[i] + m0[i], jnp.max(wend_row[i], axis=1, keepdims=True)))
    else:
        m_new = each(lambda i, r, h: jnp.maximum(
            b_last_col[i] + m0[i], jnp.max(jnp.where(same, wend_row[i], NEG), axis=1, keepdims=True)))
    decay = each(lambda i, r, h: jnp.exp(b_last_col[i] + m0[i] - m_new[i]))
    wend = each(lambda i, r, h: jnp.exp(wend_col[i] - m_new[i]))
    vw = each(lambda i, r, h: (v[i].astype(f32) * wend[i]).astype(bf16))
    kw = each(lambda i, r, h: k[i].astype(f32) * wend[i])
    if nb == 1:
        upd = each(lambda i, r, h: _dot_tn(vw[i], kc[i]))
    else:
        vw_t = each(lambda i, r, h: _dot_nt(eye_ref[...], vw[i]).astype(bf16))
        seq_of_lane = seq_of(lax.broadcasted_iota(jnp.int32, (ML_DV, ROWS), 1))
        upd = each(lambda i, r, h: _dot(jnp.concatenate(
            [jnp.where(seq_of_lane == b, vw_t[i], jnp.zeros_like(vw_t[i])) for b in range(nb)], axis=0), kc[i]))
    if one_seq:
        kw_sum = each(lambda i, r, h: jnp.sum(kw[i], axis=0, keepdims=True))
    else:
        kw_sum = each(lambda i, r, h: jnp.sum(kw[i].reshape(L, nb, ML_DQK), axis=0))
    for i, (r, h) in enumerate(chains):
        for b in range(nb):
            slot = r * nb + b
            dec = decay[i][b:b + 1, :]
            stores.append((c_ref, (slot, h), dec * c0[i][b] + upd[i][b * ML_DV:(b + 1) * ML_DV]))
            stores.append((n_ref, (slot, slice(h, h + 1), slice(None)),
                           dec * ns_ref[slot, h:h + 1, :] + kw_sum[i][b:b + 1, :]))
            stores.append((m_ref, (slot, slice(h, h + 1), slice(None)),
                           jnp.broadcast_to(m_new[i][b:b + 1, :], (1, LANES))))

    for r in range(nblk):
        gcols = [slice(g * GM_DG, (g + 1) * GM_DG) for g in range(GM_GROUPS)]
        gv = jnp.concatenate([rows_of(zgv_ref, r, c).astype(bf16) for c in gcols], axis=0)
        mixed = _dot(wsbd_ref[...], gv) + bcol_ref[...]
        for g, c in enumerate(gcols):
            token_store(hgm_ref, r, c, rows_of(zu_ref, r, c).astype(f32) * mixed[g * ROWS:(g + 1) * ROWS])

    for ref, idx, val in stores:
        ref[idx] = val


def _mixer_core(z, zg, zgt, c0, n0, m0, ml_g, ws_bd, b_col, *, nblk, nb, L, valid, h_dtype):
    n = z.shape[0]
    nbatch = c0.shape[0]
    groups = nbatch // nb
    nc = n // groups // ROWS
    if nb == 1:
        tok_shape = (groups, nc, ROWS)
        zgt4 = zgt.reshape(2 * ML_HEADS, groups, nc, ROWS).transpose(1, 2, 0, 3)
        tspec = lambda width, blk: pl.BlockSpec((nblk, None, ROWS, width), lambda b, c: (b, c, 0, blk))
    else:
        assert nc == 1 and nblk == 1
        tok_shape = (L, groups, nb)
        zgt4 = zgt.reshape(2 * ML_HEADS, L, groups, nb).transpose(2, 0, 1, 3).reshape(groups, 1, 2 * ML_HEADS, ROWS)
        tspec = lambda width, blk: pl.BlockSpec((L, None, nb, width), lambda b, c: (0, b, 0, blk))
    z4 = z.reshape(*tok_shape, N_ZB * D_MODEL)
    zg4 = zg.reshape(*tok_shape, LANES)
    zspec = lambda blk: tspec(D_MODEL, blk)
    full = lambda shape: pl.BlockSpec(shape, lambda b, c: (0,) * len(shape))
    state_specs = [
        pl.BlockSpec((nblk * nb, ML_HEADS, ML_DV, ML_DQK), lambda b, c: (b, 0, 0, 0)),
        pl.BlockSpec((nblk * nb, ML_HEADS, ML_DQK), lambda b, c: (b, 0, 0)),
        pl.BlockSpec((nblk * nb, ML_HEADS, LANES), lambda b, c: (b, 0, 0)),
    ]
    tok_spec = zspec(0)
    eye = jnp.eye(ML_DV, dtype=bf16)
    hml, hgm, c1, n1, m1 = pl.pallas_call(
        functools.partial(_mixer_core_kernel, nblk=nblk, nb=nb, L=L, valid=valid, single_chunk=(nc == 1)),
        grid=(groups // nblk, nc),
        in_specs=[zspec(ZB_QK), zspec(ZB_V), zspec(ZB_OG), zspec(ZB_U), zspec(ZB_GV),
                  tspec(LANES, 0),
                  pl.BlockSpec((nblk, None, 2 * ML_HEADS, ROWS), lambda b, c: (b, c, 0, 0)),
                  *state_specs,
                  full((ML_HEADS, ML_DV)), full((GM_GROUPS * ROWS, GM_GROUPS * ROWS)), full((GM_GROUPS * ROWS, 1)),
                  full((ML_DV, ML_DV))],
        out_specs=[tok_spec, tok_spec] + state_specs,
        out_shape=[
            jax.ShapeDtypeStruct((*tok_shape, D_MODEL), h_dtype),
            jax.ShapeDtypeStruct((*tok_shape, D_MODEL), h_dtype),
            jax.ShapeDtypeStruct((nbatch, ML_HEADS, ML_DV, ML_DQK), f32),
            jax.ShapeDtypeStruct((nbatch, ML_HEADS, ML_DQK), f32),
            jax.ShapeDtypeStruct((nbatch, ML_HEADS, LANES), f32),
        ],
        compiler_params=pltpu.CompilerParams(
            dimension_semantics=("parallel", "arbitrary"), vmem_limit_bytes=VMEM_LIMIT),
        name="mixer_core",
    )(z4, z4, z4, z4, z4, zg4, zgt4, c0, n0, m0, ml_g, ws_bd, b_col, eye)
    return hml.reshape(n, D_MODEL), hgm.reshape(n, D_MODEL), c1, n1, m1


def _gmlp_mixing(gm_ws, gm_bs, L):
    reps = ROWS // L
    tril = jnp.tril(jnp.ones((L, L), bool))
    blocks = [jnp.kron(jnp.where(tril, gm_ws[g, :L, :L], 0.0), jnp.eye(reps, dtype=f32)) for g in range(GM_GROUPS)]
    ws_bd = jax.scipy.linalg.block_diag(*blocks).astype(bf16)
    b_col = jnp.concatenate([jnp.repeat(gm_bs[g, :L], reps) for g in range(GM_GROUPS)])[:, None]
    return ws_bd, b_col


def _proj_router_kernel(hml_ref, hgm_ref, sga_ref, sgb_ref, x_ref, pa_ref, pb_ref, wo_ref, g_ref, wr_ref, br_ref,
                        x2_ref, xn_ref, comb_ref):
    tm = x_ref.shape[0]
    subs = [slice(r, r + PROJ_SUB) for r in range(0, tm, PROJ_SUB)]
    ab, xn = {}, {}

    def stage_branches(i):
        r = subs[i]
        ab[i] = (_dot(hml_ref[r, :].astype(bf16), pa_ref[...]), _dot(hgm_ref[r, :].astype(bf16), pb_ref[...]))

    def stage_merge(i):
        r = subs[i]
        a, b = ab.pop(i)
        merged = sga_ref[r, :].astype(f32) * a + sgb_ref[r, :].astype(f32) * b
        x2 = x_ref[r, :] + _dot(merged.astype(bf16), wo_ref[...])
        x2_ref[r, :] = x2
        xn[i] = _rms(x2, g_ref[...]).astype(bf16)
        xn_ref[r, :] = xn[i]

    def stage_route(i):
        lg = _dot(xn.pop(i), wr_ref[...]) + br_ref[...]
        lane = lax.broadcasted_iota(jnp.int32, lg.shape, 1).astype(f32)
        cmask = (lane >= MOE_EXPERTS) & (lane < MOE_EXPERTS + MOE_GROUPS)
        cl = jnp.where(cmask, lg, NEG)
        cmax = jnp.max(cl, axis=1, keepdims=True)
        p_grp = 1.0 / jnp.sum(jnp.where(cmask, jnp.exp(cl - cmax), 0.0), axis=1, keepdims=True)
        grp = jnp.min(jnp.where(cl == cmax, lane, 2.0 * LANES), axis=1, keepdims=True) - MOE_EXPERTS
        fmask = (lane >= grp * MOE_PER_GROUP) & (lane < (grp + 1.0) * MOE_PER_GROUP)
        fl = jnp.where(fmask, lg, NEG)
        v1 = jnp.max(fl, axis=1, keepdims=True)
        i1 = jnp.min(jnp.where(fl == v1, lane, 2.0 * LANES), axis=1, keepdims=True)
        fl2 = jnp.where(lane == i1, NEG, fl)
        v2 = jnp.max(fl2, axis=1, keepdims=True)
        i2 = jnp.min(jnp.where(fl2 == v2, lane, 2.0 * LANES), axis=1, keepdims=True)
        e2 = jnp.exp(v2 - v1)
        g1 = p_grp / (1.0 + e2)
        g2 = p_grp * e2 / (1.0 + e2)
        comb_ref[subs[i], :] = (jnp.where(lane == i1, g1, 0.0) + jnp.where(lane == i2, g2, 0.0)
                                + jnp.where(lane == GRP_LANE, grp, 0.0))

    n = len(subs)
    for step in range(n + 2):
        if step < n:
            stage_branches(step)
        if 0 <= step - 1 < n:
            stage_merge(step - 1)
        if 0 <= step - 2 < n:
            stage_route(step - 2)


def _proj_router(hml, hgm, z, x, n, pa, pb, wo, g, wr, br, tm):
    row = lambda blk=0: pl.BlockSpec((tm, D_MODEL), lambda i, blk=blk: (i, blk))
    const = lambda shape: pl.BlockSpec(shape, lambda i: (0, 0), pipeline_mode=pl.Buffered(1))
    wfull = const((D_MODEL, D_MODEL))
    return pl.pallas_call(
        _proj_router_kernel,
        grid=(n // tm,),
        in_specs=[row(), row(), row(ZB_GA), row(ZB_GB), row(), wfull, wfull, wfull,
                  const((1, D_MODEL)), const((D_MODEL, LANES)), const((1, LANES))],
        out_specs=[row(), row(), pl.BlockSpec((tm, LANES), lambda i: (i, 0))],
        out_shape=[jax.ShapeDtypeStruct((n, D_MODEL), f32),
                   jax.ShapeDtypeStruct((n, D_MODEL), bf16),
                   jax.ShapeDtypeStruct((n, LANES), f32)],
        compiler_params=pltpu.CompilerParams(
            dimension_semantics=("parallel",), vmem_limit_bytes=VMEM_LIMIT),
        name="proj_router",
    )(hml, hgm, z, z, x, pa, pb, wo, g, wr, br)


def _moe_tile_cost(rows):
    return max(MOE_TILE_FLAT_COST, MOE_TILE_BASE_COST + MOE_TILE_ROW_COST * rows)


def _moe_final_kernel(xn_ref, comb_ref, x2_ref, ltri_ref, wg_ref, wu_ref, wd_ref, g_ref, y_ref,
                      chl_ref, gp_col_ref, gp_row_ref, pos_col_ref, pos_row_ref):
    grp = pl.program_id(1)
    gf = grp.astype(f32)
    tb = xn_ref.shape[0]

    @pl.when(grp == 0)
    def _():
        y_ref[...] = jnp.zeros_like(y_ref)
        comb = comb_ref[...]
        lane = lax.broadcasted_iota(jnp.int32, comb.shape, 1)
        gcol = comb[:, GRP_LANE:GRP_LANE + 1]
        onehot = jnp.where(lane.astype(f32) == gcol, 1.0, 0.0)
        before = _dot(ltri_ref[...], onehot.astype(bf16))
        pos = jnp.sum(onehot * before, axis=1, keepdims=True)
        gp = jnp.where(lane == 0, gcol, 0.0) + jnp.where(lane == 1, pos, 0.0)
        gp_col_ref[...] = gp
        gp_row_ref[...] = gp.T[:8, :]
        chi = comb.astype(bf16)
        chl_ref[:, :LANES] = chi
        chl_ref[:, LANES:] = (comb - chi.astype(f32)).astype(bf16)

    in_grp_col = gp_col_ref[:, 0:1] == gf
    pos_col_ref[...] = jnp.broadcast_to(jnp.where(in_grp_col, gp_col_ref[:, 1:2], -1.0), pos_col_ref.shape)
    pos_row_ref[...] = jnp.broadcast_to(jnp.where(gp_row_ref[0:1, :] == gf, gp_row_ref[1:2, :], -1.0), (8, tb))
    cnt = jnp.sum(jnp.where(in_grp_col, 1.0, 0.0)).astype(jnp.int32)

    def run_tiles(rows):
        def tile(t, carry):
            base = (t * rows).astype(f32)
            r_iota = lax.broadcasted_iota(jnp.int32, (rows, tb), 0).astype(f32)
            gather = jnp.where(pos_row_ref[0:1, :] - base == r_iota, 1.0, 0.0).astype(bf16)
            x = _dot(gather, xn_ref[...]).astype(bf16)
            c2 = _dot(gather, chl_ref[...])
            c = c2[:, :LANES] + c2[:, LANES:]
            lane = lax.broadcasted_iota(jnp.int32, c.shape, 1)
            hid = []
            for e in range(MOE_PER_GROUP):
                ce = jnp.sum(jnp.where(lane == grp * MOE_PER_GROUP + e, c, 0.0), axis=1, keepdims=True)
                a = _dot(x, wg_ref[e])
                u = _dot(x, wu_ref[e])
                hid.append((a * _sigmoid(a) * u * ce).astype(bf16))
            y = _dot(jnp.concatenate(hid, axis=1), wd_ref[...]).astype(bf16)
            l_iota = lax.broadcasted_iota(jnp.int32, (tb, rows), 1).astype(f32)
            scatter = jnp.where(pos_col_ref[:, :rows] - base == l_iota, 1.0, 0.0).astype(bf16)
            y_ref[...] += _dot(scatter, y)
            return carry

        lax.fori_loop(0, (cnt + rows - 1) // rows, tile, 0)

    totals = [((cnt + rows - 1) // rows) * _moe_tile_cost(rows) for rows in MOE_TILES]
    best = functools.reduce(jnp.minimum, totals)
    taken = False
    for rows, total in zip(MOE_TILES, totals):
        pick = (total == best) & jnp.logical_not(taken)
        pl.when(pick)(functools.partial(run_tiles, rows))
        taken = taken | (total == best)

    @pl.when(grp == MOE_GROUPS - 1)
    def _():
        y_ref[...] = _rms(x2_ref[...] + y_ref[...], g_ref[...])


def _moe_final(xn, comb, x2, wg, wu, wd, g, tb):
    n = x2.shape[0]
    ltri = jnp.tril(jnp.ones((tb, tb), bf16), -1)
    return pl.pallas_call(
        _moe_final_kernel,
        grid=(n // tb, MOE_GROUPS),
        in_specs=[pl.BlockSpec((tb, D_MODEL), lambda i, j: (i, 0)),
                  pl.BlockSpec((tb, LANES), lambda i, j: (i, 0)),
                  pl.BlockSpec((tb, D_MODEL), lambda i, j: (i, 0)),
                  pl.BlockSpec((tb, tb), lambda i, j: (0, 0)),
                  pl.BlockSpec((MOE_PER_GROUP, D_MODEL, MOE_HIDDEN), lambda i, j: (j, 0, 0)),
                  pl.BlockSpec((MOE_PER_GROUP, D_MODEL, MOE_HIDDEN), lambda i, j: (j, 0, 0)),
                  pl.BlockSpec((None, MOE_PER_GROUP * MOE_HIDDEN, D_MODEL), lambda i, j: (j, 0, 0)),
                  pl.BlockSpec((1, D_MODEL), lambda i, j: (0, 0))],
        out_specs=pl.BlockSpec((tb, D_MODEL), lambda i, j: (i, 0)),
        out_shape=jax.ShapeDtypeStruct((n, D_MODEL), f32),
        scratch_shapes=[pltpu.VMEM((tb, 2 * LANES), bf16),
                        pltpu.VMEM((tb, LANES), f32), pltpu.VMEM((8, tb), f32),
                        pltpu.VMEM((tb, pl.cdiv(max(MOE_TILES), LANES) * LANES), f32), pltpu.VMEM((8, tb), f32)],
        compiler_params=pltpu.CompilerParams(
            dimension_semantics=("parallel", "arbitrary"), vmem_limit_bytes=VMEM_LIMIT),
        name="moe_final",
    )(xn, comb, x2, ltri, wg, wu, wd.reshape(MOE_GROUPS, MOE_PER_GROUP * MOE_HIDDEN, D_MODEL), g)


def _layer(x3, state, lw, *, prompt):
    nbatch, t_real, _ = x3.shape
    if prompt:
        t = t_real
        L, valid, z_dtype, tm, nblk = ML_CHUNK, ML_CHUNK, bf16, IN_TILE, MIXER_BLOCKS
        x = x3.reshape(nbatch * t, D_MODEL)
    else:
        t = SAMPLE_PAD_T
        L, valid, z_dtype, tm, nblk = t, t_real, f32, IN_SUB, 1
        x = jnp.pad(x3.transpose(1, 0, 2), ((0, t - t_real), (0, 0), (0, 0))).reshape(t * nbatch, D_MODEL)
    n = nbatch * t
    nb = ROWS // L

    n_real = nbatch * t_real
    z, zg, zgt = _in_proj(x, lw["norm_mix_g"], lw["w_a"], lw["w_b"], lw["w_gate"], lw["gate_bias"],
                          lw["gm_norm_g"], tm, z_dtype, n_real)
    c0, n0, m0 = state
    m0 = jnp.broadcast_to(m0[:, :, None], (nbatch, ML_HEADS, LANES))
    ws_bd, b_col = _gmlp_mixing(lw["gm_ws"], lw["gm_bs"], L)
    hml, hgm, c1, n1, m1 = _mixer_core(z, zg, zgt, c0, n0, m0, lw["ml_norm_g"],
                                       ws_bd, b_col, nblk=nblk, nb=nb, L=L, valid=valid, h_dtype=z_dtype)
    x2, xn, comb = _proj_router(hml, hgm, z, x, n_real, lw["p_a"], lw["p_b"], lw["w_out"],
                                lw["norm_ffn_g"], lw["w_router"], lw["b_router"], min(n_real, PROJ_TILE))
    y = _moe_final(xn, comb, x2, lw["e_wg"], lw["e_wu"], lw["e_wd"], lw["out_g"], min(n_real, MOE_BLOCK))
    if prompt:
        return y.reshape(nbatch, t_real, D_MODEL), (c1, n1, m1[:, :, 0]), None
    unrows = lambda a: a.reshape(t_real, nbatch, D_MODEL).transpose(1, 0, 2)
    return unrows(y), (c1, n1, m1[:, :, 0]), unrows(z[:n_real, ZB_GV * D_MODEL:(ZB_GV + 1) * D_MODEL])


def kernel(x_prompt, x_sample, state_mlstm_C, state_mlstm_n, state_mlstm_m, norm_mix_g, w_in, ml_b_i, ml_b_f, ml_norm_g, gm_norm_g, gm_ws, gm_bs, p_a, p_b, w_out, norm_ffn_g, rc_w, rc_b, rf_w, rf_b, e_wg, e_wu, e_wd, final_norm_g):
    depth = w_in.shape[0]
    assert depth == 1, "the final norm is fused into the last layer's MoE kernel; only depth 1 is wired up"
    nbp = x_prompt.shape[0]
    nbs, ts, _ = x_sample.shape

    def layer_weights(l):
        w = w_in[l]
        gates = w[:, GATE_OFF:GATE_OFF + 2 * ML_HEADS]
        return dict(
            norm_mix_g=norm_mix_g[l][None, :],
            w_a=w[:, :GATE_OFF].astype(bf16), w_b=w[:, GATE_OFF + 2 * ML_HEADS:].astype(bf16),
            w_gate=jnp.pad(gates, ((0, 0), (0, LANES - 2 * ML_HEADS))).astype(bf16),
            gate_bias=jnp.pad(jnp.concatenate([ml_b_i[l], ml_b_f[l]]), (0, LANES - 2 * ML_HEADS))[None, :],
            ml_norm_g=ml_norm_g[l],
            gm_norm_g=gm_norm_g[l].reshape(1, GM_GROUPS * GM_DG),
            gm_ws=gm_ws[l], gm_bs=gm_bs[l],
            p_a=p_a[l].astype(bf16), p_b=p_b[l].astype(bf16), w_out=w_out[l].astype(bf16),
            norm_ffn_g=norm_ffn_g[l][None, :],
            w_router=jnp.pad(
                jnp.concatenate([rf_w[l].transpose(1, 0, 2).reshape(D_MODEL, MOE_EXPERTS), rc_w[l]], axis=1),
                ((0, 0), (0, LANES - MOE_EXPERTS - MOE_GROUPS))).astype(bf16),
            b_router=jnp.pad(jnp.concatenate([rf_b[l].reshape(-1), rc_b[l]]),
                             (0, LANES - MOE_EXPERTS - MOE_GROUPS))[None, :],
            e_wg=e_wg[l].astype(bf16), e_wu=e_wu[l].astype(bf16), e_wd=e_wd[l].astype(bf16),
            out_g=final_norm_g[None, :],
        )

    lw = layer_weights(0)

    zero_state = (jnp.zeros((nbp, ML_HEADS, ML_DV, ML_DQK), f32), jnp.zeros((nbp, ML_HEADS, ML_DQK), f32),
                  jnp.zeros((nbp, ML_HEADS), f32))
    y_p, (c_p, n_p, m_p), _ = _layer(x_prompt, zero_state, lw, prompt=True)

    y_s, (c_s, n_s, m_s), v_s = _layer(x_sample, (state_mlstm_C[0], state_mlstm_n[0], state_mlstm_m[0]),
                                       lw, prompt=False)
    v_s = v_s.reshape(nbs, ts, GM_GROUPS, GM_DG)
    return (y_p, y_s, c_p[None], n_p[None], m_p[None], c_s[None], n_s[None], m_s[None], v_s[None])
```

```python
import functools

import numpy as np
import jax
import jax.numpy as jnp
from jax import lax
from jax.experimental import pallas as pl
from jax.experimental.pallas import tpu as pltpu

D_MODEL = 1024
ML_HEADS = 4
ML_DQK = 128
ML_DV = 256
ML_CHUNK = 128
GM_GROUPS = 4
GM_DG = 256
MOE_GROUPS = 4
MOE_PER_GROUP = 8
MOE_EXPERTS = MOE_GROUPS * MOE_PER_GROUP
MOE_HIDDEN = 256
EPS = 1e-6

LANES = 128
SAMPLE_PAD_T = 8
ROWS = 128
N_PAD_ROWS = 16
NEG = -1e30
MOE_TILES = (128, 144, 160, 192, 224, 256)
MOE_TILE_FLAT_COST, MOE_TILE_BASE_COST, MOE_TILE_ROW_COST = 5050, 1310, 29
MOE_BLOCK = 1024
IN_TILE = 512
IN_SUB = 256
MIXER_BLOCKS = 4
PROJ_TILE = 1024
PROJ_SUB = 256
GRP_LANE = MOE_EXPERTS
VMEM_LIMIT = 56 * 1024 * 1024

ZB_QK, ZB_V, ZB_OG, ZB_U, ZB_GV, ZB_GA, ZB_GB = range(7)
N_ZB = 7
N_ZB_A = 3
GATE_OFF = 2 * ML_HEADS * ML_DQK + 2 * ML_HEADS * ML_DV

f32 = jnp.float32
bf16 = jnp.bfloat16


def _sigmoid(x):
    return 0.5 * jnp.tanh(0.5 * x) + 0.5


def _log_sigmoid(x):
    return jnp.minimum(x, 0.0) - jnp.log1p(jnp.exp(-jnp.abs(x)))


def _gelu_tanh(x):
    return 0.5 * x * (1.0 + jnp.tanh(np.sqrt(2.0 / np.pi) * (x + 0.044715 * (x * x * x))))


def _rms(x, g):
    return x * lax.rsqrt(jnp.mean(x * x, axis=-1, keepdims=True) + EPS) * g


def _dot(a, b):
    return jnp.dot(a, b, preferred_element_type=f32)


def _dot_nt(a, b):
    return lax.dot_general(a, b, (((1,), (1,)), ((), ())), preferred_element_type=f32)


def _dot_tn(a, b):
    return lax.dot_general(a, b, (((0,), (0,)), ((), ())), preferred_element_type=f32)


def _in_proj_kernel(x_ref, g_ref, wa_ref, wb_ref, wg_ref, gb_ref, gmg_ref, z_ref, zg_ref, zgt_ref, *, real_steps):
    if real_steps is not None:
        @pl.when(pl.program_id(0) >= real_steps)
        def _():
            z_ref[...] = jnp.zeros_like(z_ref)
            zg_ref[...] = jnp.zeros_like(zg_ref)
            zgt_ref[...] = jnp.zeros_like(zgt_ref)

        pl.when(pl.program_id(0) < real_steps)(functools.partial(
            _in_proj_rows, x_ref, g_ref, wa_ref, wb_ref, wg_ref, gb_ref, gmg_ref, z_ref, zg_ref, zgt_ref))
    else:
        _in_proj_rows(x_ref, g_ref, wa_ref, wb_ref, wg_ref, gb_ref, gmg_ref, z_ref, zg_ref, zgt_ref)


def _in_proj_rows(x_ref, g_ref, wa_ref, wb_ref, wg_ref, gb_ref, gmg_ref, z_ref, zg_ref, zgt_ref):
    tm = x_ref.shape[0]
    subs = [slice(r, r + IN_SUB) for r in range(0, tm, IN_SUB)]
    xn = [None] * len(subs)

    def prepare(s):
        rows = subs[s]
        xn[s] = _rms(x_ref[rows, :], g_ref[...]).astype(bf16)
        zg = _dot(xn[s], wg_ref[...]) + gb_ref[...]
        lane = lax.broadcasted_iota(jnp.int32, zg.shape, 1)
        zg = jnp.where((lane >= ML_HEADS) & (lane < 2 * ML_HEADS), _log_sigmoid(zg), zg)
        zg_ref[rows, :] = zg
        zgt_ref[:, rows] = zg.T[:2 * ML_HEADS, :]

    def gelu_group_rms(a, g):
        return _rms(_gelu_tanh(a), gmg_ref[:, g * GM_DG:(g + 1) * GM_DG])

    act = {ZB_QK: None, ZB_V: None, ZB_OG: _sigmoid, ZB_GA: _sigmoid, ZB_GB: _sigmoid,
           ZB_U: _gelu_tanh, ZB_GV: gelu_group_rms}
    per_blk = D_MODEL // GM_DG

    def emit(blk, s):
        for g in range(per_blk):
            cols = slice(blk * D_MODEL + g * GM_DG, blk * D_MODEL + (g + 1) * GM_DG)
            if blk < N_ZB_A:
                a = _dot(xn[s], wa_ref[:, cols])
            else:
                a = _dot(xn[s], wb_ref[:, cols.start - N_ZB_A * D_MODEL:cols.stop - N_ZB_A * D_MODEL])
            if blk == ZB_GV:
                a = gelu_group_rms(a, g)
            elif act[blk] is not None:
                a = act[blk](a)
            z_ref[subs[s], cols] = a.astype(z_ref.dtype)

    for s in range(len(subs)):
        prepare(s)
        emit(0, s)
    for blk in range(1, N_ZB):
        for s in range(len(subs)):
            emit(blk, s)


def _in_proj(x, g, w_a, w_b, w_gate, gate_bias, gm_g, tm, z_dtype, n_real):
    n = x.shape[0]
    const = lambda shape: pl.BlockSpec(shape, lambda i: (0, 0), pipeline_mode=pl.Buffered(1))
    return pl.pallas_call(
        functools.partial(_in_proj_kernel, real_steps=None if n_real == n else n_real // tm),
        grid=(n // tm,),
        in_specs=[
            pl.BlockSpec((tm, D_MODEL), lambda i: (i, 0)),
            const((1, D_MODEL)),
            const((D_MODEL, N_ZB_A * D_MODEL)),
            const((D_MODEL, (N_ZB - N_ZB_A) * D_MODEL)),
            const((D_MODEL, LANES)),
            const((1, LANES)),
            const((1, D_MODEL)),
        ],
        out_specs=[
            pl.BlockSpec((tm, N_ZB * D_MODEL), lambda i: (i, 0)),
            pl.BlockSpec((tm, LANES), lambda i: (i, 0)),
            pl.BlockSpec((2 * ML_HEADS, tm), lambda i: (0, i)),
        ],
        out_shape=[
            jax.ShapeDtypeStruct((n, N_ZB * D_MODEL), z_dtype),
            jax.ShapeDtypeStruct((n, LANES), f32),
            jax.ShapeDtypeStruct((2 * ML_HEADS, n), f32),
        ],
        compiler_params=pltpu.CompilerParams(
            dimension_semantics=("parallel",), vmem_limit_bytes=VMEM_LIMIT),
        name="in_proj",
    )(x, g, w_a, w_b, w_gate, gate_bias, gm_g)


def _mixer_core_kernel(zqk_ref, zv_ref, zog_ref, zu_ref, zgv_ref, zg_ref, zgt_ref,
                       c0_ref, n0_ref, m0_ref, mlg_ref, wsbd_ref, bcol_ref, eye_ref,
                       hml_ref, hgm_ref, c_ref, n_ref, m_ref, *, nblk, nb, L, valid, single_chunk):
    if single_chunk:
        cs_ref, ns_ref, ms_ref = c0_ref, n0_ref, m0_ref
    else:
        cs_ref, ns_ref, ms_ref = c_ref, n_ref, m_ref

        @pl.when(pl.program_id(1) == 0)
        def _():
            c_ref[...] = c0_ref[...]
            n_ref[...] = n0_ref[...]
            m_ref[...] = m0_ref[...]

    seq_bits = nb.bit_length() - 1
    seq_of = lambda p: p & (nb - 1)
    pos_of = lambda p: p >> seq_bits
    pp = lax.broadcasted_iota(jnp.int32, (ROWS, ROWS), 0)
    qq = lax.broadcasted_iota(jnp.int32, (ROWS, ROWS), 1)
    same = seq_of(pp) == seq_of(qq)
    p_col = lax.broadcasted_iota(jnp.int32, (ROWS, 1), 0)
    p_row = lax.broadcasted_iota(jnp.int32, (1, ROWS), 1)
    col_ok = pos_of(p_col) < valid
    row_ok = pos_of(p_row) < valid
    causal = same & (qq <= pp) & row_ok
    upper = same & (pp <= qq)
    scale = ML_DQK ** -0.5
    one_seq = nb == 1

    def rows_of(ref, r, cols):
        if one_seq:
            return ref[r, :, cols]
        return ref[:, :, cols].reshape(ROWS, cols.stop - cols.start)

    def per_row(vec):
        return jnp.concatenate([vec] * L, axis=0)

    stores = []
    chains = [(r, h) for r in range(nblk) for h in range(ML_HEADS)]
    each = lambda fn: [fn(i, r, h) for i, (r, h) in enumerate(chains)]

    q = each(lambda i, r, h: rows_of(zqk_ref, r, slice(h * ML_DQK, (h + 1) * ML_DQK)))
    k = each(lambda i, r, h: rows_of(zqk_ref, r, slice((ML_HEADS + h) * ML_DQK, (ML_HEADS + h + 1) * ML_DQK)))
    v = each(lambda i, r, h: rows_of(zv_ref, r, slice(h * ML_DV, (h + 1) * ML_DV)))
    qc = [x.astype(bf16) for x in q]
    kc = [x.astype(bf16) for x in k]
    c0 = each(lambda i, r, h: cs_ref[r * nb:(r + 1) * nb, h])
    if one_seq:
        m0 = each(lambda i, r, h: ms_ref[r, h:h + 1, 0:1])
    else:
        m0 = each(lambda i, r, h: per_row(ms_ref[:, h, 0:1]))
        n0 = each(lambda i, r, h: per_row(ns_ref[:, h, :]))
    logi_col = each(lambda i, r, h: rows_of(zg_ref, r, slice(h, h + 1)))
    logi_row = each(lambda i, r, h: zgt_ref[r, h:h + 1, :])
    logf_col = each(lambda i, r, h: jnp.where(col_ok, rows_of(zg_ref, r, slice(ML_HEADS + h, ML_HEADS + h + 1)), 0.0))
    logf_row = each(lambda i, r, h: jnp.where(row_ok, zgt_ref[r, ML_HEADS + h:ML_HEADS + h + 1, :], 0.0))

    if one_seq:
        n_rows = each(lambda i, r, h: jnp.broadcast_to(ns_ref[r, h:h + 1, :], (N_PAD_ROWS, ML_DQK)).astype(bf16))
        qkc = each(lambda i, r, h: _dot_nt(
            qc[i], jnp.concatenate([kc[i], c0[i].reshape(ML_DV, ML_DQK).astype(bf16), n_rows[i]], axis=0)))
    else:
        qkc = each(lambda i, r, h: _dot_nt(
            qc[i], jnp.concatenate([kc[i], c0[i].reshape(nb * ML_DV, ML_DQK).astype(bf16)], axis=0)))
    b_col = each(lambda i, r, h: jnp.sum(jnp.where(causal, logf_row[i], 0.0), axis=1, keepdims=True))
    b_row = each(lambda i, r, h: jnp.sum(jnp.where(upper, logf_col[i], 0.0), axis=0, keepdims=True))
    if one_seq:
        b_last_col = each(lambda i, r, h: jnp.sum(logf_row[i], axis=1, keepdims=True))
        b_last_row = b_last_col
    else:
        b_last_col = each(lambda i, r, h: jnp.sum(jnp.where(same, logf_row[i], 0.0), axis=1, keepdims=True))
        b_last_row = each(lambda i, r, h: jnp.sum(jnp.where(same, logf_col[i], 0.0), axis=0, keepdims=True))
    d = each(lambda i, r, h: jnp.where(causal, b_col[i] - b_row[i] + logi_row[i], NEG))
    inter = each(lambda i, r, h: b_col[i] + m0[i])
    m_col = each(lambda i, r, h: jnp.maximum(inter[i], jnp.max(d[i], axis=1, keepdims=True)))
    w_intra = each(lambda i, r, h: jnp.exp(d[i] - m_col[i]) * scale)
    w_inter = each(lambda i, r, h: jnp.exp(inter[i] - m_col[i]) * scale)
    s = each(lambda i, r, h: qkc[i][:, :ROWS] * w_intra[i])
    if one_seq:
        q_mem = each(lambda i, r, h: qkc[i][:, ROWS:ROWS + ML_DV])
    else:
        seq_col = seq_of(p_col)
        q_mem = each(lambda i, r, h: functools.reduce(jnp.add, [
            jnp.where(seq_col == b, qkc[i][:, ROWS + b * ML_DV:ROWS + (b + 1) * ML_DV], 0.0) for b in range(nb)]))
    if one_seq:
        ones_blk = jnp.ones((ROWS, LANES), bf16)
        sv = each(lambda i, r, h: _dot(s[i].astype(bf16), jnp.concatenate([v[i].astype(bf16), ones_blk], axis=1)))
        num = each(lambda i, r, h: sv[i][:, :ML_DV] + w_inter[i] * q_mem[i])
        s_sum = each(lambda i, r, h: sv[i][:, ML_DV:ML_DV + 1])
        q_n = each(lambda i, r, h: qkc[i][:, ROWS + ML_DV:ROWS + ML_DV + 1])
    else:
        num = each(lambda i, r, h: _dot(s[i].astype(bf16), v[i].astype(bf16)) + w_inter[i] * q_mem[i])
        s_sum = each(lambda i, r, h: jnp.sum(s[i], axis=1, keepdims=True))
        q_n = each(lambda i, r, h: jnp.sum(q[i].astype(f32) * n0[i], axis=1, keepdims=True))
    den = each(lambda i, r, h: s_sum[i] + w_inter[i] * q_n[i])
    ones = jnp.ones((ML_DV, LANES), bf16)

    def head_norm(x, g):
        if not one_seq:
            return _rms(x, g)
        ssq = _dot((x * x).astype(bf16), ones)
        scale = lax.rsqrt(ssq * (1.0 / ML_DV) + EPS)
        return x * jnp.concatenate([scale] * (ML_DV // LANES), axis=1) * g

    hh = each(lambda i, r, h: head_norm(num[i] / jnp.maximum(jnp.abs(den[i]), jnp.exp(-m_col[i])), mlg_ref[h:h + 1, :]))
    def token_store(ref, r, cols, val):
        if one_seq:
            stores.append((ref, (r, slice(None), cols), val.astype(ref.dtype)))
        else:
            stores.append((ref, (slice(None), slice(None), cols),
                           val.astype(ref.dtype).reshape(L, nb, cols.stop - cols.start)))

    for i, (r, h) in enumerate(chains):
        cols = slice(h * ML_DV, (h + 1) * ML_DV)
        token_store(hml_ref, r, cols, rows_of(zog_ref, r, cols).astype(f32) * hh[i])

    wend_col = each(lambda i, r, h: jnp.where(col_ok, b_last_col[i] - b_col[i] + logi_col[i], NEG))
    wend_row = each(lambda i, r, h: jnp.where(row_ok, b_last_row[i] - b_row[i] + logi_row[i], NEG))
    if one_seq:
        m_new = each(lambda i, r, h: jnp.maximum(b_last_col[i] + m0[i], jnp.max(wend_row[i], axis=1, keepdims=True)))
    else:
        m_new = each(lambda i, r, h: jnp.maximum(
            b_last_col[i] + m0[i], jnp.max(jnp.where(same, wend_row[i], NEG), axis=1, keepdims=True)))
    decay = each(lambda i, r, h: jnp.exp(b_last_col[i] + m0[i] - m_new[i]))
    wend = each(lambda i, r, h: jnp.exp(wend_col[i] - m_new[i]))
    vw = each(lambda i, r, h: (v[i].astype(f32) * wend[i]).astype(bf16))
    kw = each(lambda i, r, h: k[i].astype(f32) * wend[i])
    if nb == 1:
        upd = each(lambda i, r, h: _dot_tn(vw[i], kc[i]))
    else:
        vw_t = each(lambda i, r, h: _dot_nt(eye_ref[...], vw[i]).astype(bf16))
        seq_of_lane = seq_of(lax.broadcasted_iota(jnp.int32, (ML_DV, ROWS), 1))
        upd = each(lambda i, r, h: _dot(jnp.concatenate(
            [jnp.where(seq_of_lane == b, vw_t[i], jnp.zeros_like(vw_t[i])) for b in range(nb)], axis=0), kc[i]))
    if one_seq:
        kw_sum = each(lambda i, r, h: jnp.sum(kw[i], axis=0, keepdims=True))
    else:
        kw_sum = each(lambda i, r, h: jnp.sum(kw[i].reshape(L, nb, ML_DQK), axis=0))
    for i, (r, h) in enumerate(chains):
        for b in range(nb):
            slot = r * nb + b
            dec = decay[i][b:b + 1, :]
            stores.append((c_ref, (slot, h), dec * c0[i][b] + upd[i][b * ML_DV:(b + 1) * ML_DV]))
            stores.append((n_ref, (slot, slice(h, h + 1), slice(None)),
                           dec * ns_ref[slot, h:h + 1, :] + kw_sum[i][b:b + 1, :]))
            stores.append((m_ref, (slot, slice(h, h + 1), slice(None)),
                           jnp.broadcast_to(m_new[i][b:b + 1, :], (1, LANES))))

    for r in range(nblk):
        gcols = [slice(g * GM_DG, (g + 1) * GM_DG) for g in range(GM_GROUPS)]
        gv = jnp.concatenate([rows_of(zgv_ref, r, c).astype(bf16) for c in gcols], axis=0)
        mixed = _dot(wsbd_ref[...], gv) + bcol_ref[...]
        for g, c in enumerate(gcols):
            token_store(hgm_ref, r, c, rows_of(zu_ref, r, c).astype(f32) * mixed[g * ROWS:(g + 1) * ROWS])

    for ref, idx, val in stores:
        ref[idx] = val


def _mixer_core(z, zg, zgt, c0, n0, m0, ml_g, ws_bd, b_col, *, nblk, nb, L, valid, h_dtype):
    n = z.shape[0]
    nbatch = c0.shape[0]
    groups = nbatch // nb
    nc = n // groups // ROWS
    if nb == 1:
        tok_shape = (groups, nc, ROWS)
        zgt4 = zgt.reshape(2 * ML_HEADS, groups, nc, ROWS).transpose(1, 2, 0, 3)
        tspec = lambda width, blk: pl.BlockSpec((nblk, None, ROWS, width), lambda b, c: (b, c, 0, blk))
    else:
        assert nc == 1 and nblk == 1
        tok_shape = (L, groups, nb)
        zgt4 = zgt.reshape(2 * ML_HEADS, L, groups, nb).transpose(2, 0, 1, 3).reshape(groups, 1, 2 * ML_HEADS, ROWS)
        tspec = lambda width, blk: pl.BlockSpec((L, None, nb, width), lambda b, c: (0, b, 0, blk))
    z4 = z.reshape(*tok_shape, N_ZB * D_MODEL)
    zg4 = zg.reshape(*tok_shape, LANES)
    zspec = lambda blk: tspec(D_MODEL, blk)
    full = lambda shape: pl.BlockSpec(shape, lambda b, c: (0,) * len(shape))
    state_specs = [
        pl.BlockSpec((nblk * nb, ML_HEADS, ML_DV, ML_DQK), lambda b, c: (b, 0, 0, 0)),
        pl.BlockSpec((nblk * nb, ML_HEADS, ML_DQK), lambda b, c: (b, 0, 0)),
        pl.BlockSpec((nblk * nb, ML_HEADS, LANES), lambda b, c: (b, 0, 0)),
    ]
    tok_spec = zspec(0)
    eye = jnp.eye(ML_DV, dtype=bf16)
    hml, hgm, c1, n1, m1 = pl.pallas_call(
        functools.partial(_mixer_core_kernel, nblk=nblk, nb=nb, L=L, valid=valid, single_chunk=(nc == 1)),
        grid=(groups // nblk, nc),
        in_specs=[zspec(ZB_QK), zspec(ZB_V), zspec(ZB_OG), zspec(ZB_U), zspec(ZB_GV),
                  tspec(LANES, 0),
                  pl.BlockSpec((nblk, None, 2 * ML_HEADS, ROWS), lambda b, c: (b, c, 0, 0)),
                  *state_specs,
                  full((ML_HEADS, ML_DV)), full((GM_GROUPS * ROWS, GM_GROUPS * ROWS)), full((GM_GROUPS * ROWS, 1)),
                  full((ML_DV, ML_DV))],
        out_specs=[tok_spec, tok_spec] + state_specs,
        out_shape=[
            jax.ShapeDtypeStruct((*tok_shape, D_MODEL), h_dtype),
            jax.ShapeDtypeStruct((*tok_shape, D_MODEL), h_dtype),
            jax.ShapeDtypeStruct((nbatch, ML_HEADS, ML_DV, ML_DQK), f32),
            jax.ShapeDtypeStruct((nbatch, ML_HEADS, ML_DQK), f32),
            jax.ShapeDtypeStruct((nbatch, ML_HEADS, LANES), f32),
        ],
        compiler_params=pltpu.CompilerParams(
            dimension_semantics=("parallel", "arbitrary"), vmem_limit_bytes=VMEM_LIMIT),
        name="mixer_core",
    )(z4, z4, z4, z4, z4, zg4, zgt4, c0, n0, m0, ml_g, ws_bd, b_col, eye)
    return hml.reshape(n, D_MODEL), hgm.reshape(n, D_MODEL), c1, n1, m1


def _gmlp_mixing(gm_ws, gm_bs, L):
    reps = ROWS // L
    tril = jnp.tril(jnp.ones((L, L), bool))
    blocks = [jnp.kron(jnp.where(tril, gm_ws[g, :L, :L], 0.0), jnp.eye(reps, dtype=f32)) for g in range(GM_GROUPS)]
    ws_bd = jax.scipy.linalg.block_diag(*blocks).astype(bf16)
    b_col = jnp.concatenate([jnp.repeat(gm_bs[g, :L], reps) for g in range(GM_GROUPS)])[:, None]
    return ws_bd, b_col


def _proj_router_kernel(hml_ref, hgm_ref, sga_ref, sgb_ref, x_ref, pa_ref, pb_ref, wo_ref, g_ref, wr_ref, br_ref,
                        x2_ref, xn_ref, comb_ref):
    tm = x_ref.shape[0]
    subs = [slice(r, r + PROJ_SUB) for r in range(0, tm, PROJ_SUB)]
    ab, xn = {}, {}

    def stage_branches(i):
        r = subs[i]
        ab[i] = (_dot(hml_ref[r, :].astype(bf16), pa_ref[...]), _dot(hgm_ref[r, :].astype(bf16), pb_ref[...]))

    def stage_merge(i):
        r = subs[i]
        a, b = ab.pop(i)
        merged = sga_ref[r, :].astype(f32) * a + sgb_ref[r, :].astype(f32) * b
        x2 = x_ref[r, :] + _dot(merged.astype(bf16), wo_ref[...])
        x2_ref[r, :] = x2
        xn[i] = _rms(x2, g_ref[...]).astype(bf16)
        xn_ref[r, :] = xn[i]

    def stage_route(i):
        lg = _dot(xn.pop(i), wr_ref[...]) + br_ref[...]
        lane = lax.broadcasted_iota(jnp.int32, lg.shape, 1).astype(f32)
        cmask = (lane >= MOE_EXPERTS) & (lane < MOE_EXPERTS + MOE_GROUPS)
        cl = jnp.where(cmask, lg, NEG)
        cmax = jnp.max(cl, axis=1, keepdims=True)
        p_grp = 1.0 / jnp.sum(jnp.where(cmask, jnp.exp(cl - cmax), 0.0), axis=1, keepdims=True)
        grp = jnp.min(jnp.where(cl == cmax, lane, 2.0 * LANES), axis=1, keepdims=True) - MOE_EXPERTS
        fmask = (lane >= grp * MOE_PER_GROUP) & (lane < (grp + 1.0) * MOE_PER_GROUP)
        fl = jnp.where(fmask, lg, NEG)
        v1 = jnp.max(fl, axis=1, keepdims=True)
        i1 = jnp.min(jnp.where(fl == v1, lane, 2.0 * LANES), axis=1, keepdims=True)
        fl2 = jnp.where(lane == i1, NEG, fl)
        v2 = jnp.max(fl2, axis=1, keepdims=True)
        i2 = jnp.min(jnp.where(fl2 == v2, lane, 2.0 * LANES), axis=1, keepdims=True)
        e2 = jnp.exp(v2 - v1)
        g1 = p_grp / (1.0 + e2)
        g2 = p_grp * e2 / (1.0 + e2)
        comb_ref[subs[i], :] = (jnp.where(lane == i1, g1, 0.0) + jnp.where(lane == i2, g2, 0.0)
                                + jnp.where(lane == GRP_LANE, grp, 0.0))

    n = len(subs)
    for step in range(n + 2):
        if step < n:
            stage_branches(step)
        if 0 <= step - 1 < n:
            stage_merge(step - 1)
        if 0 <= step - 2 < n:
            stage_route(step - 2)


def _proj_router(hml, hgm, z, x, n, pa, pb, wo, g, wr, br, tm):
    row = lambda blk=0: pl.BlockSpec((tm, D_MODEL), lambda i, blk=blk: (i, blk))
    const = lambda shape: pl.BlockSpec(shape, lambda i: (0, 0), pipeline_mode=pl.Buffered(1))
    wfull = const((D_MODEL, D_MODEL))
    return pl.pallas_call(
        _proj_router_kernel,
        grid=(n // tm,),
        in_specs=[row(), row(), row(ZB_GA), row(ZB_GB), row(), wfull, wfull, wfull,
                  const((1, D_MODEL)), const((D_MODEL, LANES)), const((1, LANES))],
        out_specs=[row(), row(), pl.BlockSpec((tm, LANES), lambda i: (i, 0))],
        out_shape=[jax.ShapeDtypeStruct((n, D_MODEL), f32),
                   jax.ShapeDtypeStruct((n, D_MODEL), bf16),
                   jax.ShapeDtypeStruct((n, LANES), f32)],
        compiler_params=pltpu.CompilerParams(
            dimension_semantics=("parallel",), vmem_limit_bytes=VMEM_LIMIT),
        name="proj_router",
    )(hml, hgm, z, z, x, pa, pb, wo, g, wr, br)


def _moe_tile_cost(rows):
    return max(MOE_TILE_FLAT_COST, MOE_TILE_BASE_COST + MOE_TILE_ROW_COST * rows)


def _moe_final_kernel(xn_ref, comb_ref, x2_ref, ltri_ref, wg_ref, wu_ref, wd_ref, g_ref, y_ref,
                      chl_ref, gp_col_ref, gp_row_ref, pos_col_ref, pos_row_ref):
    grp = pl.program_id(1)
    gf = grp.astype(f32)
    tb = xn_ref.shape[0]

    @pl.when(grp == 0)
    def _():
        y_ref[...] = jnp.zeros_like(y_ref)
        comb = comb_ref[...]
        lane = lax.broadcasted_iota(jnp.int32, comb.shape, 1)
        gcol = comb[:, GRP_LANE:GRP_LANE + 1]
        onehot = jnp.where(lane.astype(f32) == gcol, 1.0, 0.0)
        before = _dot(ltri_ref[...], onehot.astype(bf16))
        pos = jnp.sum(onehot * before, axis=1, keepdims=True)
        gp = jnp.where(lane == 0, gcol, 0.0) + jnp.where(lane == 1, pos, 0.0)
        gp_col_ref[...] = gp
        gp_row_ref[...] = gp.T[:8, :]
        chi = comb.astype(bf16)
        chl_ref[:, :LANES] = chi
        chl_ref[:, LANES:] = (comb - chi.astype(f32)).astype(bf16)

    in_grp_col = gp_col_ref[:, 0:1] == gf
    pos_col_ref[...] = jnp.broadcast_to(jnp.where(in_grp_col, gp_col_ref[:, 1:2], -1.0), pos_col_ref.shape)
    pos_row_ref[...] = jnp.broadcast_to(jnp.where(gp_row_ref[0:1, :] == gf, gp_row_ref[1:2, :], -1.0), (8, tb))
    cnt = jnp.sum(jnp.where(in_grp_col, 1.0, 0.0)).astype(jnp.int32)

    def run_tiles(rows):
        def tile(t, carry):
            base = (t * rows).astype(f32)
            r_iota = lax.broadcasted_iota(jnp.int32, (rows, tb), 0).astype(f32)
            gather = jnp.where(pos_row_ref[0:1, :] - base == r_iota, 1.0, 0.0).astype(bf16)
            x = _dot(gather, xn_ref[...]).astype(bf16)
            c2 = _dot(gather, chl_ref[...])
            c = c2[:, :LANES] + c2[:, LANES:]
            lane = lax.broadcasted_iota(jnp.int32, c.shape, 1)
            hid = []
            for e in range(MOE_PER_GROUP):
                ce = jnp.sum(jnp.where(lane == grp * MOE_PER_GROUP + e, c, 0.0), axis=1, keepdims=True)
                a = _dot(x, wg_ref[e])
                u = _dot(x, wu_ref[e])
                hid.append((a * _sigmoid(a) * u * ce).astype(bf16))
            y = _dot(jnp.concatenate(hid, axis=1), wd_ref[...]).astype(bf16)
            l_iota = lax.broadcasted_iota(jnp.int32, (tb, rows), 1).astype(f32)
            scatter = jnp.where(pos_col_ref[:, :rows] - base == l_iota, 1.0, 0.0).astype(bf16)
            y_ref[...] += _dot(scatter, y)
            return carry

        lax.fori_loop(0, (cnt + rows - 1) // rows, tile, 0)

    totals = [((cnt + rows - 1) // rows) * _moe_tile_cost(rows) for rows in MOE_TILES]
    best = functools.reduce(jnp.minimum, totals)
    taken = False
    for rows, total in zip(MOE_TILES, totals):
        pick = (total == best) & jnp.logical_not(taken)
        pl.when(pick)(functools.partial(run_tiles, rows))
        taken = taken | (total == best)

    @pl.when(grp == MOE_GROUPS - 1)
    def _():
        y_ref[...] = _rms(x2_ref[...] + y_ref[...], g_ref[...])


def _moe_final(xn, comb, x2, wg, wu, wd, g, tb):
    n = x2.shape[0]
    ltri = jnp.tril(jnp.ones((tb, tb), bf16), -1)
    return pl.pallas_call(
        _moe_final_kernel,
        grid=(n // tb, MOE_GROUPS),
        in_specs=[pl.BlockSpec((tb, D_MODEL), lambda i, j: (i, 0)),
                  pl.BlockSpec((tb, LANES), lambda i, j: (i, 0)),
                  pl.BlockSpec((tb, D_MODEL), lambda i, j: (i, 0)),
                  pl.BlockSpec((tb, tb), lambda i, j: (0, 0)),
                  pl.BlockSpec((MOE_PER_GROUP, D_MODEL, MOE_HIDDEN), lambda i, j: (j, 0, 0)),
                  pl.BlockSpec((MOE_PER_GROUP, D_MODEL, MOE_HIDDEN), lambda i, j: (j, 0, 0)),
                  pl.BlockSpec((None, MOE_PER_GROUP * MOE_HIDDEN, D_MODEL), lambda i, j: (j, 0, 0)),
                  pl.BlockSpec((1, D_MODEL), lambda i, j: (0, 0))],
        out_specs=pl.BlockSpec((tb, D_MODEL), lambda i, j: (i, 0)),
        out_shape=jax.ShapeDtypeStruct((n, D_MODEL), f32),
        scratch_shapes=[pltpu.VMEM((tb, 2 * LANES), bf16),
                        pltpu.VMEM((tb, LANES), f32), pltpu.VMEM((8, tb), f32),
                        pltpu.VMEM((tb, pl.cdiv(max(MOE_TILES), LANES) * LANES), f32), pltpu.VMEM((8, tb), f32)],
        compiler_params=pltpu.CompilerParams(
            dimension_semantics=("parallel", "arbitrary"), vmem_limit_bytes=VMEM_LIMIT),
        name="moe_final",
    )(xn, comb, x2, ltri, wg, wu, wd.reshape(MOE_GROUPS, MOE_PER_GROUP * MOE_HIDDEN, D_MODEL), g)


def _layer(x3, state, lw, *, prompt):
    nbatch, t_real, _ = x3.shape
    if prompt:
        t = t_real
        L, valid, z_dtype, tm, nblk = ML_CHUNK, ML_CHUNK, bf16, IN_TILE, MIXER_BLOCKS
        x = x3.reshape(nbatch * t, D_MODEL)
    else:
        t = SAMPLE_PAD_T
        L, valid, z_dtype, tm, nblk = t, t_real, bf16, IN_SUB, 1
        x = jnp.pad(x3.transpose(1, 0, 2), ((0, t - t_real), (0, 0), (0, 0))).reshape(t * nbatch, D_MODEL)
    n = nbatch * t
    nb = ROWS // L

    n_real = nbatch * t_real
    z, zg, zgt = _in_proj(x, lw["norm_mix_g"], lw["w_a"], lw["w_b"], lw["w_gate"], lw["gate_bias"],
                          lw["gm_norm_g"], tm, z_dtype, n_real)
    c0, n0, m0 = state
    m0 = jnp.broadcast_to(m0[:, :, None], (nbatch, ML_HEADS, LANES))
    ws_bd, b_col = _gmlp_mixing(lw["gm_ws"], lw["gm_bs"], L)
    hml, hgm, c1, n1, m1 = _mixer_core(z, zg, zgt, c0, n0, m0, lw["ml_norm_g"],
                                       ws_bd, b_col, nblk=nblk, nb=nb, L=L, valid=valid, h_dtype=z_dtype)
    x2, xn, comb = _proj_router(hml, hgm, z, x, n_real, lw["p_a"], lw["p_b"], lw["w_out"],
                                lw["norm_ffn_g"], lw["w_router"], lw["b_router"], min(n_real, PROJ_TILE))
    y = _moe_final(xn, comb, x2, lw["e_wg"], lw["e_wu"], lw["e_wd"], lw["out_g"], min(n_real, MOE_BLOCK))
    if prompt:
        return y.reshape(nbatch, t_real, D_MODEL), (c1, n1, m1[:, :, 0]), None
    unrows = lambda a: a.reshape(t_real, nbatch, D_MODEL).transpose(1, 0, 2)
    return unrows(y), (c1, n1, m1[:, :, 0]), unrows(z[:n_real, ZB_GV * D_MODEL:(ZB_GV + 1) * D_MODEL].astype(f32))


def kernel(x_prompt, x_sample, state_mlstm_C, state_mlstm_n, state_mlstm_m, norm_mix_g, w_in, ml_b_i, ml_b_f, ml_norm_g, gm_norm_g, gm_ws, gm_bs, p_a, p_b, w_out, norm_ffn_g, rc_w, rc_b, rf_w, rf_b, e_wg, e_wu, e_wd, final_norm_g):
    depth = w_in.shape[0]
    assert depth == 1, "the final norm is fused into the last layer's MoE kernel; only depth 1 is wired up"
    nbp = x_prompt.shape[0]
    nbs, ts, _ = x_sample.shape

    def layer_weights(l):
        w = w_in[l]
        gates = w[:, GATE_OFF:GATE_OFF + 2 * ML_HEADS]
        return dict(
            norm_mix_g=norm_mix_g[l][None, :],
            w_a=w[:, :GATE_OFF].astype(bf16), w_b=w[:, GATE_OFF + 2 * ML_HEADS:].astype(bf16),
            w_gate=jnp.pad(gates, ((0, 0), (0, LANES - 2 * ML_HEADS))).astype(bf16),
            gate_bias=jnp.pad(jnp.concatenate([ml_b_i[l], ml_b_f[l]]), (0, LANES - 2 * ML_HEADS))[None, :],
            ml_norm_g=ml_norm_g[l],
            gm_norm_g=gm_norm_g[l].reshape(1, GM_GROUPS * GM_DG),
            gm_ws=gm_ws[l], gm_bs=gm_bs[l],
            p_a=p_a[l].astype(bf16), p_b=p_b[l].astype(bf16), w_out=w_out[l].astype(bf16),
            norm_ffn_g=norm_ffn_g[l][None, :],
            w_router=jnp.pad(
                jnp.concatenate([rf_w[l].transpose(1, 0, 2).reshape(D_MODEL, MOE_EXPERTS), rc_w[l]], axis=1),
                ((0, 0), (0, LANES - MOE_EXPERTS - MOE_GROUPS))).astype(bf16),
            b_router=jnp.pad(jnp.concatenate([rf_b[l].reshape(-1), rc_b[l]]),
                             (0, LANES - MOE_EXPERTS - MOE_GROUPS))[None, :],
            e_wg=e_wg[l].astype(bf16), e_wu=e_wu[l].astype(bf16), e_wd=e_wd[l].astype(bf16),
            out_g=final_norm_g[None, :],
        )

    lw = layer_weights(0)

    zero_state = (jnp.zeros((nbp, ML_HEADS, ML_DV, ML_DQK), f32), jnp.zeros((nbp, ML_HEADS, ML_DQK), f32),
                  jnp.zeros((nbp, ML_HEADS), f32))
    y_p, (c_p, n_p, m_p), _ = _layer(x_prompt, zero_state, lw, prompt=True)

    y_s, (c_s, n_s, m_s), v_s = _layer(x_sample, (state_mlstm_C[0], state_mlstm_n[0], state_mlstm_m[0]),
                                       lw, prompt=False)
    v_s = v_s.reshape(nbs, ts, GM_GROUPS, GM_DG)
    return (y_p, y_s, c_p[None], n_p[None], m_p[None], c_s[None], n_s[None], m_s[None], v_s[None])
```
